```python
import jax, jax.numpy as jnp
from jax import lax

D_MODEL = 1024
BATCH = 2
SEQ = 16384
DEPTH = 4

GRID_W = 64
ROPE_THETA = 10000.0
EPS = 1e-6
Q_BLOCK = 128

POOL_DIM = D_MODEL // 4
POOL_WINDOWS = (2, 4, 8, 16)
POOL_CH = POOL_DIM // len(POOL_WINDOWS)

HEAD_DIM = 64
GQA_HEADS = 6
GQA_KV_HEADS = 2
GQA_GROUP = GQA_HEADS // GQA_KV_HEADS

MLA_HEADS = 6
MLA_NOPE_DIM = 64
MLA_ROPE_DIM = 32
MLA_QK_DIM = MLA_NOPE_DIM + MLA_ROPE_DIM
MLA_V_DIM = 64
MLA_Q_RANK = 256
MLA_KV_RANK = 256

OFF_GQA_Q = POOL_DIM
OFF_GQA_K = OFF_GQA_Q + GQA_HEADS * HEAD_DIM
OFF_GQA_V = OFF_GQA_K + GQA_KV_HEADS * HEAD_DIM
OFF_MLA_Q = OFF_GQA_V + GQA_KV_HEADS * HEAD_DIM
OFF_MLA_KV = OFF_MLA_Q + MLA_Q_RANK
OFF_MLA_ROPE = OFF_MLA_KV + MLA_KV_RANK
IN_DIM = OFF_MLA_ROPE + MLA_ROPE_DIM

MIX_DIM = POOL_DIM + GQA_HEADS * HEAD_DIM + MLA_HEADS * MLA_V_DIM

N_EXPERTS = 16
EC_CAPACITY = 2
D_FF = 2048

kernel_name = 'hybrid_pool_gqa_mla_ec_moe_encoder'


def rmsnorm(x, g):
    xf = x.astype(jnp.float32)
    y = xf * lax.rsqrt(jnp.mean(xf * xf, axis=-1, keepdims=True) + EPS)
    return (y * g.astype(jnp.float32)).astype(x.dtype)


def modulate(h, shift, scale):
    return h * (1 + scale[:, None, :]) + shift[:, None, :]


def axial_rope_tables(n, d_rot):
    n_rows = n // GRID_W
    row = jnp.repeat(jnp.arange(n_rows, dtype=jnp.float32), GRID_W)
    col = jnp.tile(jnp.arange(GRID_W, dtype=jnp.float32), n_rows)
    n_freq = d_rot // 4
    inv_freq = ROPE_THETA ** (-jnp.arange(n_freq, dtype=jnp.float32) / n_freq)
    ang = jnp.concatenate([row[:, None] * inv_freq, col[:, None] * inv_freq], axis=-1)
    return jnp.cos(ang)[:, None, :], jnp.sin(ang)[:, None, :]


def apply_rope(x, cos, sin):
    xf = x.astype(jnp.float32).reshape(*x.shape[:-1], x.shape[-1] // 2, 2)
    x1, x2 = xf[..., 0], xf[..., 1]
    out = jnp.stack([x1 * cos - x2 * sin, x1 * sin + x2 * cos], axis=-1)
    return out.reshape(x.shape).astype(x.dtype)


def block_attention(q, k, v):
    b, hk, g, s, dq = q.shape
    dv = v.shape[-1]
    nb = s // Q_BLOCK
    scale = dq ** -0.5
    qb = q.reshape(b, hk, g, nb, Q_BLOCK, dq).transpose(3, 0, 1, 2, 4, 5)

    def one_block(qi):
        sc = jnp.einsum('bkgqd,bksd->bkgqs', qi, k, preferred_element_type=jnp.float32) * scale
        p = jax.nn.softmax(sc, axis=-1)
        return jnp.einsum('bkgqs,bksd->bkgqd', p.astype(v.dtype), v)

    o = lax.map(one_block, qb)
    return o.transpose(1, 0, 4, 2, 3, 5).reshape(b, s, hk * g * dv)


def pool_mixer(u, w, scale):
    b, s, _ = u.shape
    uf = u.astype(jnp.float32)
    t = jnp.arange(s)
    outs = []
    for gi, win in enumerate(POOL_WINDOWS):
        ug = uf[..., gi * POOL_CH:(gi + 1) * POOL_CH]
        lo = win // 2
        hi = win - 1 - lo
        cs = jnp.cumsum(jnp.pad(ug, ((0, 0), (lo + 1, hi), (0, 0))), axis=1)
        wsum = cs[:, win:] - cs[:, :s]
        cnt = (jnp.minimum(t + hi, s - 1) - jnp.maximum(t - lo, 0) + 1).astype(jnp.float32)
        outs.append(wsum / cnt[None, :, None] - ug)
    p = jnp.stack(outs, axis=2)
    y = jnp.einsum('bsgc,gcd->bsgd', p, w.astype(jnp.float32)).reshape(b, s, POOL_DIM)
    return (y * scale.astype(jnp.float32)).astype(u.dtype)


def gqa_mixer(zq, zk, zv, q_gain, k_gain, cos, sin):
    b, s, _ = zq.shape
    q = rmsnorm(zq.reshape(b, s, GQA_HEADS, HEAD_DIM), q_gain)
    k = rmsnorm(zk.reshape(b, s, GQA_KV_HEADS, HEAD_DIM), k_gain)
    q = apply_rope(q, cos, sin)
    k = apply_rope(k, cos, sin)
    v = zv.reshape(b, s, GQA_KV_HEADS, HEAD_DIM)
    q = q.reshape(b, s, GQA_KV_HEADS, GQA_GROUP, HEAD_DIM).transpose(0, 2, 3, 1, 4)
    return block_attention(q, k.transpose(0, 2, 1, 3), v.transpose(0, 2, 1, 3))


def mla_mixer(zq, zkv, zr, q_gain, kv_gain, w_uq, w_ukv, cos, sin):
    b, s, _ = zq.shape
    cq = rmsnorm(zq, q_gain)
    q = jnp.einsum('bsr,rh->bsh', cq, w_uq).reshape(b, s, MLA_HEADS, MLA_QK_DIM)
    q = jnp.concatenate([q[..., :MLA_NOPE_DIM], apply_rope(q[..., MLA_NOPE_DIM:], cos, sin)], axis=-1)
    ckv = rmsnorm(zkv, kv_gain)
    kv = jnp.einsum('bsr,rh->bsh', ckv, w_ukv).reshape(b, s, MLA_HEADS, MLA_NOPE_DIM + MLA_V_DIM)
    k_rope = apply_rope(zr[:, :, None, :], cos, sin)
    k = jnp.concatenate([kv[..., :MLA_NOPE_DIM],
                         jnp.broadcast_to(k_rope, (b, s, MLA_HEADS, MLA_ROPE_DIM))], axis=-1)
    v = kv[..., MLA_NOPE_DIM:]
    q = q.transpose(0, 2, 1, 3)[:, :, None]
    return block_attention(q, k.transpose(0, 2, 1, 3), v.transpose(0, 2, 1, 3))


def expert_choice_ffn(h, w_router, w_gate, w_up, w_down):
    b, s, _ = h.shape
    cap = (EC_CAPACITY * s) // N_EXPERTS
    logits = jnp.einsum('bsd,de->bse', h, w_router, preferred_element_type=jnp.float32)
    aff = jax.nn.softmax(logits, axis=-1)
    gate_vals, idx = lax.top_k(aff.transpose(0, 2, 1), cap)
    bidx = jnp.arange(b)[:, None, None]
    xe = h[bidx, idx]
    a = jnp.einsum('becd,edf->becf', xe, w_gate)
    u = jnp.einsum('becd,edf->becf', xe, w_up)
    ye = jnp.einsum('becf,efd->becd', jax.nn.silu(a) * u, w_down)
    ye = ye * gate_vals[..., None].astype(ye.dtype)
    return jnp.zeros_like(h).at[bidx, idx].add(ye)


def setup_inputs(seed: int = 0) -> dict:
    key = jax.random.key(seed)
    ks = jax.random.split(key, 24)
    nrm = jax.random.normal
    L = DEPTH
    return {
        'x': nrm(ks[0], (BATCH, SEQ, D_MODEL), jnp.float32),
        'c': nrm(ks[1], (BATCH, D_MODEL), jnp.float32),
        'w_mod': nrm(ks[2], (L, D_MODEL, 6 * D_MODEL), jnp.float32) * (0.5 * D_MODEL ** -0.5),
        'b_mod': 0.02 * nrm(ks[3], (L, 6 * D_MODEL), jnp.float32),
        'g_norm1': 1.0 + 0.02 * nrm(ks[4], (L, D_MODEL), jnp.float32),
        'w_in': nrm(ks[5], (L, D_MODEL, IN_DIM), jnp.float32) * D_MODEL ** -0.5,
        'pool_w': nrm(ks[6], (L, len(POOL_WINDOWS), POOL_CH, POOL_CH), jnp.float32) * POOL_CH ** -0.5,
        'pool_scale': 1.0 + 0.02 * nrm(ks[7], (L, POOL_DIM), jnp.float32),
        'gqa_q_gain': 1.0 + 0.02 * nrm(ks[8], (L, HEAD_DIM), jnp.float32),
        'gqa_k_gain': 1.0 + 0.02 * nrm(ks[9], (L, HEAD_DIM), jnp.float32),
        'mla_q_gain': 1.0 + 0.02 * nrm(ks[10], (L, MLA_Q_RANK), jnp.float32),
        'mla_kv_gain': 1.0 + 0.02 * nrm(ks[11], (L, MLA_KV_RANK), jnp.float32),
        'mla_w_uq': nrm(ks[12], (L, MLA_Q_RANK, MLA_HEADS * MLA_QK_DIM), jnp.float32) * MLA_Q_RANK ** -0.5,
        'mla_w_ukv': nrm(ks[13], (L, MLA_KV_RANK, MLA_HEADS * (MLA_NOPE_DIM + MLA_V_DIM)), jnp.float32) * MLA_KV_RANK ** -0.5,
        'w_out': nrm(ks[14], (L, MIX_DIM, D_MODEL), jnp.float32) * MIX_DIM ** -0.5,
        'g_norm2': 1.0 + 0.02 * nrm(ks[15], (L, D_MODEL), jnp.float32),
        'w_router': nrm(ks[16], (L, D_MODEL, N_EXPERTS), jnp.float32) * D_MODEL ** -0.5,
        'w_gate': nrm(ks[17], (L, N_EXPERTS, D_MODEL, D_FF), jnp.float32) * D_MODEL ** -0.5,
        'w_up': nrm(ks[18], (L, N_EXPERTS, D_MODEL, D_FF), jnp.float32) * D_MODEL ** -0.5,
        'w_down': nrm(ks[19], (L, N_EXPERTS, D_FF, D_MODEL), jnp.float32) * D_FF ** -0.5,
        'g_final': 1.0 + 0.02 * nrm(ks[20], (D_MODEL,), jnp.float32),
    }


def reference(x, c, w_mod, b_mod, g_norm1, w_in, pool_w, pool_scale, gqa_q_gain, gqa_k_gain,
              mla_q_gain, mla_kv_gain, mla_w_uq, mla_w_ukv, w_out, g_norm2, w_router,
              w_gate, w_up, w_down, g_final):
    n = x.shape[1]
    cos_g, sin_g = axial_rope_tables(n, HEAD_DIM)
    cos_m, sin_m = axial_rope_tables(n, MLA_ROPE_DIM)
    c_act = jax.nn.silu(c)
    for l in range(DEPTH):
        mod = jnp.einsum('bd,de->be', c_act, w_mod[l]) + b_mod[l]
        sh1, sc1, gt1, sh2, sc2, gt2 = jnp.split(mod, 6, axis=-1)
        h = modulate(rmsnorm(x, g_norm1[l]), sh1, sc1)
        z = jnp.einsum('bsd,de->bse', h, w_in[l])
        y_pool = pool_mixer(z[..., :POOL_DIM], pool_w[l], pool_scale[l])
        y_gqa = gqa_mixer(z[..., OFF_GQA_Q:OFF_GQA_K], z[..., OFF_GQA_K:OFF_GQA_V], z[..., OFF_GQA_V:OFF_MLA_Q],
                          gqa_q_gain[l], gqa_k_gain[l], cos_g, sin_g)
        y_mla = mla_mixer(z[..., OFF_MLA_Q:OFF_MLA_KV], z[..., OFF_MLA_KV:OFF_MLA_ROPE], z[..., OFF_MLA_ROPE:IN_DIM],
                          mla_q_gain[l], mla_kv_gain[l], mla_w_uq[l], mla_w_ukv[l], cos_m, sin_m)
        y = jnp.einsum('bsm,md->bsd', jnp.concatenate([y_pool, y_gqa, y_mla], axis=-1), w_out[l])
        x = x + gt1[:, None, :] * y
        h = modulate(rmsnorm(x, g_norm2[l]), sh2, sc2)
        x = x + gt2[:, None, :] * expert_choice_ffn(h, w_router[l], w_gate[l], w_up[l], w_down[l])
    return rmsnorm(x, g_final)
```

```python
import functools
import math

import numpy as np
import jax
import jax.numpy as jnp
from jax import lax
from jax.experimental import pallas as pl
from jax.experimental.pallas import tpu as pltpu

f32, bf16, i32 = jnp.float32, jnp.bfloat16, jnp.int32

D_MODEL = 1024
DEPTH = 4
GRID_W = 64
ROPE_THETA = 10000.0
EPS = 1e-6
POOL_DIM = 256
POOL_WINDOWS = (2, 4, 8, 16)
POOL_CH = 64
HEAD_DIM = 64
GQA_HEADS = 6
GQA_KV_HEADS = 2
GQA_GROUP = 3
MLA_HEADS = 6
MLA_NOPE_DIM = 64
MLA_ROPE_DIM = 32
MLA_QK_DIM = 96
MLA_V_DIM = 64
MLA_Q_RANK = 256
MLA_KV_RANK = 256
OFF_GQA_Q = POOL_DIM
OFF_GQA_K = OFF_GQA_Q + GQA_HEADS * HEAD_DIM
OFF_GQA_V = OFF_GQA_K + GQA_KV_HEADS * HEAD_DIM
OFF_MLA_Q = OFF_GQA_V + GQA_KV_HEADS * HEAD_DIM
OFF_MLA_KV = OFF_MLA_Q + MLA_Q_RANK
OFF_MLA_ROPE = OFF_MLA_KV + MLA_KV_RANK
IN_DIM = OFF_MLA_ROPE + MLA_ROPE_DIM
REST_DIM = IN_DIM - POOL_DIM
N_EXPERTS = 16
EC_CAPACITY = 2
D_FF = 2048

R_GQ = 0
R_GK = R_GQ + GQA_HEADS * HEAD_DIM
R_GV = R_GK + GQA_KV_HEADS * HEAD_DIM
R_MQ = R_GV + GQA_KV_HEADS * HEAD_DIM
R_MKV = R_MQ + MLA_Q_RANK
R_MR = R_MKV + MLA_KV_RANK

V7X_LANES = 128
V7X_VMEM_LIMIT_BYTES = 60000 * 1024
V_ROWS = 80

TOK_TILE = 512
ATT_TQ = 512
ATT_TK = 512
SLOT_BLK = 128
TOK_CHUNK = 256
FF_TILE = 512
COL_TILE = 256
LOG2E = math.log2(math.e)


def _cparams(sem, vmem=None):
    return pltpu.CompilerParams(dimension_semantics=sem, vmem_limit_bytes=vmem)


def _split_bf16(a):
    hi = a.astype(bf16)
    lo = (a - hi.astype(f32)).astype(bf16)
    return hi, lo


def _dot(a, b):
    return jnp.dot(a, b, preferred_element_type=f32)


def _dot3(a, b):
    ah, al = _split_bf16(a)
    bh, bl = _split_bf16(b)
    return _dot(ah, bh) + _dot(ah, bl) + _dot(al, bh)


def _mod_kernel(c_ref, w_ref, b_ref, o_ref):
    c = c_ref[...]
    act = c * (1.0 / (1.0 + jnp.exp(-c)))
    o_ref[0] = _dot3(act, w_ref[0]) + b_ref[0]


def _modulation(c_pad, w_mod, b_mod):
    depth, d, six_d = w_mod.shape
    rows = c_pad.shape[0]
    return pl.pallas_call(
        _mod_kernel,
        grid=(depth, six_d // d),
        in_specs=[
            pl.BlockSpec((rows, d), lambda l, j: (0, 0)),
            pl.BlockSpec((1, d, d), lambda l, j: (l, 0, j)),
            pl.BlockSpec((1, 1, d), lambda l, j: (l, 0, j)),
        ],
        out_specs=pl.BlockSpec((1, rows, d), lambda l, j: (l, 0, j)),
        out_shape=jax.ShapeDtypeStruct((depth, rows, six_d), f32),
        compiler_params=_cparams(("parallel", "parallel")),
        name="modulation",
    )(c_pad, w_mod, b_mod.reshape(depth, 1, six_d))


def _rms_rows(z, gain_col):
    r = lax.rsqrt(jnp.mean(z * z, axis=0, keepdims=True) + EPS)
    return z * r * gain_col


def _rope_rows(z, cos, sin):
    half = z.shape[0] // 2
    x1, x2 = z[:half], z[half:]
    return jnp.concatenate([x1 * cos - x2 * sin, x1 * sin + x2 * cos], axis=0)


def _inproj_kernel(x_ref, g_ref, sh_ref, sc_ref, wp_ref, wr_ref, gq_ref, gk_ref, gmq_ref, gmkv_ref,
                   wuq_ref, wukv_ref, cg_ref, sg_ref, cm_ref, sm_ref,
                   u_ref, qg_ref, kg_ref, vg_ref, qm_ref, km_ref, vm_ref, *, tk):
    x = x_ref[0]
    tt = x.shape[0]
    h = x * lax.rsqrt(jnp.mean(x * x, axis=-1, keepdims=True) + EPS) * g_ref[...]
    h = h * (1.0 + sc_ref[0]) + sh_ref[0]
    hb = h.astype(bf16)
    u_ref[0] = _dot(hb, wp_ref[...])
    zt = lax.dot_general(wr_ref[...], hb, (((1,), (1,)), ((), ())), preferred_element_type=f32)

    cg, sg, cm, sm = cg_ref[...], sg_ref[...], cm_ref[...], sm_ref[...]
    ones = jnp.ones((V_ROWS - MLA_V_DIM, tt), f32)
    n_sub = tt // tk

    def put_v(ref, head, vt):
        ve = jnp.concatenate([vt, ones], axis=0).astype(bf16)
        for j in range(n_sub):
            ref[0, head, j] = ve[:, j * tk:(j + 1) * tk]

    gq = gq_ref[...] * (HEAD_DIM ** -0.5 * LOG2E)
    gk = gk_ref[...]
    for hd in range(GQA_HEADS):
        q = _rms_rows(zt[R_GQ + hd * HEAD_DIM:R_GQ + (hd + 1) * HEAD_DIM], gq)
        qg_ref[0, hd] = _rope_rows(q, cg, sg).astype(bf16)
    for hk in range(GQA_KV_HEADS):
        k = _rms_rows(zt[R_GK + hk * HEAD_DIM:R_GK + (hk + 1) * HEAD_DIM], gk)
        kg_ref[0, hk] = _rope_rows(k, cg, sg).T.astype(bf16)
        put_v(vg_ref, hk, zt[R_GV + hk * HEAD_DIM:R_GV + (hk + 1) * HEAD_DIM])

    cq = _rms_rows(zt[R_MQ:R_MQ + MLA_Q_RANK], gmq_ref[...]).astype(bf16)
    qm = _dot(wuq_ref[...], cq) * (MLA_QK_DIM ** -0.5 * LOG2E)
    ckv = _rms_rows(zt[R_MKV:R_MKV + MLA_KV_RANK], gmkv_ref[...]).astype(bf16)
    kv = _dot(wukv_ref[...], ckv)
    k_rope = _rope_rows(zt[R_MR:R_MR + MLA_ROPE_DIM], cm, sm)
    for hd in range(MLA_HEADS):
        qh = qm[hd * MLA_QK_DIM:(hd + 1) * MLA_QK_DIM]
        qr = _rope_rows(qh[MLA_NOPE_DIM:], cm, sm)
        qm_ref[0, hd] = jnp.concatenate([qh[:MLA_NOPE_DIM], qr], axis=0).astype(bf16)
        kvh = kv[hd * (MLA_NOPE_DIM + MLA_V_DIM):(hd + 1) * (MLA_NOPE_DIM + MLA_V_DIM)]
        kh = jnp.concatenate([kvh[:MLA_NOPE_DIM], k_rope], axis=0)
        km_ref[0, hd] = kh.T.astype(bf16)
        put_v(vm_ref, hd, kvh[MLA_NOPE_DIM:])


def _inproj(x, g1, sh1, sc1, wp, wr, gq, gk, gmq, gmkv, wuq, wukv, cg, sg, cm, sm, *, tt, tk):
    b, s, d = x.shape
    n_t = s // tt
    n_sub = tt // tk
    full = lambda shape: pl.BlockSpec(shape, lambda bi, i: (0,) * len(shape))
    vec = pl.BlockSpec((1, 1, d), lambda bi, i: (bi, 0, 0))
    rope_g = pl.BlockSpec((HEAD_DIM // 2, tt), lambda bi, i: (0, i))
    rope_m = pl.BlockSpec((MLA_ROPE_DIM // 2, tt), lambda bi, i: (0, i))
    out_shapes = (
        jax.ShapeDtypeStruct((b, s, POOL_DIM), f32),
        jax.ShapeDtypeStruct((b, GQA_HEADS, HEAD_DIM, s), bf16),
        jax.ShapeDtypeStruct((b, GQA_KV_HEADS, s, HEAD_DIM), bf16),
        jax.ShapeDtypeStruct((b, GQA_KV_HEADS, s // tk, V_ROWS, tk), bf16),
        jax.ShapeDtypeStruct((b, MLA_HEADS, MLA_QK_DIM, s), bf16),
        jax.ShapeDtypeStruct((b, MLA_HEADS, s, MLA_QK_DIM), bf16),
        jax.ShapeDtypeStruct((b, MLA_HEADS, s // tk, V_ROWS, tk), bf16),
    )
    out_specs = (
        pl.BlockSpec((1, tt, POOL_DIM), lambda bi, i: (bi, i, 0)),
        pl.BlockSpec((1, GQA_HEADS, HEAD_DIM, tt), lambda bi, i: (bi, 0, 0, i)),
        pl.BlockSpec((1, GQA_KV_HEADS, tt, HEAD_DIM), lambda bi, i: (bi, 0, i, 0)),
        pl.BlockSpec((1, GQA_KV_HEADS, n_sub, V_ROWS, tk), lambda bi, i: (bi, 0, i, 0, 0)),
        pl.BlockSpec((1, MLA_HEADS, MLA_QK_DIM, tt), lambda bi, i: (bi, 0, 0, i)),
        pl.BlockSpec((1, MLA_HEADS, tt, MLA_QK_DIM), lambda bi, i: (bi, 0, i, 0)),
        pl.BlockSpec((1, MLA_HEADS, n_sub, V_ROWS, tk), lambda bi, i: (bi, 0, i, 0, 0)),
    )
    return pl.pallas_call(
        functools.partial(_inproj_kernel, tk=tk),
        grid=(b, n_t),
        in_specs=[
            pl.BlockSpec((1, tt, d), lambda bi, i: (bi, i, 0)),
            full((1, d)), vec, vec,
            full(wp.shape), full(wr.shape), full(gq.shape), full(gk.shape), full(gmq.shape), full(gmkv.shape),
            full(wuq.shape), full(wukv.shape), rope_g, rope_g, rope_m, rope_m,
        ],
        out_specs=out_specs,
        out_shape=out_shapes,
        compiler_params=_cparams(("parallel", "parallel"), V7X_VMEM_LIMIT_BYTES),
        name="inproj",
    )(x, g1, sh1, sc1, wp, wr, gq, gk, gmq, gmkv, wuq, wukv, cg, sg, cm, sm)


def _attn_kernel(q_ref, k_ref, v_ref, o_ref, *, tk, n_chunks, dv):
    qt = q_ref[0, 0]
    tq = qt.shape[1]

    def step(c, carry):
        m, acc = carry
        off = pl.multiple_of(c * tk, tk)
        s = _dot(k_ref[0, 0, pl.ds(off, tk), :], qt)
        m_new = jnp.maximum(m, jnp.max(s, axis=0, keepdims=True))
        p = jnp.exp2(s - m_new).astype(bf16)
        acc = acc * jnp.exp2(m - m_new) + _dot(v_ref[0, 0, c], p)
        return m_new, acc

    m0 = jnp.full((1, tq), -jnp.inf, f32)
    acc0 = jnp.zeros((V_ROWS, tq), f32)
    _, acc = lax.fori_loop(0, n_chunks, step, (m0, acc0))
    o_ref[0] = (acc[:dv] / acc[dv:dv + 1]).astype(o_ref.dtype)


def _attention(qt, k, vt, *, group, tq):
    b, hq, dq, s = qt.shape
    _, hk, n_chunks, v_rows, tk = vt.shape
    dv = MLA_V_DIM
    return pl.pallas_call(
        functools.partial(_attn_kernel, tk=tk, n_chunks=n_chunks, dv=dv),
        grid=(b, hq, s // tq),
        in_specs=[
            pl.BlockSpec((1, 1, dq, tq), lambda bi, h, i: (bi, h, 0, i)),
            pl.BlockSpec((1, 1, s, dq), lambda bi, h, i: (bi, h // group, 0, 0)),
            pl.BlockSpec((1, 1, n_chunks, v_rows, tk), lambda bi, h, i: (bi, h // group, 0, 0, 0)),
        ],
        out_specs=pl.BlockSpec((1, dv, tq), lambda bi, h, i: (bi, h, i)),
        out_shape=jax.ShapeDtypeStruct((b, hq * dv, s), bf16),
        compiler_params=_cparams(("parallel", "parallel", "parallel"), V7X_VMEM_LIMIT_BYTES),
        name="attention",
    )(qt, k, vt)


POOL_HALO = 16


def _pool_kernel(up_ref, uc_ref, un_ref, w_ref, sc_ref, o_ref, *, seq):
    i = pl.program_id(1)
    n_t = pl.num_programs(1)
    cur = uc_ref[0]
    tp = cur.shape[0]
    prev = jnp.where(i > 0, up_ref[0], 0.0)
    nxt = jnp.where(i < n_t - 1, un_ref[0], 0.0)
    ext = jnp.concatenate([prev, cur, nxt], axis=0)
    t = i * tp + lax.broadcasted_iota(i32, (tp, 1), 0)
    parts = []
    for gi, win in enumerate(POOL_WINDOWS):
        lo = win // 2
        hi = win - 1 - lo
        cols = slice(gi * POOL_CH, (gi + 1) * POOL_CH)
        eg = ext[:, cols]
        wsum = eg[POOL_HALO - lo:POOL_HALO - lo + tp]
        for j in range(1, win):
            wsum = wsum + eg[POOL_HALO - lo + j:POOL_HALO - lo + j + tp]
        cnt = (jnp.minimum(t + hi, seq - 1) - jnp.maximum(t - lo, 0) + 1).astype(f32)
        parts.append(wsum / cnt - cur[:, cols])
    p = jnp.concatenate(parts, axis=1).astype(bf16)
    o_ref[0] = (_dot(p, w_ref[...]) * sc_ref[...]).astype(o_ref.dtype)


def _pool(u, w_bd, scale, *, tp):
    b, s, c = u.shape
    n_t = s // tp
    r = tp // POOL_HALO
    return pl.pallas_call(
        functools.partial(_pool_kernel, seq=s),
        grid=(b, n_t),
        in_specs=[
            pl.BlockSpec((1, POOL_HALO, c), lambda bi, i: (bi, jnp.maximum(i * r - 1, 0), 0)),
            pl.BlockSpec((1, tp, c), lambda bi, i: (bi, i, 0)),
            pl.BlockSpec((1, POOL_HALO, c), lambda bi, i: (bi, jnp.minimum((i + 1) * r, s // POOL_HALO - 1), 0)),
            pl.BlockSpec(w_bd.shape, lambda bi, i: (0, 0)),
            pl.BlockSpec((1, c), lambda bi, i: (0, 0)),
        ],
        out_specs=pl.BlockSpec((1, tp, c), lambda bi, i: (bi, i, 0)),
        out_shape=jax.ShapeDtypeStruct((b, s, c), bf16),
        compiler_params=_cparams(("parallel", "parallel")),
        name="pool",
    )(u, u, u, w_bd, scale)


def _outproj_kernel(x_ref, yp_ref, og_ref, om_ref, wop_ref, wog_ref, wom_ref, gt_ref, g2_ref, sh_ref, sc_ref,
                    wrt_ref, x1_ref, h2_ref, aff_ref):
    tn = (((0,), (0,)), ((), ()))
    y = _dot(yp_ref[0], wop_ref[...])
    y = y + lax.dot_general(og_ref[0], wog_ref[...], tn, preferred_element_type=f32)
    y = y + lax.dot_general(om_ref[0], wom_ref[...], tn, preferred_element_type=f32)
    x1 = x_ref[0] + gt_ref[0] * y
    x1_ref[0] = x1
    h = x1 * lax.rsqrt(jnp.mean(x1 * x1, axis=-1, keepdims=True) + EPS) * g2_ref[...]
    h = h * (1.0 + sc_ref[0]) + sh_ref[0]
    h2_ref[0] = h.astype(bf16)
    logits = _dot3(h, wrt_ref[...])
    ex = jnp.exp(logits - jnp.max(logits, axis=-1, keepdims=True))
    aff_ref[0] = ex / jnp.sum(ex, axis=-1, keepdims=True)


def _outproj(x, ypool, og, om, wop, wog, wom, gt1, g2, sh2, sc2, w_router, *, tt):
    b, s, d = x.shape
    full = lambda shape: pl.BlockSpec(shape, lambda bi, i: (0,) * len(shape))
    vec = pl.BlockSpec((1, 1, d), lambda bi, i: (bi, 0, 0))
    n_e = w_router.shape[1]
    return pl.pallas_call(
        _outproj_kernel,
        grid=(b, s // tt),
        in_specs=[
            pl.BlockSpec((1, tt, d), lambda bi, i: (bi, i, 0)),
            pl.BlockSpec((1, tt, POOL_DIM), lambda bi, i: (bi, i, 0)),
            pl.BlockSpec((1, og.shape[1], tt), lambda bi, i: (bi, 0, i)),
            pl.BlockSpec((1, om.shape[1], tt), lambda bi, i: (bi, 0, i)),
            full(wop.shape), full(wog.shape), full(wom.shape),
            vec, full((1, d)), vec, vec, full(w_router.shape),
        ],
        out_specs=(
            pl.BlockSpec((1, tt, d), lambda bi, i: (bi, i, 0)),
            pl.BlockSpec((1, tt, d), lambda bi, i: (bi, i, 0)),
            pl.BlockSpec((1, tt, n_e), lambda bi, i: (bi, i, 0)),
        ),
        out_shape=(
            jax.ShapeDtypeStruct((b, s, d), f32),
            jax.ShapeDtypeStruct((b, s, d), bf16),
            jax.ShapeDtypeStruct((b, s, n_e), f32),
        ),
        compiler_params=_cparams(("parallel", "parallel"), V7X_VMEM_LIMIT_BYTES),
        name="outproj",
    )(x, ypool, og, om, wop, wog, wom, gt1, g2, sh2, sc2, w_router)


def _route_kernel(a_ref, posm_ref, pos_ref, *, cap):
    a = a_ref[0]
    n_e, nc, ln = a.shape
    bits = pltpu.bitcast(a, i32)

    def count(mask):
        c = jnp.sum(jnp.where(mask, 1.0, 0.0), axis=2, keepdims=True)
        return jnp.sum(c, axis=1, keepdims=True)

    thr = jnp.zeros((n_e, 1, 1), i32)
    for bit in range(30, -1, -1):
        cand = thr | (1 << bit)
        thr = jnp.where(count(bits >= cand) >= cap, cand, thr)
    gt = bits > thr
    eq = bits == thr
    need = cap - count(gt)

    r_i = lax.broadcasted_iota(i32, (ln, ln), 0)
    c_i = lax.broadcasted_iota(i32, (ln, ln), 1)
    tri_incl = jnp.where(r_i <= c_i, 1.0, 0.0).astype(bf16)
    r_c = lax.broadcasted_iota(i32, (nc, nc), 0)
    c_c = lax.broadcasted_iota(i32, (nc, nc), 1)
    tri_strict = jnp.where(c_c < r_c, 1.0, 0.0).astype(bf16)

    def excl_prefix(mask):
        x = jnp.where(mask, 1.0, 0.0)
        incl = _dot(x.astype(bf16).reshape(n_e * nc, ln), tri_incl).reshape(n_e, nc, ln)
        tot = jnp.broadcast_to(incl[:, :, ln - 1:ln], (n_e, nc, ln))
        tot_hi = tot.astype(bf16)
        offs = [_dot(tri_strict, tot_hi[e]) for e in range(n_e)]
        return jnp.stack(offs, axis=0) + incl - x

    sel = gt | (eq & (excl_prefix(eq) < need))
    pos = excl_prefix(sel).astype(i32)
    pos_ref[0] = pos
    posm_ref[0] = jnp.where(sel, pos, -1)


def _route(aff_r, *, cap):
    b, n_e, nc, ln = aff_r.shape
    spec = pl.BlockSpec((1, n_e, nc, ln), lambda bi: (bi, 0, 0, 0))
    return pl.pallas_call(
        functools.partial(_route_kernel, cap=cap),
        grid=(b,),
        in_specs=[spec],
        out_specs=(spec, spec),
        out_shape=(jax.ShapeDtypeStruct(aff_r.shape, i32), jax.ShapeDtypeStruct(aff_r.shape, i32)),
        compiler_params=_cparams(("parallel",), V7X_VMEM_LIMIT_BYTES),
        name="route",
    )(aff_r)


def _gather_kernel(offs_ref, h_ref, posm_ref, xe_ref, acc_ref, *, n_off, n_chunks):
    b, e = pl.program_id(0), pl.program_id(1)
    base = (b * pl.num_programs(1) + e) * n_off
    step = TOK_CHUNK // V7X_LANES
    acc_ref[...] = jnp.zeros_like(acc_ref)
    slot_iota = lax.broadcasted_iota(i32, (SLOT_BLK, TOK_CHUNK), 0)

    def chunk(c, carry):
        lo = offs_ref[base + c * step]
        hi = offs_ref[base + (c + 1) * step]
        tok0 = pl.multiple_of(c * TOK_CHUNK, TOK_CHUNK)
        pr = posm_ref[0, 0, c]
        jb0 = lax.shift_right_logical(lo, 7)
        for k in range(TOK_CHUNK // SLOT_BLK + 1):
            sbase = pl.multiple_of((jb0 + k) * SLOT_BLK, SLOT_BLK)

            @pl.when(hi > sbase)
            def _():
                onehot = jnp.where(pr == slot_iota + sbase, 1.0, 0.0).astype(bf16)
                acc_ref[pl.ds(sbase, SLOT_BLK), :] += _dot(onehot, h_ref[0, pl.ds(tok0, TOK_CHUNK), :])
        return carry

    lax.fori_loop(0, n_chunks, chunk, 0)
    xe_ref[0, 0] = acc_ref[...].astype(xe_ref.dtype)


def _gather(offs, h2, posm_c, *, cap):
    b, s, d = h2.shape
    n_e = posm_c.shape[1]
    n_chunks = s // TOK_CHUNK
    n_off = s // V7X_LANES + 1
    return pl.pallas_call(
        functools.partial(_gather_kernel, n_off=n_off, n_chunks=n_chunks),
        grid_spec=pltpu.PrefetchScalarGridSpec(
            num_scalar_prefetch=1,
            grid=(b, n_e),
            in_specs=[
                pl.BlockSpec((1, s, d), lambda bi, e, offs: (bi, 0, 0), pipeline_mode=pl.Buffered(1)),
                pl.BlockSpec((1, 1, n_chunks, 1, TOK_CHUNK), lambda bi, e, offs: (bi, e, 0, 0, 0)),
            ],
            out_specs=pl.BlockSpec((1, 1, cap, d), lambda bi, e, offs: (bi, e, 0, 0)),
            scratch_shapes=[pltpu.VMEM((cap, d), f32)],
        ),
        out_shape=jax.ShapeDtypeStruct((b, n_e, cap, d), bf16),
        compiler_params=_cparams(("arbitrary", "arbitrary"), V7X_VMEM_LIMIT_BYTES),
        name="gather",
    )(offs, h2, posm_c)


def _ffn_kernel(x_ref, wg_ref, wu_ref, wd_ref, o_ref, acc_ref):
    f = pl.program_id(2)
    x = x_ref[0, 0]
    a = _dot(x, wg_ref[0, 0].astype(bf16))
    u = _dot(x, wu_ref[0, 0].astype(bf16))
    hmid = (a * (1.0 / (1.0 + jnp.exp(-a))) * u).astype(bf16)
    part = _dot(hmid, wd_ref[0, 0].astype(bf16))

    @pl.when(f == 0)
    def _():
        acc_ref[...] = part

    @pl.when(f > 0)
    def _():
        acc_ref[...] += part

    @pl.when(f == pl.num_programs(2) - 1)
    def _():
        o_ref[0, 0] = acc_ref[...].astype(o_ref.dtype)


def _ffn(xe, w_gate, w_up, w_down, layer):
    b, n_e, cap, d = xe.shape
    d_ff = w_gate.shape[-1]
    n_f = d_ff // FF_TILE
    return pl.pallas_call(
        _ffn_kernel,
        grid=(n_e, b, n_f),
        in_specs=[
            pl.BlockSpec((1, 1, cap, d), lambda e, bi, f: (bi, e, 0, 0)),
            pl.BlockSpec((1, 1, d, FF_TILE), lambda e, bi, f: (layer, e, 0, f)),
            pl.BlockSpec((1, 1, d, FF_TILE), lambda e, bi, f: (layer, e, 0, f)),
            pl.BlockSpec((1, 1, FF_TILE, d), lambda e, bi, f: (layer, e, f, 0)),
        ],
        out_specs=pl.BlockSpec((1, 1, cap, d), lambda e, bi, f: (bi, e, 0, 0)),
        out_shape=jax.ShapeDtypeStruct((b, n_e, cap, d), bf16),
        scratch_shapes=[pltpu.VMEM((cap, d), f32)],
        compiler_params=_cparams(("parallel", "parallel", "arbitrary"), V7X_VMEM_LIMIT_BYTES),
        name="expert_ffn",
    )(xe, w_gate, w_up, w_down)


def _combine_kernel(offs_ref, x_ref, aff_ref, posm_ref, gt_ref, ye_ref, o_ref, acc_ref, *, n_off):
    b, i = pl.program_id(0), pl.program_id(2)
    n_e = ye_ref.shape[1]
    step = TOK_CHUNK // V7X_LANES
    acc_ref[...] = jnp.zeros_like(acc_ref)
    slot_iota = lax.broadcasted_iota(i32, (TOK_CHUNK, SLOT_BLK), 1)
    posm = posm_ref[0]
    aff = aff_ref[0]
    for e in range(n_e):
        base = (b * n_e + e) * n_off
        lo = offs_ref[base + i * step]
        hi = offs_ref[base + (i + 1) * step]
        jb0 = lax.shift_right_logical(lo, 7)
        pcol = posm[:, e:e + 1]
        gcol = aff[:, e:e + 1]
        for k in range(TOK_CHUNK // SLOT_BLK + 1):
            sbase = pl.multiple_of((jb0 + k) * SLOT_BLK, SLOT_BLK)

            @pl.when(hi > sbase)
            def _():
                onehot = jnp.where(pcol == slot_iota + sbase, 1.0, 0.0).astype(bf16)
                acc_ref[...] += gcol * _dot(onehot, ye_ref[0, e, pl.ds(sbase, SLOT_BLK), :])
    o_ref[0] = x_ref[0] + gt_ref[0] * acc_ref[...]


def _combine(offs, x1, aff, posm_t, gt2, ye):
    b, s, d = x1.shape
    n_e, cap = ye.shape[1], ye.shape[2]
    n_off = s // V7X_LANES + 1
    return pl.pallas_call(
        functools.partial(_combine_kernel, n_off=n_off),
        grid_spec=pltpu.PrefetchScalarGridSpec(
            num_scalar_prefetch=1,
            grid=(b, d // COL_TILE, s // TOK_CHUNK),
            in_specs=[
                pl.BlockSpec((1, TOK_CHUNK, COL_TILE), lambda bi, j, i, offs: (bi, i, j)),
                pl.BlockSpec((1, TOK_CHUNK, n_e), lambda bi, j, i, offs: (bi, i, 0)),
                pl.BlockSpec((1, TOK_CHUNK, n_e), lambda bi, j, i, offs: (bi, i, 0)),
                pl.BlockSpec((1, 1, COL_TILE), lambda bi, j, i, offs: (bi, 0, j)),
                pl.BlockSpec((1, n_e, cap, COL_TILE), lambda bi, j, i, offs: (bi, 0, 0, j)),
            ],
            out_specs=pl.BlockSpec((1, TOK_CHUNK, COL_TILE), lambda bi, j, i, offs: (bi, i, j)),
            scratch_shapes=[pltpu.VMEM((TOK_CHUNK, COL_TILE), f32)],
        ),
        out_shape=jax.ShapeDtypeStruct((b, s, d), f32),
        compiler_params=_cparams(("arbitrary", "arbitrary", "arbitrary"), V7X_VMEM_LIMIT_BYTES),
        name="combine",
    )(offs, x1, aff, posm_t, gt2, ye)


def _final_kernel(x_ref, g_ref, o_ref):
    x = x_ref[0]
    o_ref[0] = x * lax.rsqrt(jnp.mean(x * x, axis=-1, keepdims=True) + EPS) * g_ref[...]


def _final_norm(x, g, *, tt):
    b, s, d = x.shape
    return pl.pallas_call(
        _final_kernel,
        grid=(b, s // tt),
        in_specs=[pl.BlockSpec((1, tt, d), lambda bi, i: (bi, i, 0)), pl.BlockSpec((1, d), lambda bi, i: (0, 0))],
        out_specs=pl.BlockSpec((1, tt, d), lambda bi, i: (bi, i, 0)),
        out_shape=jax.ShapeDtypeStruct((b, s, d), f32),
        compiler_params=_cparams(("parallel", "parallel")),
        name="final_norm",
    )(x, g)


def _deinterleave(n):
    return np.concatenate([np.arange(0, n, 2), np.arange(1, n, 2)])


def _rope_tables_t(n, d_rot):
    n_rows = n // GRID_W
    row = jnp.repeat(jnp.arange(n_rows, dtype=f32), GRID_W)
    col = jnp.tile(jnp.arange(GRID_W, dtype=f32), n_rows)
    n_freq = d_rot // 4
    inv_freq = ROPE_THETA ** (-jnp.arange(n_freq, dtype=f32) / n_freq)
    ang = jnp.concatenate([row[:, None] * inv_freq, col[:, None] * inv_freq], axis=-1)
    return jnp.cos(ang).T, jnp.sin(ang).T


def _rest_columns():
    p64, p32 = _deinterleave(HEAD_DIM), _deinterleave(MLA_ROPE_DIM)
    cols = [OFF_GQA_Q + h * HEAD_DIM + p64 for h in range(GQA_HEADS)]
    cols += [OFF_GQA_K + h * HEAD_DIM + p64 for h in range(GQA_KV_HEADS)]
    cols += [np.arange(OFF_GQA_V, IN_DIM - MLA_ROPE_DIM), OFF_MLA_ROPE + p32]
    return np.concatenate(cols)


def _uq_columns():
    p32 = _deinterleave(MLA_ROPE_DIM)
    cols = []
    for h in range(MLA_HEADS):
        cols += [h * MLA_QK_DIM + np.arange(MLA_NOPE_DIM), h * MLA_QK_DIM + MLA_NOPE_DIM + p32]
    return np.concatenate(cols)


def _block_diag(w):
    g, c, _ = w.shape
    out = jnp.zeros((g * c, g * c), w.dtype)
    for i in range(g):
        out = out.at[i * c:(i + 1) * c, i * c:(i + 1) * c].set(w[i])
    return out


def _trunk(x, c, w_mod, b_mod, g_norm1, w_in, pool_w, pool_scale, gqa_q_gain, gqa_k_gain, mla_q_gain, mla_kv_gain,
           mla_w_uq, mla_w_ukv, w_out, g_norm2, w_router, w_gate, w_up, w_down, g_final):
    b, s, d = x.shape
    depth = w_mod.shape[0]
    cap = (EC_CAPACITY * s) // N_EXPERTS
    tt = min(TOK_TILE, s)
    tq = min(ATT_TQ, s)
    tk = min(ATT_TK, s)
    nc = s // V7X_LANES

    mod_rows = 8
    c_pad = jnp.zeros((mod_rows, d), f32).at[:b].set(c)
    mod = _modulation(c_pad, w_mod, b_mod)[:, :b].reshape(depth, b, 6, 1, d)

    cg, sg = _rope_tables_t(s, HEAD_DIM)
    cm, sm = _rope_tables_t(s, MLA_ROPE_DIM)
    p64 = _deinterleave(HEAD_DIM)
    rest_cols, uq_cols = _rest_columns(), _uq_columns()

    for l in range(depth):
        sh1, sc1, gt1, sh2, sc2, gt2 = (mod[l, :, k] for k in range(6))
        wp = w_in[l][:, :POOL_DIM].astype(bf16)
        wr = w_in[l][:, rest_cols].T.astype(bf16)
        wuq = mla_w_uq[l][:, uq_cols].T.astype(bf16)
        wukv = mla_w_ukv[l].T.astype(bf16)
        u, qg, kg, vg, qm, km, vm = _inproj(
            x, g_norm1[l][None], sh1, sc1, wp, wr,
            gqa_q_gain[l][p64][:, None], gqa_k_gain[l][p64][:, None], mla_q_gain[l][:, None], mla_kv_gain[l][:, None],
            wuq, wukv, cg, sg, cm, sm, tt=tt, tk=tk)
        og = _attention(qg, kg, vg, group=GQA_GROUP, tq=tq)
        om = _attention(qm, km, vm, group=1, tq=tq)
        ypool = _pool(u, _block_diag(pool_w[l]).astype(bf16), pool_scale[l][None], tp=tt)
        wo = w_out[l].astype(bf16)
        n_g = GQA_HEADS * HEAD_DIM
        x1, h2, aff = _outproj(x, ypool, og, om, wo[:POOL_DIM], wo[POOL_DIM:POOL_DIM + n_g], wo[POOL_DIM + n_g:],
                               gt1, g_norm2[l][None], sh2, sc2, w_router[l], tt=tt)

        aff_r = aff.transpose(0, 2, 1).reshape(b, N_EXPERTS, nc, V7X_LANES)
        posm, pos = _route(aff_r, cap=cap)
        offs = jnp.concatenate([pos[..., 0], jnp.full((b, N_EXPERTS, 1), cap, i32)], axis=-1).reshape(-1)
        posm_c = posm.reshape(b, N_EXPERTS, s // TOK_CHUNK, 1, TOK_CHUNK)
        posm_t = posm.reshape(b, N_EXPERTS, s).transpose(0, 2, 1)
        xe = _gather(offs, h2, posm_c, cap=cap)
        ye = _ffn(xe, w_gate, w_up, w_down, l)
        x = _combine(offs, x1, aff, posm_t, gt2, ye)
    return _final_norm(x, g_final[None], tt=tt)


def kernel(x, c, w_mod, b_mod, g_norm1, w_in, pool_w, pool_scale, gqa_q_gain, gqa_k_gain, mla_q_gain, mla_kv_gain,
           mla_w_uq, mla_w_ukv, w_out, g_norm2, w_router, w_gate, w_up, w_down, g_final):
    return _trunk(x, c, w_mod, b_mod, g_norm1, w_in, pool_w, pool_scale, gqa_q_gain, gqa_k_gain, mla_q_gain,
                  mla_kv_gain, mla_w_uq, mla_w_ukv, w_out, g_norm2, w_router, w_gate, w_up, w_down, g_final)
```

```python
import functools
import math

import numpy as np
import jax
import jax.numpy as jnp
from jax import lax
from jax.experimental import pallas as pl
from jax.experimental.pallas import tpu as pltpu

f32, bf16, i32 = jnp.float32, jnp.bfloat16, jnp.int32

D_MODEL = 1024
DEPTH = 4
GRID_W = 64
ROPE_THETA = 10000.0
EPS = 1e-6
POOL_DIM = 256
POOL_WINDOWS = (2, 4, 8, 16)
POOL_CH = 64
HEAD_DIM = 64
GQA_HEADS = 6
GQA_KV_HEADS = 2
GQA_GROUP = 3
MLA_HEADS = 6
MLA_NOPE_DIM = 64
MLA_ROPE_DIM = 32
MLA_QK_DIM = 96
MLA_V_DIM = 64
MLA_Q_RANK = 256
MLA_KV_RANK = 256
OFF_GQA_Q = POOL_DIM
OFF_GQA_K = OFF_GQA_Q + GQA_HEADS * HEAD_DIM
OFF_GQA_V = OFF_GQA_K + GQA_KV_HEADS * HEAD_DIM
OFF_MLA_Q = OFF_GQA_V + GQA_KV_HEADS * HEAD_DIM
OFF_MLA_KV = OFF_MLA_Q + MLA_Q_RANK
OFF_MLA_ROPE = OFF_MLA_KV + MLA_KV_RANK
IN_DIM = OFF_MLA_ROPE + MLA_ROPE_DIM
REST_DIM = IN_DIM - POOL_DIM
N_EXPERTS = 16
EC_CAPACITY = 2
D_FF = 2048

R_GQ = 0
R_GK = R_GQ + GQA_HEADS * HEAD_DIM
R_GV = R_GK + GQA_KV_HEADS * HEAD_DIM
R_MQ = R_GV + GQA_KV_HEADS * HEAD_DIM
R_MKV = R_MQ + MLA_Q_RANK
R_MR = R_MKV + MLA_KV_RANK

V7X_LANES = 128
V7X_VMEM_LIMIT_BYTES = 60000 * 1024
V_ROWS = 80

TOK_TILE = 512
ATT_TQ = 512
GQA_TQ = 1024
MLA_TQ = 2048
ATT_TK = 512
TOK_CHUNK = 256
FAST_SLOTS = 64
SLOT_ALIGN_LOG2 = 4
FF_TILE = 512
COL_TILE = 512
LOG2E = math.log2(math.e)


def _cparams(sem, vmem=None):
    return pltpu.CompilerParams(dimension_semantics=sem, vmem_limit_bytes=vmem)


def _split_bf16(a):
    hi = a.astype(bf16)
    lo = (a - hi.astype(f32)).astype(bf16)
    return hi, lo


def _dot(a, b):
    return jnp.dot(a, b, preferred_element_type=f32)


def _dot3(a, b):
    ah, al = _split_bf16(a)
    bh, bl = _split_bf16(b)
    return _dot(ah, bh) + _dot(ah, bl) + _dot(al, bh)


def _mod_kernel(c_ref, w_ref, b_ref, o_ref):
    c = c_ref[...]
    act = c * (1.0 / (1.0 + jnp.exp(-c)))
    o_ref[0] = _dot3(act, w_ref[0]) + b_ref[0]


def _modulation(c_pad, w_mod, b_mod):
    depth, d, six_d = w_mod.shape
    rows = c_pad.shape[0]
    return pl.pallas_call(
        _mod_kernel,
        grid=(depth, six_d // d),
        in_specs=[
            pl.BlockSpec((rows, d), lambda l, j: (0, 0)),
            pl.BlockSpec((1, d, d), lambda l, j: (l, 0, j)),
            pl.BlockSpec((1, 1, d), lambda l, j: (l, 0, j)),
        ],
        out_specs=pl.BlockSpec((1, rows, d), lambda l, j: (l, 0, j)),
        out_shape=jax.ShapeDtypeStruct((depth, rows, six_d), f32),
        compiler_params=_cparams(("parallel", "parallel")),
        name="modulation",
    )(c_pad, w_mod, b_mod.reshape(depth, 1, six_d))


def _rms_rows(z, gain_col):
    r = lax.rsqrt(jnp.mean(z * z, axis=0, keepdims=True) + EPS)
    return z * r * gain_col


def _rope_rows(z, cos, sin):
    half = z.shape[0] // 2
    x1, x2 = z[:half], z[half:]
    return jnp.concatenate([x1 * cos - x2 * sin, x1 * sin + x2 * cos], axis=0)


def _inproj_kernel(x_ref, g_ref, sh_ref, sc_ref, wp_ref, wr_ref, gq_ref, gk_ref, gmq_ref, gmkv_ref,
                   wuq_ref, wukv_ref, cg_ref, sg_ref, cm_ref, sm_ref,
                   u_ref, qg_ref, kg_ref, kng_ref, vg_ref, qm_ref, km_ref, knm_ref, vm_ref, *, tk):
    x = x_ref[0]
    tt = x.shape[0]
    h = x * lax.rsqrt(jnp.mean(x * x, axis=-1, keepdims=True) + EPS) * g_ref[...]
    h = h * (1.0 + sc_ref[0]) + sh_ref[0]
    hb = h.astype(bf16)
    u_ref[0] = _dot(hb, wp_ref[...])
    zt = lax.dot_general(wr_ref[...], hb, (((1,), (1,)), ((), ())), preferred_element_type=f32)

    cg, sg, cm, sm = cg_ref[...], sg_ref[...], cm_ref[...], sm_ref[...]
    ones = jnp.ones((V_ROWS - MLA_V_DIM, tt), f32)
    n_sub = tt // tk

    def put_k(ref, norm_ref, head, kt):
        kb = kt.astype(bf16)
        kf = kb.astype(f32)
        norm_ref[0, head] = jnp.sqrt(jnp.sum(kf * kf, axis=0, keepdims=True))
        ref[0, head] = kt.T.astype(bf16)

    def put_v(ref, head, vt):
        ve = jnp.concatenate([vt, ones], axis=0).astype(bf16)
        for j in range(n_sub):
            ref[0, head, j] = ve[:, j * tk:(j + 1) * tk]

    gq = gq_ref[...] * (HEAD_DIM ** -0.5 * LOG2E)
    gk = gk_ref[...]
    for hd in range(GQA_HEADS):
        q = _rms_rows(zt[R_GQ + hd * HEAD_DIM:R_GQ + (hd + 1) * HEAD_DIM], gq)
        qg_ref[0, hd] = _rope_rows(q, cg, sg).astype(bf16)
    for hk in range(GQA_KV_HEADS):
        k = _rms_rows(zt[R_GK + hk * HEAD_DIM:R_GK + (hk + 1) * HEAD_DIM], gk)
        put_k(kg_ref, kng_ref, hk, _rope_rows(k, cg, sg))
        put_v(vg_ref, hk, zt[R_GV + hk * HEAD_DIM:R_GV + (hk + 1) * HEAD_DIM])

    cq = _rms_rows(zt[R_MQ:R_MQ + MLA_Q_RANK], gmq_ref[...]).astype(bf16)
    qm = _dot(wuq_ref[...], cq) * (MLA_QK_DIM ** -0.5 * LOG2E)
    ckv = _rms_rows(zt[R_MKV:R_MKV + MLA_KV_RANK], gmkv_ref[...]).astype(bf16)
    kv = _dot(wukv_ref[...], ckv)
    k_rope = _rope_rows(zt[R_MR:R_MR + MLA_ROPE_DIM], cm, sm)
    for hd in range(MLA_HEADS):
        qh = qm[hd * MLA_QK_DIM:(hd + 1) * MLA_QK_DIM]
        qr = _rope_rows(qh[MLA_NOPE_DIM:], cm, sm)
        qm_ref[0, hd] = jnp.concatenate([qh[:MLA_NOPE_DIM], qr], axis=0).astype(bf16)
        kvh = kv[hd * (MLA_NOPE_DIM + MLA_V_DIM):(hd + 1) * (MLA_NOPE_DIM + MLA_V_DIM)]
        kh = jnp.concatenate([kvh[:MLA_NOPE_DIM], k_rope], axis=0)
        put_k(km_ref, knm_ref, hd, kh)
        put_v(vm_ref, hd, kvh[MLA_NOPE_DIM:])


def _inproj(x, g1, sh1, sc1, wp, wr, gq, gk, gmq, gmkv, wuq, wukv, cg, sg, cm, sm, *, tt, tk):
    b, s, d = x.shape
    n_t = s // tt
    n_sub = tt // tk
    full = lambda shape: pl.BlockSpec(shape, lambda bi, i: (0,) * len(shape))
    vec = pl.BlockSpec((1, 1, d), lambda bi, i: (bi, 0, 0))
    rope_g = pl.BlockSpec((HEAD_DIM // 2, tt), lambda bi, i: (0, i))
    rope_m = pl.BlockSpec((MLA_ROPE_DIM // 2, tt), lambda bi, i: (0, i))
    out_shapes = (
        jax.ShapeDtypeStruct((b, s, POOL_DIM), f32),
        jax.ShapeDtypeStruct((b, GQA_HEADS, HEAD_DIM, s), bf16),
        jax.ShapeDtypeStruct((b, GQA_KV_HEADS, s, HEAD_DIM), bf16),
        jax.ShapeDtypeStruct((b, GQA_KV_HEADS, 1, s), f32),
        jax.ShapeDtypeStruct((b, GQA_KV_HEADS, s // tk, V_ROWS, tk), bf16),
        jax.ShapeDtypeStruct((b, MLA_HEADS, MLA_QK_DIM, s), bf16),
        jax.ShapeDtypeStruct((b, MLA_HEADS, s, MLA_QK_DIM), bf16),
        jax.ShapeDtypeStruct((b, MLA_HEADS, 1, s), f32),
        jax.ShapeDtypeStruct((b, MLA_HEADS, s // tk, V_ROWS, tk), bf16),
    )
    out_specs = (
        pl.BlockSpec((1, tt, POOL_DIM), lambda bi, i: (bi, i, 0)),
        pl.BlockSpec((1, GQA_HEADS, HEAD_DIM, tt), lambda bi, i: (bi, 0, 0, i)),
        pl.BlockSpec((1, GQA_KV_HEADS, tt, HEAD_DIM), lambda bi, i: (bi, 0, i, 0)),
        pl.BlockSpec((1, GQA_KV_HEADS, 1, tt), lambda bi, i: (bi, 0, 0, i)),
        pl.BlockSpec((1, GQA_KV_HEADS, n_sub, V_ROWS, tk), lambda bi, i: (bi, 0, i, 0, 0)),
        pl.BlockSpec((1, MLA_HEADS, MLA_QK_DIM, tt), lambda bi, i: (bi, 0, 0, i)),
        pl.BlockSpec((1, MLA_HEADS, tt, MLA_QK_DIM), lambda bi, i: (bi, 0, i, 0)),
        pl.BlockSpec((1, MLA_HEADS, 1, tt), lambda bi, i: (bi, 0, 0, i)),
        pl.BlockSpec((1, MLA_HEADS, n_sub, V_ROWS, tk), lambda bi, i: (bi, 0, i, 0, 0)),
    )
    return pl.pallas_call(
        functools.partial(_inproj_kernel, tk=tk),
        grid=(b, n_t),
        in_specs=[
            pl.BlockSpec((1, tt, d), lambda bi, i: (bi, i, 0)),
            full((1, d)), vec, vec,
            full(wp.shape), full(wr.shape), full(gq.shape), full(gk.shape), full(gmq.shape), full(gmkv.shape),
            full(wuq.shape), full(wukv.shape), rope_g, rope_g, rope_m, rope_m,
        ],
        out_specs=out_specs,
        out_shape=out_shapes,
        compiler_params=_cparams(("parallel", "parallel"), V7X_VMEM_LIMIT_BYTES),
        name="inproj",
    )(x, g1, sh1, sc1, wp, wr, gq, gk, gmq, gmkv, wuq, wukv, cg, sg, cm, sm)


SAFE_LOGIT_BOUND = 50.0


def _attn_kernel(q_ref, k_ref, kn_ref, v_ref, o_ref, s0_ref, m_ref, acc_ref, *, tk, n_chunks, dv, n_sub):
    group, tq = q_ref.shape[1], q_ref.shape[3]
    ts = tq // n_sub
    streams = [(h, j) for h in range(group) for j in range(n_sub)]
    qts = [q_ref[0, h, :, j * ts:(j + 1) * ts] for h, j in streams]

    def k_chunk(c):
        return k_ref[0, 0, pl.ds(pl.multiple_of(c * tk, tk), tk), :]

    def finish():
        for n, (h, j) in enumerate(streams):
            acc = acc_ref[n]
            o_ref[0, h * dv:(h + 1) * dv, j * ts:(j + 1) * ts] = (acc[:dv] / acc[dv:dv + 1]).astype(o_ref.dtype)

    k_max = jnp.max(kn_ref[0, 0], axis=-1, keepdims=True)
    bounds = []
    for qt in qts:
        qf = qt.astype(f32)
        bounds.append(jnp.sqrt(jnp.sum(qf * qf, axis=0, keepdims=True)) * k_max)
    worst = functools.reduce(jnp.maximum, [jnp.max(u, axis=-1, keepdims=True) for u in bounds])
    safe = worst[0, 0] <= SAFE_LOGIT_BOUND

    @pl.when(safe)
    def _():
        acc_ref[...] = jnp.zeros(acc_ref.shape, f32)
        s0_ref[...] = _dot(k_chunk(0), qts[0])
        order = [(dc, n) for dc in range(2) for n in range(len(streams))]

        def step(i, carry):
            c = 2 * i
            kcs = [k_chunk(c), k_chunk(c + 1), k_chunk(jnp.minimum(c + 2, n_chunks - 1))]
            vcs = [v_ref[0, 0, c], v_ref[0, 0, c + 1]]
            s_cur = s0_ref[...]
            for idx, (dc, n) in enumerate(order):
                dc2, n2 = order[idx + 1] if idx + 1 < len(order) else (2, 0)
                s_next = _dot(kcs[dc2], qts[n2])
                p = jnp.exp2(s_cur - bounds[n]).astype(bf16)
                acc_ref[n] += _dot(vcs[dc], p)
                s_cur = s_next
            s0_ref[...] = s_cur
            return carry

        lax.fori_loop(0, n_chunks // 2, step, 0)
        finish()

    @pl.when(jnp.logical_not(safe))
    def _():
        m_ref[...] = jnp.full(m_ref.shape, -jnp.inf, f32)
        acc_ref[...] = jnp.zeros(acc_ref.shape, f32)

        def step(c, carry):
            kc = k_chunk(c)
            vc = v_ref[0, 0, c]
            for n, qt in enumerate(qts):
                s = _dot(kc, qt)
                m = m_ref[n]
                m_new = jnp.maximum(m, jnp.max(s, axis=0, keepdims=True))
                p = jnp.exp2(s - m_new).astype(bf16)
                acc_ref[n] = acc_ref[n] * jnp.exp2(m - m_new) + _dot(vc, p)
                m_ref[n] = m_new
            return carry

        lax.fori_loop(0, n_chunks, step, 0)
        finish()


def _attention(qt, k, kn, vt, *, group, tq, n_sub):
    b, hq, dq, s = qt.shape
    _, hk, n_chunks, v_rows, tk = vt.shape
    dv = MLA_V_DIM
    assert n_chunks % 2 == 0 and tq % n_sub == 0
    n_streams, ts = group * n_sub, tq // n_sub
    return pl.pallas_call(
        functools.partial(_attn_kernel, tk=tk, n_chunks=n_chunks, dv=dv, n_sub=n_sub),
        grid=(b, hk, s // tq),
        scratch_shapes=[
            pltpu.VMEM((tk, ts), f32),
            pltpu.VMEM((n_streams, 1, ts), f32),
            pltpu.VMEM((n_streams, v_rows, ts), f32),
        ],
        in_specs=[
            pl.BlockSpec((1, group, dq, tq), lambda bi, g, i: (bi, g, 0, i)),
            pl.BlockSpec((1, 1, s, dq), lambda bi, g, i: (bi, g, 0, 0)),
            pl.BlockSpec((1, 1, 1, s), lambda bi, g, i: (bi, g, 0, 0)),
            pl.BlockSpec((1, 1, n_chunks, v_rows, tk), lambda bi, g, i: (bi, g, 0, 0, 0)),
        ],
        out_specs=pl.BlockSpec((1, group * dv, tq), lambda bi, g, i: (bi, g, i)),
        out_shape=jax.ShapeDtypeStruct((b, hq * dv, s), bf16),
        compiler_params=_cparams(("parallel", "parallel", "parallel"), V7X_VMEM_LIMIT_BYTES),
        name="attention",
    )(qt, k, kn, vt)


POOL_HALO = 16


def _pool_kernel(up_ref, uc_ref, un_ref, w_ref, sc_ref, o_ref, *, seq):
    i = pl.program_id(1)
    n_t = pl.num_programs(1)
    cur = uc_ref[0]
    tp = cur.shape[0]
    prev = jnp.where(i > 0, up_ref[0], 0.0)
    nxt = jnp.where(i < n_t - 1, un_ref[0], 0.0)
    ext = jnp.concatenate([prev, cur, nxt], axis=0)
    t = i * tp + lax.broadcasted_iota(i32, (tp, 1), 0)
    parts = []
    for gi, win in enumerate(POOL_WINDOWS):
        lo = win // 2
        hi = win - 1 - lo
        cols = slice(gi * POOL_CH, (gi + 1) * POOL_CH)
        eg = ext[:, cols]
        wsum = eg[POOL_HALO - lo:POOL_HALO - lo + tp]
        for j in range(1, win):
            wsum = wsum + eg[POOL_HALO - lo + j:POOL_HALO - lo + j + tp]
        cnt = (jnp.minimum(t + hi, seq - 1) - jnp.maximum(t - lo, 0) + 1).astype(f32)
        parts.append(wsum / cnt - cur[:, cols])
    p = jnp.concatenate(parts, axis=1).astype(bf16)
    o_ref[0] = (_dot(p, w_ref[...]) * sc_ref[...]).astype(o_ref.dtype)


def _pool(u, w_bd, scale, *, tp):
    b, s, c = u.shape
    n_t = s // tp
    r = tp // POOL_HALO
    return pl.pallas_call(
        functools.partial(_pool_kernel, seq=s),
        grid=(b, n_t),
        in_specs=[
            pl.BlockSpec((1, POOL_HALO, c), lambda bi, i: (bi, jnp.maximum(i * r - 1, 0), 0)),
            pl.BlockSpec((1, tp, c), lambda bi, i: (bi, i, 0)),
            pl.BlockSpec((1, POOL_HALO, c), lambda bi, i: (bi, jnp.minimum((i + 1) * r, s // POOL_HALO - 1), 0)),
            pl.BlockSpec(w_bd.shape, lambda bi, i: (0, 0)),
            pl.BlockSpec((1, c), lambda bi, i: (0, 0)),
        ],
        out_specs=pl.BlockSpec((1, tp, c), lambda bi, i: (bi, i, 0)),
        out_shape=jax.ShapeDtypeStruct((b, s, c), bf16),
        compiler_params=_cparams(("parallel", "parallel")),
        name="pool",
    )(u, u, u, w_bd, scale)


def _outproj_kernel(x_ref, yp_ref, og_ref, om_ref, wop_ref, wog_ref, wom_ref, gt_ref, g2_ref, sh_ref, sc_ref,
                    wrt_ref, x1_ref, h2_ref, aff_ref):
    tn = (((0,), (0,)), ((), ()))
    y = _dot(yp_ref[0], wop_ref[...])
    y = y + lax.dot_general(og_ref[0], wog_ref[...], tn, preferred_element_type=f32)
    y = y + lax.dot_general(om_ref[0], wom_ref[...], tn, preferred_element_type=f32)
    x1 = x_ref[0] + gt_ref[0] * y
    x1_ref[0] = x1
    h = x1 * lax.rsqrt(jnp.mean(x1 * x1, axis=-1, keepdims=True) + EPS) * g2_ref[...]
    h = h * (1.0 + sc_ref[0]) + sh_ref[0]
    h2_ref[0] = h.astype(bf16)
    logits = _dot3(h, wrt_ref[...])
    ex = jnp.exp(logits - jnp.max(logits, axis=-1, keepdims=True))
    aff_ref[0] = ex / jnp.sum(ex, axis=-1, keepdims=True)


def _outproj(x, ypool, og, om, wop, wog, wom, gt1, g2, sh2, sc2, w_router, *, tt):
    b, s, d = x.shape
    full = lambda shape: pl.BlockSpec(shape, lambda bi, i: (0,) * len(shape))
    vec = pl.BlockSpec((1, 1, d), lambda bi, i: (bi, 0, 0))
    n_e = w_router.shape[1]
    return pl.pallas_call(
        _outproj_kernel,
        grid=(b, s // tt),
        in_specs=[
            pl.BlockSpec((1, tt, d), lambda bi, i: (bi, i, 0)),
            pl.BlockSpec((1, tt, POOL_DIM), lambda bi, i: (bi, i, 0)),
            pl.BlockSpec((1, og.shape[1], tt), lambda bi, i: (bi, 0, i)),
            pl.BlockSpec((1, om.shape[1], tt), lambda bi, i: (bi, 0, i)),
            full(wop.shape), full(wog.shape), full(wom.shape),
            vec, full((1, d)), vec, vec, full(w_router.shape),
        ],
        out_specs=(
            pl.BlockSpec((1, tt, d), lambda bi, i: (bi, i, 0)),
            pl.BlockSpec((1, tt, d), lambda bi, i: (bi, i, 0)),
            pl.BlockSpec((1, tt, n_e), lambda bi, i: (bi, i, 0)),
        ),
        out_shape=(
            jax.ShapeDtypeStruct((b, s, d), f32),
            jax.ShapeDtypeStruct((b, s, d), bf16),
            jax.ShapeDtypeStruct((b, s, n_e), f32),
        ),
        compiler_params=_cparams(("parallel", "parallel"), V7X_VMEM_LIMIT_BYTES),
        name="outproj",
    )(x, ypool, og, om, wop, wog, wom, gt1, g2, sh2, sc2, w_router)


def _route_kernel(a_ref, posm_ref, pos_ref, *, cap):
    a = a_ref[0]
    n_e, nc, ln = a.shape
    bits = pltpu.bitcast(a, i32)

    def count(mask):
        c = jnp.sum(jnp.where(mask, 1.0, 0.0), axis=2, keepdims=True)
        return jnp.sum(c, axis=1, keepdims=True)

    thr = jnp.zeros((n_e, 1, 1), i32)
    for bit in range(30, -1, -1):
        cand = thr | (1 << bit)
        thr = jnp.where(count(bits >= cand) >= cap, cand, thr)
    gt = bits > thr
    eq = bits == thr
    need = cap - count(gt)

    r_i = lax.broadcasted_iota(i32, (ln, ln), 0)
    c_i = lax.broadcasted_iota(i32, (ln, ln), 1)
    tri_incl = jnp.where(r_i <= c_i, 1.0, 0.0).astype(bf16)
    r_c = lax.broadcasted_iota(i32, (nc, nc), 0)
    c_c = lax.broadcasted_iota(i32, (nc, nc), 1)
    tri_strict = jnp.where(c_c < r_c, 1.0, 0.0).astype(bf16)

    def excl_prefix(mask):
        x = jnp.where(mask, 1.0, 0.0)
        incl = _dot(x.astype(bf16).reshape(n_e * nc, ln), tri_incl).reshape(n_e, nc, ln)
        tot = jnp.broadcast_to(incl[:, :, ln - 1:ln], (n_e, nc, ln))
        tot_hi = tot.astype(bf16)
        offs = [_dot(tri_strict, tot_hi[e]) for e in range(n_e)]
        return jnp.stack(offs, axis=0) + incl - x

    sel = gt | (eq & (excl_prefix(eq) < need))
    pos = excl_prefix(sel).astype(i32)
    pos_ref[0] = pos
    posm_ref[0] = jnp.where(sel, pos, -1)


def _route(aff_r, *, cap):
    b, n_e, nc, ln = aff_r.shape
    spec = pl.BlockSpec((1, n_e, nc, ln), lambda bi: (bi, 0, 0, 0))
    return pl.pallas_call(
        functools.partial(_route_kernel, cap=cap),
        grid=(b,),
        in_specs=[spec],
        out_specs=(spec, spec),
        out_shape=(jax.ShapeDtypeStruct(aff_r.shape, i32), jax.ShapeDtypeStruct(aff_r.shape, i32)),
        compiler_params=_cparams(("parallel",), V7X_VMEM_LIMIT_BYTES),
        name="route",
    )(aff_r)


def _window_start(lo, width, cap):
    aligned = lax.shift_left(lax.shift_right_logical(lo, SLOT_ALIGN_LOG2), SLOT_ALIGN_LOG2)
    return pl.multiple_of(jnp.minimum(aligned, cap - width), 1 << SLOT_ALIGN_LOG2)


def _window_widths(cap):
    return min(FAST_SLOTS, cap), min(TOK_CHUNK + (1 << SLOT_ALIGN_LOG2), cap)


def _gather_kernel(offs_ref, h_ref, posm_ref, xe_ref, *, n_off, n_chunks):
    b, e = pl.program_id(0), pl.program_id(1)
    base = (b * pl.num_programs(1) + e) * n_off
    step = TOK_CHUNK // V7X_LANES
    cap = xe_ref.shape[2]
    fast_w, slow_w = _window_widths(cap)
    xe_ref[...] = jnp.zeros_like(xe_ref)

    def misfit(c, bad):
        lo = offs_ref[base + c * step]
        hi = offs_ref[base + (c + 1) * step]
        return bad + (hi - _window_start(lo, fast_w, cap) > fast_w).astype(i32)

    bad = lax.fori_loop(0, n_chunks, misfit, jnp.int32(0))

    def run(width):
        slot_iota = lax.broadcasted_iota(i32, (width, TOK_CHUNK), 0)

        def chunk(c, carry):
            w = _window_start(offs_ref[base + c * step], width, cap)
            tok0 = pl.multiple_of(c * TOK_CHUNK, TOK_CHUNK)
            pr = posm_ref[0, 0, c]
            onehot = jnp.where(pr == slot_iota + w, 1.0, 0.0).astype(bf16)
            rows = _dot(onehot, h_ref[0, pl.ds(tok0, TOK_CHUNK), :]).astype(xe_ref.dtype)
            xe_ref[0, 0, pl.ds(w, width), :] = xe_ref[0, 0, pl.ds(w, width), :] + rows
            return carry

        lax.fori_loop(0, n_chunks, chunk, 0, unroll=2)

    @pl.when(bad == 0)
    def _():
        run(fast_w)

    @pl.when(bad != 0)
    def _():
        run(slow_w)


def _gather(offs, h2, posm_c, *, cap):
    b, s, d = h2.shape
    n_e = posm_c.shape[1]
    n_chunks = s // TOK_CHUNK
    n_off = s // V7X_LANES + 1
    return pl.pallas_call(
        functools.partial(_gather_kernel, n_off=n_off, n_chunks=n_chunks),
        grid_spec=pltpu.PrefetchScalarGridSpec(
            num_scalar_prefetch=1,
            grid=(b, n_e),
            in_specs=[
                pl.BlockSpec((1, s, d), lambda bi, e, offs: (bi, 0, 0), pipeline_mode=pl.Buffered(1)),
                pl.BlockSpec((1, 1, n_chunks, 1, TOK_CHUNK), lambda bi, e, offs: (bi, e, 0, 0, 0)),
            ],
            out_specs=pl.BlockSpec((1, 1, cap, d), lambda bi, e, offs: (bi, e, 0, 0)),
        ),
        out_shape=jax.ShapeDtypeStruct((b, n_e, cap, d), bf16),
        compiler_params=_cparams(("arbitrary", "arbitrary"), V7X_VMEM_LIMIT_BYTES),
        name="gather",
    )(offs, h2, posm_c)


def _ffn_kernel(x_ref, wg_ref, wu_ref, wd_ref, o_ref, acc_ref):
    f = pl.program_id(2)
    x = x_ref[0, 0]
    a = _dot(x, wg_ref[0, 0].astype(bf16))
    u = _dot(x, wu_ref[0, 0].astype(bf16))
    hmid = (a * (1.0 / (1.0 + jnp.exp(-a))) * u).astype(bf16)
    part = _dot(hmid, wd_ref[0, 0].astype(bf16))

    @pl.when(f == 0)
    def _():
        acc_ref[...] = part

    @pl.when(f > 0)
    def _():
        acc_ref[...] += part

    @pl.when(f == pl.num_programs(2) - 1)
    def _():
        o_ref[0, 0] = acc_ref[...].astype(o_ref.dtype)


def _ffn(xe, w_gate, w_up, w_down, layer):
    b, n_e, cap, d = xe.shape
    d_ff = w_gate.shape[-1]
    n_f = d_ff // FF_TILE
    return pl.pallas_call(
        _ffn_kernel,
        grid=(n_e, b, n_f),
        in_specs=[
            pl.BlockSpec((1, 1, cap, d), lambda e, bi, f: (bi, e, 0, 0)),
            pl.BlockSpec((1, 1, d, FF_TILE), lambda e, bi, f: (layer, e, 0, f)),
            pl.BlockSpec((1, 1, d, FF_TILE), lambda e, bi, f: (layer, e, 0, f)),
            pl.BlockSpec((1, 1, FF_TILE, d), lambda e, bi, f: (layer, e, f, 0)),
        ],
        out_specs=pl.BlockSpec((1, 1, cap, d), lambda e, bi, f: (bi, e, 0, 0)),
        out_shape=jax.ShapeDtypeStruct((b, n_e, cap, d), bf16),
        scratch_shapes=[pltpu.VMEM((cap, d), f32)],
        compiler_params=_cparams(("parallel", "parallel", "arbitrary"), V7X_VMEM_LIMIT_BYTES),
        name="expert_ffn",
    )(xe, w_gate, w_up, w_down)


def _combine_kernel(offs_ref, x_ref, aff_ref, posm_ref, gt_ref, ye_ref, o_ref, *, n_off):
    b, i = pl.program_id(0), pl.program_id(2)
    n_e, cap = ye_ref.shape[1], ye_ref.shape[2]
    fast_w, slow_w = _window_widths(cap)
    step = TOK_CHUNK // V7X_LANES
    posm = posm_ref[0]
    aff = aff_ref[0]
    los = [offs_ref[(b * n_e + e) * n_off + i * step] for e in range(n_e)]
    his = [offs_ref[(b * n_e + e) * n_off + (i + 1) * step] for e in range(n_e)]
    bad = functools.reduce(
        lambda a, c: a + c, [(his[e] - _window_start(los[e], fast_w, cap) > fast_w).astype(i32) for e in range(n_e)])

    def run(width):
        slot_iota = lax.broadcasted_iota(i32, (TOK_CHUNK, width), 1)
        total = jnp.zeros((TOK_CHUNK, o_ref.shape[2]), f32)
        for e in range(n_e):
            w = _window_start(los[e], width, cap)
            onehot = jnp.where(posm[:, e:e + 1] == slot_iota + w, 1.0, 0.0).astype(bf16)
            total = total + aff[:, e:e + 1] * _dot(onehot, ye_ref[0, e, pl.ds(w, width), :])
        o_ref[0] = x_ref[0] + gt_ref[0] * total

    @pl.when(bad == 0)
    def _():
        run(fast_w)

    @pl.when(bad != 0)
    def _():
        run(slow_w)


def _combine(offs, x1, aff, posm_t, gt2, ye):
    b, s, d = x1.shape
    n_e, cap = ye.shape[1], ye.shape[2]
    n_off = s // V7X_LANES + 1
    return pl.pallas_call(
        functools.partial(_combine_kernel, n_off=n_off),
        grid_spec=pltpu.PrefetchScalarGridSpec(
            num_scalar_prefetch=1,
            grid=(b, d // COL_TILE, s // TOK_CHUNK),
            in_specs=[
                pl.BlockSpec((1, TOK_CHUNK, COL_TILE), lambda bi, j, i, offs: (bi, i, j)),
                pl.BlockSpec((1, TOK_CHUNK, n_e), lambda bi, j, i, offs: (bi, i, 0)),
                pl.BlockSpec((1, TOK_CHUNK, n_e), lambda bi, j, i, offs: (bi, i, 0)),
                pl.BlockSpec((1, 1, COL_TILE), lambda bi, j, i, offs: (bi, 0, j)),
                pl.BlockSpec((1, n_e, cap, COL_TILE), lambda bi, j, i, offs: (bi, 0, 0, j),
                             pipeline_mode=pl.Buffered(1)),
            ],
            out_specs=pl.BlockSpec((1, TOK_CHUNK, COL_TILE), lambda bi, j, i, offs: (bi, i, j)),
        ),
        out_shape=jax.ShapeDtypeStruct((b, s, d), f32),
        compiler_params=_cparams(("arbitrary", "arbitrary", "arbitrary"), V7X_VMEM_LIMIT_BYTES),
        name="combine",
    )(offs, x1, aff, posm_t, gt2, ye)


def _final_kernel(x_ref, g_ref, o_ref):
    x = x_ref[0]
    o_ref[0] = x * lax.rsqrt(jnp.mean(x * x, axis=-1, keepdims=True) + EPS) * g_ref[...]


def _final_norm(x, g, *, tt):
    b, s, d = x.shape
    return pl.pallas_call(
        _final_kernel,
        grid=(b, s // tt),
        in_specs=[pl.BlockSpec((1, tt, d), lambda bi, i: (bi, i, 0)), pl.BlockSpec((1, d), lambda bi, i: (0, 0))],
        out_specs=pl.BlockSpec((1, tt, d), lambda bi, i: (bi, i, 0)),
        out_shape=jax.ShapeDtypeStruct((b, s, d), f32),
        compiler_params=_cparams(("parallel", "parallel")),
        name="final_norm",
    )(x, g)


def _deinterleave(n):
    return np.concatenate([np.arange(0, n, 2), np.arange(1, n, 2)])


def _rope_tables_t(n, d_rot):
    n_rows = n // GRID_W
    row = jnp.repeat(jnp.arange(n_rows, dtype=f32), GRID_W)
    col = jnp.tile(jnp.arange(GRID_W, dtype=f32), n_rows)
    n_freq = d_rot // 4
    inv_freq = ROPE_THETA ** (-jnp.arange(n_freq, dtype=f32) / n_freq)
    ang = jnp.concatenate([row[:, None] * inv_freq, col[:, None] * inv_freq], axis=-1)
    return jnp.cos(ang).T, jnp.sin(ang).T


def _rest_columns():
    p64, p32 = _deinterleave(HEAD_DIM), _deinterleave(MLA_ROPE_DIM)
    cols = [OFF_GQA_Q + h * HEAD_DIM + p64 for h in range(GQA_HEADS)]
    cols += [OFF_GQA_K + h * HEAD_DIM + p64 for h in range(GQA_KV_HEADS)]
    cols += [np.arange(OFF_GQA_V, IN_DIM - MLA_ROPE_DIM), OFF_MLA_ROPE + p32]
    return np.concatenate(cols)


def _uq_columns():
    p32 = _deinterleave(MLA_ROPE_DIM)
    cols = []
    for h in range(MLA_HEADS):
        cols += [h * MLA_QK_DIM + np.arange(MLA_NOPE_DIM), h * MLA_QK_DIM + MLA_NOPE_DIM + p32]
    return np.concatenate(cols)


def _block_diag(w):
    g, c, _ = w.shape
    out = jnp.zeros((g * c, g * c), w.dtype)
    for i in range(g):
        out = out.at[i * c:(i + 1) * c, i * c:(i + 1) * c].set(w[i])
    return out


def _trunk(x, c, w_mod, b_mod, g_norm1, w_in, pool_w, pool_scale, gqa_q_gain, gqa_k_gain, mla_q_gain, mla_kv_gain,
           mla_w_uq, mla_w_ukv, w_out, g_norm2, w_router, w_gate, w_up, w_down, g_final):
    b, s, d = x.shape
    depth = w_mod.shape[0]
    cap = (EC_CAPACITY * s) // N_EXPERTS
    tt = min(TOK_TILE, s)
    tq = min(ATT_TQ, s)
    tk = min(ATT_TK, s)
    nc = s // V7X_LANES

    mod_rows = 8
    c_pad = jnp.zeros((mod_rows, d), f32).at[:b].set(c)
    mod = _modulation(c_pad, w_mod, b_mod)[:, :b].reshape(depth, b, 6, 1, d)

    cg, sg = _rope_tables_t(s, HEAD_DIM)
    cm, sm = _rope_tables_t(s, MLA_ROPE_DIM)
    p64 = _deinterleave(HEAD_DIM)
    rest_cols, uq_cols = _rest_columns(), _uq_columns()

    for l in range(depth):
        sh1, sc1, gt1, sh2, sc2, gt2 = (mod[l, :, k] for k in range(6))
        wp = w_in[l][:, :POOL_DIM].astype(bf16)
        wr = w_in[l][:, rest_cols].T.astype(bf16)
        wuq = mla_w_uq[l][:, uq_cols].T.astype(bf16)
        wukv = mla_w_ukv[l].T.astype(bf16)
        u, qg, kg, kng, vg, qm, km, knm, vm = _inproj(
            x, g_norm1[l][None], sh1, sc1, wp, wr,
            gqa_q_gain[l][p64][:, None], gqa_k_gain[l][p64][:, None], mla_q_gain[l][:, None], mla_kv_gain[l][:, None],
            wuq, wukv, cg, sg, cm, sm, tt=tt, tk=tk)
        og = _attention(qg, kg, kng, vg, group=GQA_GROUP, tq=min(GQA_TQ, s), n_sub=min(GQA_TQ, s) // tq)
        om = _attention(qm, km, knm, vm, group=1, tq=min(MLA_TQ, s), n_sub=min(MLA_TQ, s) // tq)
        ypool = _pool(u, _block_diag(pool_w[l]).astype(bf16), pool_scale[l][None], tp=tt)
        wo = w_out[l].astype(bf16)
        n_g = GQA_HEADS * HEAD_DIM
        x1, h2, aff = _outproj(x, ypool, og, om, wo[:POOL_DIM], wo[POOL_DIM:POOL_DIM + n_g], wo[POOL_DIM + n_g:],
                               gt1, g_norm2[l][None], sh2, sc2, w_router[l], tt=tt)

        aff_r = aff.transpose(0, 2, 1).reshape(b, N_EXPERTS, nc, V7X_LANES)
        posm, pos = _route(aff_r, cap=cap)
        offs = jnp.concatenate([pos[..., 0], jnp.full((b, N_EXPERTS, 1), cap, i32)], axis=-1).reshape(-1)
        posm_c = posm.reshape(b, N_EXPERTS, s // TOK_CHUNK, 1, TOK_CHUNK)
        posm_t = posm.reshape(b, N_EXPERTS, s).transpose(0, 2, 1)
        xe = _gather(offs, h2, posm_c, cap=cap)
        ye = _ffn(xe, w_gate, w_up, w_down, l)
        x = _combine(offs, x1, aff, posm_t, gt2, ye)
    return _final_norm(x, g_final[None], tt=tt)


def kernel(x, c, w_mod, b_mod, g_norm1, w_in, pool_w, pool_scale, gqa_q_gain, gqa_k_gain, mla_q_gain, mla_kv_gain,
           mla_w_uq, mla_w_ukv, w_out, g_norm2, w_router, w_gate, w_up, w_down, g_final):
    return _trunk(x, c, w_mod, b_mod, g_norm1, w_in, pool_w, pool_scale, gqa_q_gain, gqa_k_gain, mla_q_gain,
                  mla_kv_gain, mla_w_uq, mla_w_ukv, w_out, g_norm2, w_router, w_gate, w_up, w_down, g_final)
```

```python
import functools
import math

import numpy as np
import jax
import jax.numpy as jnp
from jax import lax
from jax.experimental import pallas as pl
from jax.experimental.pallas import tpu as pltpu

f32, bf16, i32 = jnp.float32, jnp.bfloat16, jnp.int32

D_MODEL = 1024
DEPTH = 4
GRID_W = 64
ROPE_THETA = 10000.0
EPS = 1e-6
POOL_DIM = 256
POOL_WINDOWS = (2, 4, 8, 16)
POOL_CH = 64
HEAD_DIM = 64
GQA_HEADS = 6
GQA_KV_HEADS = 2
GQA_GROUP = 3
MLA_HEADS = 6
MLA_NOPE_DIM = 64
MLA_ROPE_DIM = 32
MLA_QK_DIM = 96
MLA_V_DIM = 64
MLA_Q_RANK = 256
MLA_KV_RANK = 256
OFF_GQA_Q = POOL_DIM
OFF_GQA_K = OFF_GQA_Q + GQA_HEADS * HEAD_DIM
OFF_GQA_V = OFF_GQA_K + GQA_KV_HEADS * HEAD_DIM
OFF_MLA_Q = OFF_GQA_V + GQA_KV_HEADS * HEAD_DIM
OFF_MLA_KV = OFF_MLA_Q + MLA_Q_RANK
OFF_MLA_ROPE = OFF_MLA_KV + MLA_KV_RANK
IN_DIM = OFF_MLA_ROPE + MLA_ROPE_DIM
REST_DIM = IN_DIM - POOL_DIM
N_EXPERTS = 16
EC_CAPACITY = 2
D_FF = 2048

R_GQ = 0
R_GK = R_GQ + GQA_HEADS * HEAD_DIM
R_GV = R_GK + GQA_KV_HEADS * HEAD_DIM
R_MQ = R_GV + GQA_KV_HEADS * HEAD_DIM
R_MKV = R_MQ + MLA_Q_RANK
R_MR = R_MKV + MLA_KV_RANK

V7X_LANES = 128
V7X_VMEM_LIMIT_BYTES = 60000 * 1024
V_ROWS = 64
V7X_SUBLANES = 8
K_PAD = 16

TOK_TILE = 512
ATT_TQ = 512
GQA_TQ = 1024
MLA_TQ = 2048
ATT_TK = 512
TOK_CHUNK = 256
FAST_SLOTS = 64
SLOT_ALIGN_LOG2 = 4
FF_TILE = 512
COL_TILE = 512
LOG2E = math.log2(math.e)


def _cparams(sem, vmem=None):
    return pltpu.CompilerParams(dimension_semantics=sem, vmem_limit_bytes=vmem)


def _split_bf16(a):
    hi = a.astype(bf16)
    lo = (a - hi.astype(f32)).astype(bf16)
    return hi, lo


def _dot(a, b):
    return jnp.dot(a, b, preferred_element_type=f32)


def _dot3(a, b):
    ah, al = _split_bf16(a)
    bh, bl = _split_bf16(b)
    return _dot(ah, bh) + _dot(ah, bl) + _dot(al, bh)


def _mod_kernel(c_ref, w_ref, b_ref, o_ref):
    c = c_ref[...]
    act = c * (1.0 / (1.0 + jnp.exp(-c)))
    o_ref[0] = _dot3(act, w_ref[0]) + b_ref[0]


def _modulation(c_pad, w_mod, b_mod):
    depth, d, six_d = w_mod.shape
    rows = c_pad.shape[0]
    return pl.pallas_call(
        _mod_kernel,
        grid=(depth, six_d // d),
        in_specs=[
            pl.BlockSpec((rows, d), lambda l, j: (0, 0)),
            pl.BlockSpec((1, d, d), lambda l, j: (l, 0, j)),
            pl.BlockSpec((1, 1, d), lambda l, j: (l, 0, j)),
        ],
        out_specs=pl.BlockSpec((1, rows, d), lambda l, j: (l, 0, j)),
        out_shape=jax.ShapeDtypeStruct((depth, rows, six_d), f32),
        compiler_params=_cparams(("parallel", "parallel")),
        name="modulation",
    )(c_pad, w_mod, b_mod.reshape(depth, 1, six_d))


def _rms_rows(z, gain_col):
    r = lax.rsqrt(jnp.mean(z * z, axis=0, keepdims=True) + EPS)
    return z * r * gain_col


def _rope_rows(z, cos, sin):
    half = z.shape[0] // 2
    x1, x2 = z[:half], z[half:]
    return jnp.concatenate([x1 * cos - x2 * sin, x1 * sin + x2 * cos], axis=0)


def _inproj_kernel(x_ref, g_ref, sh_ref, sc_ref, wp_ref, wr_ref, gq_ref, gk_ref, gmq_ref, gmkv_ref,
                   wuq_ref, wukv_ref, cg_ref, sg_ref, cm_ref, sm_ref,
                   u_ref, qg_ref, kg_ref, kng_ref, vg_ref, qm_ref, km_ref, knm_ref, vm_ref, *, tk):
    x = x_ref[0]
    tt = x.shape[0]
    h = x * lax.rsqrt(jnp.mean(x * x, axis=-1, keepdims=True) + EPS) * g_ref[...]
    h = h * (1.0 + sc_ref[0]) + sh_ref[0]
    hb = h.astype(bf16)
    u_ref[0] = _dot(hb, wp_ref[...])
    zt = lax.dot_general(wr_ref[...], hb, (((1,), (1,)), ((), ())), preferred_element_type=f32)

    cg, sg, cm, sm = cg_ref[...], sg_ref[...], cm_ref[...], sm_ref[...]
    n_sub = tt // tk

    def put_k(ref, norm_ref, head, kt):
        kb = kt.astype(bf16)
        kf = kb.astype(f32)
        norm_ref[0, head] = jnp.sqrt(jnp.sum(kf * kf, axis=0, keepdims=True))
        pad = jnp.where(lax.broadcasted_iota(i32, (K_PAD, tt), 0) == 0, 1.0, 0.0)
        ke = jnp.concatenate([kt, pad], axis=0)
        ref[0, head] = ke.T.astype(bf16)

    def put_v(ref, head, vt):
        ve = vt.astype(bf16)
        for j in range(n_sub):
            ref[0, head, j] = ve[:, j * tk:(j + 1) * tk]

    gq = gq_ref[...] * (HEAD_DIM ** -0.5 * LOG2E)
    gk = gk_ref[...]
    for hd in range(GQA_HEADS):
        q = _rms_rows(zt[R_GQ + hd * HEAD_DIM:R_GQ + (hd + 1) * HEAD_DIM], gq)
        qg_ref[0, hd] = _rope_rows(q, cg, sg).astype(bf16)
    for hk in range(GQA_KV_HEADS):
        k = _rms_rows(zt[R_GK + hk * HEAD_DIM:R_GK + (hk + 1) * HEAD_DIM], gk)
        put_k(kg_ref, kng_ref, hk, _rope_rows(k, cg, sg))
        put_v(vg_ref, hk, zt[R_GV + hk * HEAD_DIM:R_GV + (hk + 1) * HEAD_DIM])

    cq = _rms_rows(zt[R_MQ:R_MQ + MLA_Q_RANK], gmq_ref[...]).astype(bf16)
    qm = _dot(wuq_ref[...], cq) * (MLA_QK_DIM ** -0.5 * LOG2E)
    ckv = _rms_rows(zt[R_MKV:R_MKV + MLA_KV_RANK], gmkv_ref[...]).astype(bf16)
    kv = _dot(wukv_ref[...], ckv)
    k_rope = _rope_rows(zt[R_MR:R_MR + MLA_ROPE_DIM], cm, sm)
    for hd in range(MLA_HEADS):
        qh = qm[hd * MLA_QK_DIM:(hd + 1) * MLA_QK_DIM]
        qr = _rope_rows(qh[MLA_NOPE_DIM:], cm, sm)
        qm_ref[0, hd] = jnp.concatenate([qh[:MLA_NOPE_DIM], qr], axis=0).astype(bf16)
        kvh = kv[hd * (MLA_NOPE_DIM + MLA_V_DIM):(hd + 1) * (MLA_NOPE_DIM + MLA_V_DIM)]
        kh = jnp.concatenate([kvh[:MLA_NOPE_DIM], k_rope], axis=0)
        put_k(km_ref, knm_ref, hd, kh)
        put_v(vm_ref, hd, kvh[MLA_NOPE_DIM:])


def _inproj(x, g1, sh1, sc1, wp, wr, gq, gk, gmq, gmkv, wuq, wukv, cg, sg, cm, sm, *, tt, tk):
    b, s, d = x.shape
    n_t = s // tt
    n_sub = tt // tk
    full = lambda shape: pl.BlockSpec(shape, lambda bi, i: (0,) * len(shape))
    vec = pl.BlockSpec((1, 1, d), lambda bi, i: (bi, 0, 0))
    rope_g = pl.BlockSpec((HEAD_DIM // 2, tt), lambda bi, i: (0, i))
    rope_m = pl.BlockSpec((MLA_ROPE_DIM // 2, tt), lambda bi, i: (0, i))
    out_shapes = (
        jax.ShapeDtypeStruct((b, s, POOL_DIM), f32),
        jax.ShapeDtypeStruct((b, GQA_HEADS, HEAD_DIM, s), bf16),
        jax.ShapeDtypeStruct((b, GQA_KV_HEADS, s, HEAD_DIM + K_PAD), bf16),
        jax.ShapeDtypeStruct((b, GQA_KV_HEADS, 1, s), f32),
        jax.ShapeDtypeStruct((b, GQA_KV_HEADS, s // tk, V_ROWS, tk), bf16),
        jax.ShapeDtypeStruct((b, MLA_HEADS, MLA_QK_DIM, s), bf16),
        jax.ShapeDtypeStruct((b, MLA_HEADS, s, MLA_QK_DIM + K_PAD), bf16),
        jax.ShapeDtypeStruct((b, MLA_HEADS, 1, s), f32),
        jax.ShapeDtypeStruct((b, MLA_HEADS, s // tk, V_ROWS, tk), bf16),
    )
    out_specs = (
        pl.BlockSpec((1, tt, POOL_DIM), lambda bi, i: (bi, i, 0)),
        pl.BlockSpec((1, GQA_HEADS, HEAD_DIM, tt), lambda bi, i: (bi, 0, 0, i)),
        pl.BlockSpec((1, GQA_KV_HEADS, tt, HEAD_DIM + K_PAD), lambda bi, i: (bi, 0, i, 0)),
        pl.BlockSpec((1, GQA_KV_HEADS, 1, tt), lambda bi, i: (bi, 0, 0, i)),
        pl.BlockSpec((1, GQA_KV_HEADS, n_sub, V_ROWS, tk), lambda bi, i: (bi, 0, i, 0, 0)),
        pl.BlockSpec((1, MLA_HEADS, MLA_QK_DIM, tt), lambda bi, i: (bi, 0, 0, i)),
        pl.BlockSpec((1, MLA_HEADS, tt, MLA_QK_DIM + K_PAD), lambda bi, i: (bi, 0, i, 0)),
        pl.BlockSpec((1, MLA_HEADS, 1, tt), lambda bi, i: (bi, 0, 0, i)),
        pl.BlockSpec((1, MLA_HEADS, n_sub, V_ROWS, tk), lambda bi, i: (bi, 0, i, 0, 0)),
    )
    return pl.pallas_call(
        functools.partial(_inproj_kernel, tk=tk),
        grid=(b, n_t),
        in_specs=[
            pl.BlockSpec((1, tt, d), lambda bi, i: (bi, i, 0)),
            full((1, d)), vec, vec,
            full(wp.shape), full(wr.shape), full(gq.shape), full(gk.shape), full(gmq.shape), full(gmkv.shape),
            full(wuq.shape), full(wukv.shape), rope_g, rope_g, rope_m, rope_m,
        ],
        out_specs=out_specs,
        out_shape=out_shapes,
        compiler_params=_cparams(("parallel", "parallel"), V7X_VMEM_LIMIT_BYTES),
        name="inproj",
    )(x, g1, sh1, sc1, wp, wr, gq, gk, gmq, gmkv, wuq, wukv, cg, sg, cm, sm)


SAFE_LOGIT_BOUND = 50.0


def _attn_kernel(q_ref, k_ref, kn_ref, v_ref, o_ref, s0_ref, m_ref, l_ref, acc_ref, *, tk, n_chunks, dv, n_sub):
    group, tq = q_ref.shape[1], q_ref.shape[3]
    ts = tq // n_sub
    streams = [(h, j) for h in range(group) for j in range(n_sub)]
    qts = [q_ref[0, h, :, j * ts:(j + 1) * ts] for h, j in streams]

    def k_chunk(c):
        return k_ref[0, 0, pl.ds(pl.multiple_of(c * tk, tk), tk), :]

    def finish():
        for n, (h, j) in enumerate(streams):
            l = jnp.sum(l_ref[n], axis=0, keepdims=True)
            o_ref[0, h * dv:(h + 1) * dv, j * ts:(j + 1) * ts] = (acc_ref[n] / l).astype(o_ref.dtype)

    def row_sums(p):
        return jnp.sum(p.reshape(tk // V7X_SUBLANES, V7X_SUBLANES, ts), axis=0)

    k_max = jnp.max(kn_ref[0, 0], axis=-1, keepdims=True)
    bounds = []
    for qt in qts:
        qf = qt.astype(f32)
        bounds.append(jnp.sqrt(jnp.sum(qf * qf, axis=0, keepdims=True)) * k_max)
    worst = functools.reduce(jnp.maximum, [jnp.max(u, axis=-1, keepdims=True) for u in bounds])
    safe = worst[0, 0] <= SAFE_LOGIT_BOUND

    @pl.when(safe)
    def _():
        row0 = lax.broadcasted_iota(i32, (K_PAD, ts), 0) == 0
        qes = [jnp.concatenate([qt, jnp.where(row0, -c, 0.0).astype(qt.dtype)], axis=0) for qt, c in zip(qts, bounds)]
        acc_ref[...] = jnp.zeros(acc_ref.shape, f32)
        l_ref[...] = jnp.zeros(l_ref.shape, f32)
        s0_ref[...] = _dot(k_chunk(0), qes[0])
        order = [(dc, n) for dc in range(2) for n in range(len(streams))]

        def step(i, carry):
            c = 2 * i
            kcs = [k_chunk(c), k_chunk(c + 1), k_chunk(jnp.minimum(c + 2, n_chunks - 1))]
            vcs = [v_ref[0, 0, c], v_ref[0, 0, c + 1]]
            s_cur = s0_ref[...]
            for idx, (dc, n) in enumerate(order):
                dc2, n2 = order[idx + 1] if idx + 1 < len(order) else (2, 0)
                s_next = _dot(kcs[dc2], qes[n2])
                p = jnp.exp2(s_cur)
                l_ref[n] += row_sums(p)
                acc_ref[n] += _dot(vcs[dc], p.astype(bf16))
                s_cur = s_next
            s0_ref[...] = s_cur
            return carry

        lax.fori_loop(0, n_chunks // 2, step, 0)
        finish()

    @pl.when(jnp.logical_not(safe))
    def _():
        m_ref[...] = jnp.full(m_ref.shape, -jnp.inf, f32)
        acc_ref[...] = jnp.zeros(acc_ref.shape, f32)
        l_ref[...] = jnp.zeros(l_ref.shape, f32)
        qzs = [jnp.concatenate([qt, jnp.zeros((K_PAD, ts), qt.dtype)], axis=0) for qt in qts]

        def step(c, carry):
            kc = k_chunk(c)
            vc = v_ref[0, 0, c]
            for n, qz in enumerate(qzs):
                s = _dot(kc, qz)
                m = m_ref[n]
                m_new = jnp.maximum(m, jnp.max(s, axis=0, keepdims=True))
                p = jnp.exp2(s - m_new)
                alpha = jnp.exp2(m - m_new)
                l_ref[n] = l_ref[n] * alpha + row_sums(p)
                acc_ref[n] = acc_ref[n] * alpha + _dot(vc, p.astype(bf16))
                m_ref[n] = m_new
            return carry

        lax.fori_loop(0, n_chunks, step, 0)
        finish()


def _attention(qt, k, kn, vt, *, group, tq, n_sub):
    b, hq, dq, s = qt.shape
    _, hk, n_chunks, v_rows, tk = vt.shape
    dv = MLA_V_DIM
    assert n_chunks % 2 == 0 and tq % n_sub == 0
    n_streams, ts = group * n_sub, tq // n_sub
    return pl.pallas_call(
        functools.partial(_attn_kernel, tk=tk, n_chunks=n_chunks, dv=dv, n_sub=n_sub),
        grid=(b, hk, s // tq),
        scratch_shapes=[
            pltpu.VMEM((tk, ts), f32),
            pltpu.VMEM((n_streams, 1, ts), f32),
            pltpu.VMEM((n_streams, V7X_SUBLANES, ts), f32),
            pltpu.VMEM((n_streams, v_rows, ts), f32),
        ],
        in_specs=[
            pl.BlockSpec((1, group, dq, tq), lambda bi, g, i: (bi, g, 0, i)),
            pl.BlockSpec((1, 1, s, dq + K_PAD), lambda bi, g, i: (bi, g, 0, 0)),
            pl.BlockSpec((1, 1, 1, s), lambda bi, g, i: (bi, g, 0, 0)),
            pl.BlockSpec((1, 1, n_chunks, v_rows, tk), lambda bi, g, i: (bi, g, 0, 0, 0)),
        ],
        out_specs=pl.BlockSpec((1, group * dv, tq), lambda bi, g, i: (bi, g, i)),
        out_shape=jax.ShapeDtypeStruct((b, hq * dv, s), bf16),
        compiler_params=_cparams(("parallel", "parallel", "parallel"), V7X_VMEM_LIMIT_BYTES),
        name="attention",
    )(qt, k, kn, vt)


POOL_HALO = 16


def _pool_kernel(up_ref, uc_ref, un_ref, w_ref, sc_ref, o_ref, *, seq):
    i = pl.program_id(1)
    n_t = pl.num_programs(1)
    cur = uc_ref[0]
    tp = cur.shape[0]
    prev = jnp.where(i > 0, up_ref[0], 0.0)
    nxt = jnp.where(i < n_t - 1, un_ref[0], 0.0)
    ext = jnp.concatenate([prev, cur, nxt], axis=0)
    t = i * tp + lax.broadcasted_iota(i32, (tp, 1), 0)
    parts = []
    for gi, win in enumerate(POOL_WINDOWS):
        lo = win // 2
        hi = win - 1 - lo
        cols = slice(gi * POOL_CH, (gi + 1) * POOL_CH)
        eg = ext[:, cols]
        wsum = eg[POOL_HALO - lo:POOL_HALO - lo + tp]
        for j in range(1, win):
            wsum = wsum + eg[POOL_HALO - lo + j:POOL_HALO - lo + j + tp]
        cnt = (jnp.minimum(t + hi, seq - 1) - jnp.maximum(t - lo, 0) + 1).astype(f32)
        parts.append(wsum / cnt - cur[:, cols])
    p = jnp.concatenate(parts, axis=1).astype(bf16)
    o_ref[0] = (_dot(p, w_ref[...]) * sc_ref[...]).astype(o_ref.dtype)


def _pool(u, w_bd, scale, *, tp):
    b, s, c = u.shape
    n_t = s // tp
    r = tp // POOL_HALO
    return pl.pallas_call(
        functools.partial(_pool_kernel, seq=s),
        grid=(b, n_t),
        in_specs=[
            pl.BlockSpec((1, POOL_HALO, c), lambda bi, i: (bi, jnp.maximum(i * r - 1, 0), 0)),
            pl.BlockSpec((1, tp, c), lambda bi, i: (bi, i, 0)),
            pl.BlockSpec((1, POOL_HALO, c), lambda bi, i: (bi, jnp.minimum((i + 1) * r, s // POOL_HALO - 1), 0)),
            pl.BlockSpec(w_bd.shape, lambda bi, i: (0, 0)),
            pl.BlockSpec((1, c), lambda bi, i: (0, 0)),
        ],
        out_specs=pl.BlockSpec((1, tp, c), lambda bi, i: (bi, i, 0)),
        out_shape=jax.ShapeDtypeStruct((b, s, c), bf16),
        compiler_params=_cparams(("parallel", "parallel")),
        name="pool",
    )(u, u, u, w_bd, scale)


def _outproj_kernel(x_ref, yp_ref, og_ref, om_ref, wop_ref, wog_ref, wom_ref, gt_ref, g2_ref, sh_ref, sc_ref,
                    wrt_ref, x1_ref, h2_ref, aff_ref):
    tn = (((0,), (0,)), ((), ()))
    y = _dot(yp_ref[0], wop_ref[...])
    y = y + lax.dot_general(og_ref[0], wog_ref[...], tn, preferred_element_type=f32)
    y = y + lax.dot_general(om_ref[0], wom_ref[...], tn, preferred_element_type=f32)
    x1 = x_ref[0] + gt_ref[0] * y
    x1_ref[0] = x1
    h = x1 * lax.rsqrt(jnp.mean(x1 * x1, axis=-1, keepdims=True) + EPS) * g2_ref[...]
    h = h * (1.0 + sc_ref[0]) + sh_ref[0]
    h2_ref[0] = h.astype(bf16)
    logits = _dot3(h, wrt_ref[...])
    ex = jnp.exp(logits - jnp.max(logits, axis=-1, keepdims=True))
    aff_ref[0] = ex / jnp.sum(ex, axis=-1, keepdims=True)


def _outproj(x, ypool, og, om, wop, wog, wom, gt1, g2, sh2, sc2, w_router, *, tt):
    b, s, d = x.shape
    full = lambda shape: pl.BlockSpec(shape, lambda bi, i: (0,) * len(shape))
    vec = pl.BlockSpec((1, 1, d), lambda bi, i: (bi, 0, 0))
    n_e = w_router.shape[1]
    return pl.pallas_call(
        _outproj_kernel,
        grid=(b, s // tt),
        in_specs=[
            pl.BlockSpec((1, tt, d), lambda bi, i: (bi, i, 0)),
            pl.BlockSpec((1, tt, POOL_DIM), lambda bi, i: (bi, i, 0)),
            pl.BlockSpec((1, og.shape[1], tt), lambda bi, i: (bi, 0, i)),
            pl.BlockSpec((1, om.shape[1], tt), lambda bi, i: (bi, 0, i)),
            full(wop.shape), full(wog.shape), full(wom.shape),
            vec, full((1, d)), vec, vec, full(w_router.shape),
        ],
        out_specs=(
            pl.BlockSpec((1, tt, d), lambda bi, i: (bi, i, 0)),
            pl.BlockSpec((1, tt, d), lambda bi, i: (bi, i, 0)),
            pl.BlockSpec((1, tt, n_e), lambda bi, i: (bi, i, 0)),
        ),
        out_shape=(
            jax.ShapeDtypeStruct((b, s, d), f32),
            jax.ShapeDtypeStruct((b, s, d), bf16),
            jax.ShapeDtypeStruct((b, s, n_e), f32),
        ),
        compiler_params=_cparams(("parallel", "parallel"), V7X_VMEM_LIMIT_BYTES),
        name="outproj",
    )(x, ypool, og, om, wop, wog, wom, gt1, g2, sh2, sc2, w_router)


def _route_kernel(a_ref, posm_ref, pos_ref, *, cap):
    a = a_ref[0]
    n_e, nc, ln = a.shape
    bits = pltpu.bitcast(a, i32)

    def count(mask):
        c = jnp.sum(jnp.where(mask, 1.0, 0.0), axis=2, keepdims=True)
        return jnp.sum(c, axis=1, keepdims=True)

    thr = jnp.zeros((n_e, 1, 1), i32)
    for bit in range(30, -1, -1):
        cand = thr | (1 << bit)
        thr = jnp.where(count(bits >= cand) >= cap, cand, thr)
    gt = bits > thr
    eq = bits == thr
    need = cap - count(gt)

    r_i = lax.broadcasted_iota(i32, (ln, ln), 0)
    c_i = lax.broadcasted_iota(i32, (ln, ln), 1)
    tri_incl = jnp.where(r_i <= c_i, 1.0, 0.0).astype(bf16)
    r_c = lax.broadcasted_iota(i32, (nc, nc), 0)
    c_c = lax.broadcasted_iota(i32, (nc, nc), 1)
    tri_strict = jnp.where(c_c < r_c, 1.0, 0.0).astype(bf16)

    def excl_prefix(mask):
        x = jnp.where(mask, 1.0, 0.0)
        incl = _dot(x.astype(bf16).reshape(n_e * nc, ln), tri_incl).reshape(n_e, nc, ln)
        tot = jnp.broadcast_to(incl[:, :, ln - 1:ln], (n_e, nc, ln))
        tot_hi = tot.astype(bf16)
        offs = [_dot(tri_strict, tot_hi[e]) for e in range(n_e)]
        return jnp.stack(offs, axis=0) + incl - x

    sel = gt | (eq & (excl_prefix(eq) < need))
    pos = excl_prefix(sel).astype(i32)
    pos_ref[0] = pos
    posm_ref[0] = jnp.where(sel, pos, -1)


def _route(aff_r, *, cap):
    b, n_e, nc, ln = aff_r.shape
    spec = pl.BlockSpec((1, n_e, nc, ln), lambda bi: (bi, 0, 0, 0))
    return pl.pallas_call(
        functools.partial(_route_kernel, cap=cap),
        grid=(b,),
        in_specs=[spec],
        out_specs=(spec, spec),
        out_shape=(jax.ShapeDtypeStruct(aff_r.shape, i32), jax.ShapeDtypeStruct(aff_r.shape, i32)),
        compiler_params=_cparams(("parallel",), V7X_VMEM_LIMIT_BYTES),
        name="route",
    )(aff_r)


def _window_start(lo, width, cap):
    aligned = lax.shift_left(lax.shift_right_logical(lo, SLOT_ALIGN_LOG2), SLOT_ALIGN_LOG2)
    return pl.multiple_of(jnp.minimum(aligned, cap - width), 1 << SLOT_ALIGN_LOG2)


def _window_widths(cap):
    return min(FAST_SLOTS, cap), min(TOK_CHUNK + (1 << SLOT_ALIGN_LOG2), cap)


def _gather_kernel(offs_ref, h_ref, posm_ref, xe_ref, *, n_off, n_chunks):
    b, e = pl.program_id(0), pl.program_id(1)
    base = (b * pl.num_programs(1) + e) * n_off
    step = TOK_CHUNK // V7X_LANES
    cap = xe_ref.shape[2]
    fast_w, slow_w = _window_widths(cap)
    xe_ref[...] = jnp.zeros_like(xe_ref)

    def misfit(c, bad):
        lo = offs_ref[base + c * step]
        hi = offs_ref[base + (c + 1) * step]
        return bad + (hi - _window_start(lo, fast_w, cap) > fast_w).astype(i32)

    bad = lax.fori_loop(0, n_chunks, misfit, jnp.int32(0))

    def run(width):
        slot_iota = lax.broadcasted_iota(i32, (width, TOK_CHUNK), 0)

        def chunk(c, carry):
            w = _window_start(offs_ref[base + c * step], width, cap)
            tok0 = pl.multiple_of(c * TOK_CHUNK, TOK_CHUNK)
            pr = posm_ref[0, 0, c]
            onehot = jnp.where(pr == slot_iota + w, 1.0, 0.0).astype(bf16)
            rows = _dot(onehot, h_ref[0, pl.ds(tok0, TOK_CHUNK), :]).astype(xe_ref.dtype)
            xe_ref[0, 0, pl.ds(w, width), :] = xe_ref[0, 0, pl.ds(w, width), :] + rows
            return carry

        lax.fori_loop(0, n_chunks, chunk, 0, unroll=2)

    @pl.when(bad == 0)
    def _():
        run(fast_w)

    @pl.when(bad != 0)
    def _():
        run(slow_w)


def _gather(offs, h2, posm_c, *, cap):
    b, s, d = h2.shape
    n_e = posm_c.shape[1]
    n_chunks = s // TOK_CHUNK
    n_off = s // V7X_LANES + 1
    return pl.pallas_call(
        functools.partial(_gather_kernel, n_off=n_off, n_chunks=n_chunks),
        grid_spec=pltpu.PrefetchScalarGridSpec(
            num_scalar_prefetch=1,
            grid=(b, n_e),
            in_specs=[
                pl.BlockSpec((1, s, d), lambda bi, e, offs: (bi, 0, 0), pipeline_mode=pl.Buffered(1)),
                pl.BlockSpec((1, 1, n_chunks, 1, TOK_CHUNK), lambda bi, e, offs: (bi, e, 0, 0, 0)),
            ],
            out_specs=pl.BlockSpec((1, 1, cap, d), lambda bi, e, offs: (bi, e, 0, 0)),
        ),
        out_shape=jax.ShapeDtypeStruct((b, n_e, cap, d), bf16),
        compiler_params=_cparams(("arbitrary", "arbitrary"), V7X_VMEM_LIMIT_BYTES),
        name="gather",
    )(offs, h2, posm_c)


def _ffn_kernel(x_ref, wg_ref, wu_ref, wd_ref, o_ref, acc_ref):
    @pl.when(pl.program_id(2) == 0)
    def _():
        acc_ref[...] = jnp.zeros_like(acc_ref)

    x = x_ref[0, 0]
    a = _dot(x, wg_ref[0, 0].astype(bf16))
    u = _dot(x, wu_ref[0, 0].astype(bf16))
    hmid = (a * (1.0 / (1.0 + jnp.exp(-a))) * u).astype(bf16)
    total = acc_ref[...] + _dot(hmid, wd_ref[0, 0].astype(bf16))
    acc_ref[...] = total
    o_ref[0, 0] = total.astype(o_ref.dtype)


def _ffn(xe, w_gate, w_up, w_down, layer):
    b, n_e, cap, d = xe.shape
    d_ff = w_gate.shape[-1]
    n_f = d_ff // FF_TILE
    return pl.pallas_call(
        _ffn_kernel,
        grid=(n_e, b, n_f),
        in_specs=[
            pl.BlockSpec((1, 1, cap, d), lambda e, bi, f: (bi, e, 0, 0)),
            pl.BlockSpec((1, 1, d, FF_TILE), lambda e, bi, f: (layer, e, 0, f)),
            pl.BlockSpec((1, 1, d, FF_TILE), lambda e, bi, f: (layer, e, 0, f)),
            pl.BlockSpec((1, 1, FF_TILE, d), lambda e, bi, f: (layer, e, f, 0)),
        ],
        out_specs=pl.BlockSpec((1, 1, cap, d), lambda e, bi, f: (bi, e, 0, 0)),
        out_shape=jax.ShapeDtypeStruct((b, n_e, cap, d), bf16),
        scratch_shapes=[pltpu.VMEM((cap, d), f32)],
        compiler_params=_cparams(("parallel", "parallel", "arbitrary"), V7X_VMEM_LIMIT_BYTES),
        name="expert_ffn",
    )(xe, w_gate, w_up, w_down)


def _combine_kernel(offs_ref, x_ref, aff_ref, posm_ref, gt_ref, ye_ref, o_ref, *, n_off):
    b, i = pl.program_id(0), pl.program_id(2)
    n_e, cap = ye_ref.shape[1], ye_ref.shape[2]
    fast_w, slow_w = _window_widths(cap)
    step = TOK_CHUNK // V7X_LANES
    posm = posm_ref[0]
    aff = aff_ref[0]
    los = [offs_ref[(b * n_e + e) * n_off + i * step] for e in range(n_e)]
    his = [offs_ref[(b * n_e + e) * n_off + (i + 1) * step] for e in range(n_e)]
    bad = functools.reduce(
        lambda a, c: a + c, [(his[e] - _window_start(los[e], fast_w, cap) > fast_w).astype(i32) for e in range(n_e)])

    def run(width):
        slot_iota = lax.broadcasted_iota(i32, (TOK_CHUNK, width), 1)
        total = jnp.zeros((TOK_CHUNK, o_ref.shape[2]), f32)
        for e in range(n_e):
            w = _window_start(los[e], width, cap)
            onehot = jnp.where(posm[:, e:e + 1] == slot_iota + w, 1.0, 0.0).astype(bf16)
            total = total + aff[:, e:e + 1] * _dot(onehot, ye_ref[0, e, pl.ds(w, width), :])
        o_ref[0] = x_ref[0] + gt_ref[0] * total

    @pl.when(bad == 0)
    def _():
        run(fast_w)

    @pl.when(bad != 0)
    def _():
        run(slow_w)


def _combine(offs, x1, aff, posm_t, gt2, ye):
    b, s, d = x1.shape
    n_e, cap = ye.shape[1], ye.shape[2]
    n_off = s // V7X_LANES + 1
    return pl.pallas_call(
        functools.partial(_combine_kernel, n_off=n_off),
        grid_spec=pltpu.PrefetchScalarGridSpec(
            num_scalar_prefetch=1,
            grid=(b, d // COL_TILE, s // TOK_CHUNK),
            in_specs=[
                pl.BlockSpec((1, TOK_CHUNK, COL_TILE), lambda bi, j, i, offs: (bi, i, j)),
                pl.BlockSpec((1, TOK_CHUNK, n_e), lambda bi, j, i, offs: (bi, i, 0)),
                pl.BlockSpec((1, TOK_CHUNK, n_e), lambda bi, j, i, offs: (bi, i, 0)),
                pl.BlockSpec((1, 1, COL_TILE), lambda bi, j, i, offs: (bi, 0, j)),
                pl.BlockSpec((1, n_e, cap, COL_TILE), lambda bi, j, i, offs: (bi, 0, 0, j),
                             pipeline_mode=pl.Buffered(1)),
            ],
            out_specs=pl.BlockSpec((1, TOK_CHUNK, COL_TILE), lambda bi, j, i, offs: (bi, i, j)),
        ),
        out_shape=jax.ShapeDtypeStruct((b, s, d), f32),
        compiler_params=_cparams(("arbitrary", "arbitrary", "arbitrary"), V7X_VMEM_LIMIT_BYTES),
        name="combine",
    )(offs, x1, aff, posm_t, gt2, ye)


def _final_kernel(x_ref, g_ref, o_ref):
    x = x_ref[0]
    o_ref[0] = x * lax.rsqrt(jnp.mean(x * x, axis=-1, keepdims=True) + EPS) * g_ref[...]


def _final_norm(x, g, *, tt):
    b, s, d = x.shape
    return pl.pallas_call(
        _final_kernel,
        grid=(b, s // tt),
        in_specs=[pl.BlockSpec((1, tt, d), lambda bi, i: (bi, i, 0)), pl.BlockSpec((1, d), lambda bi, i: (0, 0))],
        out_specs=pl.BlockSpec((1, tt, d), lambda bi, i: (bi, i, 0)),
        out_shape=jax.ShapeDtypeStruct((b, s, d), f32),
        compiler_params=_cparams(("parallel", "parallel")),
        name="final_norm",
    )(x, g)


def _deinterleave(n):
    return np.concatenate([np.arange(0, n, 2), np.arange(1, n, 2)])


def _rope_tables_t(n, d_rot):
    n_rows = n // GRID_W
    row = jnp.repeat(jnp.arange(n_rows, dtype=f32), GRID_W)
    col = jnp.tile(jnp.arange(GRID_W, dtype=f32), n_rows)
    n_freq = d_rot // 4
    inv_freq = ROPE_THETA ** (-jnp.arange(n_freq, dtype=f32) / n_freq)
    ang = jnp.concatenate([row[:, None] * inv_freq, col[:, None] * inv_freq], axis=-1)
    return jnp.cos(ang).T, jnp.sin(ang).T


def _rest_columns():
    p64, p32 = _deinterleave(HEAD_DIM), _deinterleave(MLA_ROPE_DIM)
    cols = [OFF_GQA_Q + h * HEAD_DIM + p64 for h in range(GQA_HEADS)]
    cols += [OFF_GQA_K + h * HEAD_DIM + p64 for h in range(GQA_KV_HEADS)]
    cols += [np.arange(OFF_GQA_V, IN_DIM - MLA_ROPE_DIM), OFF_MLA_ROPE + p32]
    return np.concatenate(cols)


def _uq_columns():
    p32 = _deinterleave(MLA_ROPE_DIM)
    cols = []
    for h in range(MLA_HEADS):
        cols += [h * MLA_QK_DIM + np.arange(MLA_NOPE_DIM), h * MLA_QK_DIM + MLA_NOPE_DIM + p32]
    return np.concatenate(cols)


def _block_diag(w):
    g, c, _ = w.shape
    out = jnp.zeros((g * c, g * c), w.dtype)
    for i in range(g):
        out = out.at[i * c:(i + 1) * c, i * c:(i + 1) * c].set(w[i])
    return out


def _trunk(x, c, w_mod, b_mod, g_norm1, w_in, pool_w, pool_scale, gqa_q_gain, gqa_k_gain, mla_q_gain, mla_kv_gain,
           mla_w_uq, mla_w_ukv, w_out, g_norm2, w_router, w_gate, w_up, w_down, g_final):
    b, s, d = x.shape
    depth = w_mod.shape[0]
    cap = (EC_CAPACITY * s) // N_EXPERTS
    tt = min(TOK_TILE, s)
    tq = min(ATT_TQ, s)
    tk = min(ATT_TK, s)
    nc = s // V7X_LANES

    mod_rows = 8
    c_pad = jnp.zeros((mod_rows, d), f32).at[:b].set(c)
    mod = _modulation(c_pad, w_mod, b_mod)[:, :b].reshape(depth, b, 6, 1, d)

    cg, sg = _rope_tables_t(s, HEAD_DIM)
    cm, sm = _rope_tables_t(s, MLA_ROPE_DIM)
    p64 = _deinterleave(HEAD_DIM)
    rest_cols, uq_cols = _rest_columns(), _uq_columns()

    for l in range(depth):
        sh1, sc1, gt1, sh2, sc2, gt2 = (mod[l, :, k] for k in range(6))
        wp = w_in[l][:, :POOL_DIM].astype(bf16)
        wr = w_in[l][:, rest_cols].T.astype(bf16)
        wuq = mla_w_uq[l][:, uq_cols].T.astype(bf16)
        wukv = mla_w_ukv[l].T.astype(bf16)
        u, qg, kg, kng, vg, qm, km, knm, vm = _inproj(
            x, g_norm1[l][None], sh1, sc1, wp, wr,
            gqa_q_gain[l][p64][:, None], gqa_k_gain[l][p64][:, None], mla_q_gain[l][:, None], mla_kv_gain[l][:, None],
            wuq, wukv, cg, sg, cm, sm, tt=tt, tk=tk)
        og = _attention(qg, kg, kng, vg, group=GQA_GROUP, tq=min(GQA_TQ, s), n_sub=min(GQA_TQ, s) // tq)
        om = _attention(qm, km, knm, vm, group=1, tq=min(MLA_TQ, s), n_sub=min(MLA_TQ, s) // tq)
        ypool = _pool(u, _block_diag(pool_w[l]).astype(bf16), pool_scale[l][None], tp=tt)
        wo = w_out[l].astype(bf16)
        n_g = GQA_HEADS * HEAD_DIM
        x1, h2, aff = _outproj(x, ypool, og, om, wo[:POOL_DIM], wo[POOL_DIM:POOL_DIM + n_g], wo[POOL_DIM + n_g:],
                               gt1, g_norm2[l][None], sh2, sc2, w_router[l], tt=tt)

        aff_r = aff.transpose(0, 2, 1).reshape(b, N_EXPERTS, nc, V7X_LANES)
        posm, pos = _route(aff_r, cap=cap)
        offs = jnp.concatenate([pos[..., 0], jnp.full((b, N_EXPERTS, 1), cap, i32)], axis=-1).reshape(-1)
        posm_c = posm.reshape(b, N_EXPERTS, s // TOK_CHUNK, 1, TOK_CHUNK)
        posm_t = posm.reshape(b, N_EXPERTS, s).transpose(0, 2, 1)
        xe = _gather(offs, h2, posm_c, cap=cap)
        ye = _ffn(xe, w_gate, w_up, w_down, l)
        x = _combine(offs, x1, aff, posm_t, gt2, ye)
    return _final_norm(x, g_final[None], tt=tt)


def kernel(x, c, w_mod, b_mod, g_norm1, w_in, pool_w, pool_scale, gqa_q_gain, gqa_k_gain, mla_q_gain, mla_kv_gain,
           mla_w_uq, mla_w_ukv, w_out, g_norm2, w_router, w_gate, w_up, w_down, g_final):
    return _trunk(x, c, w_mod, b_mod, g_norm1, w_in, pool_w, pool_scale, gqa_q_gain, gqa_k_gain, mla_q_gain,
                  mla_kv_gain, mla_w_uq, mla_w_ukv, w_out, g_norm2, w_router, w_gate, w_up, w_down, g_final)
```

```python
import functools
import math

import numpy as np
import jax
import jax.numpy as jnp
from jax import lax
from jax.experimental import pallas as pl
from jax.experimental.pallas import tpu as pltpu

f32, bf16, i32 = jnp.float32, jnp.bfloat16, jnp.int32

D_MODEL = 1024
DEPTH = 4
GRID_W = 64
ROPE_THETA = 10000.0
EPS = 1e-6
POOL_DIM = 256
POOL_WINDOWS = (2, 4, 8, 16)
POOL_CH = 64
HEAD_DIM = 64
GQA_HEADS = 6
GQA_KV_HEADS = 2
GQA_GROUP = 3
MLA_HEADS = 6
MLA_NOPE_DIM = 64
MLA_ROPE_DIM = 32
MLA_QK_DIM = 96
MLA_V_DIM = 64
MLA_Q_RANK = 256
MLA_KV_RANK = 256
OFF_GQA_Q = POOL_DIM
OFF_GQA_K = OFF_GQA_Q + GQA_HEADS * HEAD_DIM
OFF_GQA_V = OFF_GQA_K + GQA_KV_HEADS * HEAD_DIM
OFF_MLA_Q = OFF_GQA_V + GQA_KV_HEADS * HEAD_DIM
OFF_MLA_KV = OFF_MLA_Q + MLA_Q_RANK
OFF_MLA_ROPE = OFF_MLA_KV + MLA_KV_RANK
IN_DIM = OFF_MLA_ROPE + MLA_ROPE_DIM
REST_DIM = IN_DIM - POOL_DIM
N_EXPERTS = 16
EC_CAPACITY = 2
D_FF = 2048

R_GQ = 0
R_GK = R_GQ + GQA_HEADS * HEAD_DIM
R_GV = R_GK + GQA_KV_HEADS * HEAD_DIM
R_MQ = R_GV + GQA_KV_HEADS * HEAD_DIM
R_MKV = R_MQ + MLA_Q_RANK
R_MR = R_MKV + MLA_KV_RANK

V7X_LANES = 128
V7X_VMEM_LIMIT_BYTES = 60000 * 1024
V_ROWS = 64
V7X_SUBLANES = 8
K_PAD = 16

TOK_TILE = 512
ATT_TQ = 512
GQA_TQ = 1024
MLA_TQ = 2048
ATT_TK = 512
TOK_CHUNK = 256
FAST_SLOTS = 64
SLOT_ALIGN_LOG2 = 4
GATHER_UNROLL = 8
FF_TILE = 512
COL_TILE = 512
LOG2E = math.log2(math.e)


def _cparams(sem, vmem=None):
    return pltpu.CompilerParams(dimension_semantics=sem, vmem_limit_bytes=vmem)


def _split_bf16(a):
    hi = a.astype(bf16)
    lo = (a - hi.astype(f32)).astype(bf16)
    return hi, lo


def _dot(a, b):
    return jnp.dot(a, b, preferred_element_type=f32)


def _dot3(a, b):
    ah, al = _split_bf16(a)
    bh, bl = _split_bf16(b)
    return _dot(ah, bh) + _dot(ah, bl) + _dot(al, bh)


def _mod_kernel(c_ref, w_ref, b_ref, o_ref):
    c = c_ref[...]
    act = c * (1.0 / (1.0 + jnp.exp(-c)))
    o_ref[0] = _dot3(act, w_ref[0]) + b_ref[0]


def _modulation(c_pad, w_mod, b_mod):
    depth, d, six_d = w_mod.shape
    rows = c_pad.shape[0]
    return pl.pallas_call(
        _mod_kernel,
        grid=(depth, six_d // d),
        in_specs=[
            pl.BlockSpec((rows, d), lambda l, j: (0, 0)),
            pl.BlockSpec((1, d, d), lambda l, j: (l, 0, j)),
            pl.BlockSpec((1, 1, d), lambda l, j: (l, 0, j)),
        ],
        out_specs=pl.BlockSpec((1, rows, d), lambda l, j: (l, 0, j)),
        out_shape=jax.ShapeDtypeStruct((depth, rows, six_d), f32),
        compiler_params=_cparams(("parallel", "parallel")),
        name="modulation",
    )(c_pad, w_mod, b_mod.reshape(depth, 1, six_d))


def _rms_rows(z, gain_col):
    r = lax.rsqrt(jnp.mean(z * z, axis=0, keepdims=True) + EPS)
    return z * r * gain_col


def _rope_rows(z, cos, sin):
    half = z.shape[0] // 2
    x1, x2 = z[:half], z[half:]
    return jnp.concatenate([x1 * cos - x2 * sin, x1 * sin + x2 * cos], axis=0)


def _inproj_kernel(x_ref, g_ref, sh_ref, sc_ref, wp_ref, wr_ref, gq_ref, gk_ref, gmq_ref, gmkv_ref,
                   wuq_ref, wukv_ref, cg_ref, sg_ref, cm_ref, sm_ref,
                   u_ref, qg_ref, kg_ref, kng_ref, vg_ref, qm_ref, km_ref, knm_ref, vm_ref, *, tk):
    x = x_ref[0]
    tt = x.shape[0]
    h = x * lax.rsqrt(jnp.mean(x * x, axis=-1, keepdims=True) + EPS) * g_ref[...]
    h = h * (1.0 + sc_ref[0]) + sh_ref[0]
    hb = h.astype(bf16)
    u_ref[0] = _dot(hb, wp_ref[...])
    zt = lax.dot_general(wr_ref[...], hb, (((1,), (1,)), ((), ())), preferred_element_type=f32)

    cg, sg, cm, sm = cg_ref[...], sg_ref[...], cm_ref[...], sm_ref[...]
    n_sub = tt // tk

    def put_k(ref, norm_ref, head, kt):
        kb = kt.astype(bf16)
        kf = kb.astype(f32)
        norm_ref[0, head] = jnp.sqrt(jnp.sum(kf * kf, axis=0, keepdims=True))
        pad = jnp.where(lax.broadcasted_iota(i32, (K_PAD, tt), 0) == 0, 1.0, 0.0)
        ke = jnp.concatenate([kt, pad], axis=0)
        ref[0, head] = ke.T.astype(bf16)

    def put_v(ref, head, vt):
        ve = vt.astype(bf16)
        for j in range(n_sub):
            ref[0, head, j] = ve[:, j * tk:(j + 1) * tk]

    gq = gq_ref[...] * (HEAD_DIM ** -0.5 * LOG2E)
    gk = gk_ref[...]
    for hd in range(GQA_HEADS):
        q = _rms_rows(zt[R_GQ + hd * HEAD_DIM:R_GQ + (hd + 1) * HEAD_DIM], gq)
        qg_ref[0, hd] = _rope_rows(q, cg, sg).astype(bf16)
    for hk in range(GQA_KV_HEADS):
        k = _rms_rows(zt[R_GK + hk * HEAD_DIM:R_GK + (hk + 1) * HEAD_DIM], gk)
        put_k(kg_ref, kng_ref, hk, _rope_rows(k, cg, sg))
        put_v(vg_ref, hk, zt[R_GV + hk * HEAD_DIM:R_GV + (hk + 1) * HEAD_DIM])

    cq = _rms_rows(zt[R_MQ:R_MQ + MLA_Q_RANK], gmq_ref[...]).astype(bf16)
    qm = _dot(wuq_ref[...], cq) * (MLA_QK_DIM ** -0.5 * LOG2E)
    ckv = _rms_rows(zt[R_MKV:R_MKV + MLA_KV_RANK], gmkv_ref[...]).astype(bf16)
    kv = _dot(wukv_ref[...], ckv)
    k_rope = _rope_rows(zt[R_MR:R_MR + MLA_ROPE_DIM], cm, sm)
    for hd in range(MLA_HEADS):
        qh = qm[hd * MLA_QK_DIM:(hd + 1) * MLA_QK_DIM]
        qr = _rope_rows(qh[MLA_NOPE_DIM:], cm, sm)
        qm_ref[0, hd] = jnp.concatenate([qh[:MLA_NOPE_DIM], qr], axis=0).astype(bf16)
        kvh = kv[hd * (MLA_NOPE_DIM + MLA_V_DIM):(hd + 1) * (MLA_NOPE_DIM + MLA_V_DIM)]
        kh = jnp.concatenate([kvh[:MLA_NOPE_DIM], k_rope], axis=0)
        put_k(km_ref, knm_ref, hd, kh)
        put_v(vm_ref, hd, kvh[MLA_NOPE_DIM:])


def _inproj(x, g1, sh1, sc1, wp, wr, gq, gk, gmq, gmkv, wuq, wukv, cg, sg, cm, sm, *, tt, tk):
    b, s, d = x.shape
    n_t = s // tt
    n_sub = tt // tk
    full = lambda shape: pl.BlockSpec(shape, lambda bi, i: (0,) * len(shape))
    vec = pl.BlockSpec((1, 1, d), lambda bi, i: (bi, 0, 0))
    rope_g = pl.BlockSpec((HEAD_DIM // 2, tt), lambda bi, i: (0, i))
    rope_m = pl.BlockSpec((MLA_ROPE_DIM // 2, tt), lambda bi, i: (0, i))
    out_shapes = (
        jax.ShapeDtypeStruct((b, s, POOL_DIM), f32),
        jax.ShapeDtypeStruct((b, GQA_HEADS, HEAD_DIM, s), bf16),
        jax.ShapeDtypeStruct((b, GQA_KV_HEADS, s, HEAD_DIM + K_PAD), bf16),
        jax.ShapeDtypeStruct((b, GQA_KV_HEADS, 1, s), f32),
        jax.ShapeDtypeStruct((b, GQA_KV_HEADS, s // tk, V_ROWS, tk), bf16),
        jax.ShapeDtypeStruct((b, MLA_HEADS, MLA_QK_DIM, s), bf16),
        jax.ShapeDtypeStruct((b, MLA_HEADS, s, MLA_QK_DIM + K_PAD), bf16),
        jax.ShapeDtypeStruct((b, MLA_HEADS, 1, s), f32),
        jax.ShapeDtypeStruct((b, MLA_HEADS, s // tk, V_ROWS, tk), bf16),
    )
    out_specs = (
        pl.BlockSpec((1, tt, POOL_DIM), lambda bi, i: (bi, i, 0)),
        pl.BlockSpec((1, GQA_HEADS, HEAD_DIM, tt), lambda bi, i: (bi, 0, 0, i)),
        pl.BlockSpec((1, GQA_KV_HEADS, tt, HEAD_DIM + K_PAD), lambda bi, i: (bi, 0, i, 0)),
        pl.BlockSpec((1, GQA_KV_HEADS, 1, tt), lambda bi, i: (bi, 0, 0, i)),
        pl.BlockSpec((1, GQA_KV_HEADS, n_sub, V_ROWS, tk), lambda bi, i: (bi, 0, i, 0, 0)),
        pl.BlockSpec((1, MLA_HEADS, MLA_QK_DIM, tt), lambda bi, i: (bi, 0, 0, i)),
        pl.BlockSpec((1, MLA_HEADS, tt, MLA_QK_DIM + K_PAD), lambda bi, i: (bi, 0, i, 0)),
        pl.BlockSpec((1, MLA_HEADS, 1, tt), lambda bi, i: (bi, 0, 0, i)),
        pl.BlockSpec((1, MLA_HEADS, n_sub, V_ROWS, tk), lambda bi, i: (bi, 0, i, 0, 0)),
    )
    return pl.pallas_call(
        functools.partial(_inproj_kernel, tk=tk),
        grid=(b, n_t),
        in_specs=[
            pl.BlockSpec((1, tt, d), lambda bi, i: (bi, i, 0)),
            full((1, d)), vec, vec,
            full(wp.shape), full(wr.shape), full(gq.shape), full(gk.shape), full(gmq.shape), full(gmkv.shape),
            full(wuq.shape), full(wukv.shape), rope_g, rope_g, rope_m, rope_m,
        ],
        out_specs=out_specs,
        out_shape=out_shapes,
        compiler_params=_cparams(("parallel", "parallel"), V7X_VMEM_LIMIT_BYTES),
        name="inproj",
    )(x, g1, sh1, sc1, wp, wr, gq, gk, gmq, gmkv, wuq, wukv, cg, sg, cm, sm)


SAFE_LOGIT_BOUND = 50.0


def _attn_kernel(q_ref, k_ref, kn_ref, v_ref, o_ref, s0_ref, m_ref, l_ref, acc_ref, *, tk, n_chunks, dv, n_sub):
    group, tq = q_ref.shape[1], q_ref.shape[3]
    ts = tq // n_sub
    streams = [(h, j) for h in range(group) for j in range(n_sub)]
    qts = [q_ref[0, h, :, j * ts:(j + 1) * ts] for h, j in streams]

    def k_chunk(c):
        return k_ref[0, 0, pl.ds(pl.multiple_of(c * tk, tk), tk), :]

    def finish():
        for n, (h, j) in enumerate(streams):
            l = jnp.sum(l_ref[n], axis=0, keepdims=True)
            o_ref[0, h * dv:(h + 1) * dv, j * ts:(j + 1) * ts] = (acc_ref[n] / l).astype(o_ref.dtype)

    def row_sums(p):
        return jnp.sum(p.reshape(tk // V7X_SUBLANES, V7X_SUBLANES, ts), axis=0)

    k_max = jnp.max(kn_ref[0, 0], axis=-1, keepdims=True)
    bounds = []
    for qt in qts:
        qf = qt.astype(f32)
        bounds.append(jnp.sqrt(jnp.sum(qf * qf, axis=0, keepdims=True)) * k_max)
    worst = functools.reduce(jnp.maximum, [jnp.max(u, axis=-1, keepdims=True) for u in bounds])
    safe = worst[0, 0] <= SAFE_LOGIT_BOUND

    @pl.when(safe)
    def _():
        row0 = lax.broadcasted_iota(i32, (K_PAD, ts), 0) == 0
        qes = [jnp.concatenate([qt, jnp.where(row0, -c, 0.0).astype(qt.dtype)], axis=0) for qt, c in zip(qts, bounds)]
        acc_ref[...] = jnp.zeros(acc_ref.shape, f32)
        l_ref[...] = jnp.zeros(l_ref.shape, f32)
        s0_ref[...] = _dot(k_chunk(0), qes[0])
        order = [(dc, n) for dc in range(2) for n in range(len(streams))]

        def step(i, carry):
            c = 2 * i
            kcs = [k_chunk(c), k_chunk(c + 1), k_chunk(jnp.minimum(c + 2, n_chunks - 1))]
            vcs = [v_ref[0, 0, c], v_ref[0, 0, c + 1]]
            s_cur = s0_ref[...]
            for idx, (dc, n) in enumerate(order):
                dc2, n2 = order[idx + 1] if idx + 1 < len(order) else (2, 0)
                s_next = _dot(kcs[dc2], qes[n2])
                p = jnp.exp2(s_cur)
                l_ref[n] += row_sums(p)
                acc_ref[n] += _dot(vcs[dc], p.astype(bf16))
                s_cur = s_next
            s0_ref[...] = s_cur
            return carry

        lax.fori_loop(0, n_chunks // 2, step, 0)
        finish()

    @pl.when(jnp.logical_not(safe))
    def _():
        m_ref[...] = jnp.full(m_ref.shape, -jnp.inf, f32)
        acc_ref[...] = jnp.zeros(acc_ref.shape, f32)
        l_ref[...] = jnp.zeros(l_ref.shape, f32)
        qzs = [jnp.concatenate([qt, jnp.zeros((K_PAD, ts), qt.dtype)], axis=0) for qt in qts]

        def step(c, carry):
            kc = k_chunk(c)
            vc = v_ref[0, 0, c]
            for n, qz in enumerate(qzs):
                s = _dot(kc, qz)
                m = m_ref[n]
                m_new = jnp.maximum(m, jnp.max(s, axis=0, keepdims=True))
                p = jnp.exp2(s - m_new)
                alpha = jnp.exp2(m - m_new)
                l_ref[n] = l_ref[n] * alpha + row_sums(p)
                acc_ref[n] = acc_ref[n] * alpha + _dot(vc, p.astype(bf16))
                m_ref[n] = m_new
            return carry

        lax.fori_loop(0, n_chunks, step, 0)
        finish()


def _attention(qt, k, kn, vt, *, group, tq, n_sub):
    b, hq, dq, s = qt.shape
    _, hk, n_chunks, v_rows, tk = vt.shape
    dv = MLA_V_DIM
    assert n_chunks % 2 == 0 and tq % n_sub == 0
    n_streams, ts = group * n_sub, tq // n_sub
    return pl.pallas_call(
        functools.partial(_attn_kernel, tk=tk, n_chunks=n_chunks, dv=dv, n_sub=n_sub),
        grid=(b, hk, s // tq),
        scratch_shapes=[
            pltpu.VMEM((tk, ts), f32),
            pltpu.VMEM((n_streams, 1, ts), f32),
            pltpu.VMEM((n_streams, V7X_SUBLANES, ts), f32),
            pltpu.VMEM((n_streams, v_rows, ts), f32),
        ],
        in_specs=[
            pl.BlockSpec((1, group, dq, tq), lambda bi, g, i: (bi, g, 0, i)),
            pl.BlockSpec((1, 1, s, dq + K_PAD), lambda bi, g, i: (bi, g, 0, 0)),
            pl.BlockSpec((1, 1, 1, s), lambda bi, g, i: (bi, g, 0, 0)),
            pl.BlockSpec((1, 1, n_chunks, v_rows, tk), lambda bi, g, i: (bi, g, 0, 0, 0)),
        ],
        out_specs=pl.BlockSpec((1, group * dv, tq), lambda bi, g, i: (bi, g, i)),
        out_shape=jax.ShapeDtypeStruct((b, hq * dv, s), bf16),
        compiler_params=_cparams(("parallel", "parallel", "parallel"), V7X_VMEM_LIMIT_BYTES),
        name="attention",
    )(qt, k, kn, vt)


POOL_HALO = 16


def _pool_kernel(up_ref, uc_ref, un_ref, w_ref, sc_ref, o_ref, *, seq):
    i = pl.program_id(1)
    n_t = pl.num_programs(1)
    cur = uc_ref[0]
    tp = cur.shape[0]
    prev = jnp.where(i > 0, up_ref[0], 0.0)
    nxt = jnp.where(i < n_t - 1, un_ref[0], 0.0)
    ext = jnp.concatenate([prev, cur, nxt], axis=0)
    t = i * tp + lax.broadcasted_iota(i32, (tp, 1), 0)
    parts = []
    for gi, win in enumerate(POOL_WINDOWS):
        lo = win // 2
        hi = win - 1 - lo
        cols = slice(gi * POOL_CH, (gi + 1) * POOL_CH)
        eg = ext[:, cols]
        wsum = eg[POOL_HALO - lo:POOL_HALO - lo + tp]
        for j in range(1, win):
            wsum = wsum + eg[POOL_HALO - lo + j:POOL_HALO - lo + j + tp]
        cnt = (jnp.minimum(t + hi, seq - 1) - jnp.maximum(t - lo, 0) + 1).astype(f32)
        parts.append(wsum / cnt - cur[:, cols])
    p = jnp.concatenate(parts, axis=1).astype(bf16)
    o_ref[0] = (_dot(p, w_ref[...]) * sc_ref[...]).astype(o_ref.dtype)


def _pool(u, w_bd, scale, *, tp):
    b, s, c = u.shape
    n_t = s // tp
    r = tp // POOL_HALO
    return pl.pallas_call(
        functools.partial(_pool_kernel, seq=s),
        grid=(b, n_t),
        in_specs=[
            pl.BlockSpec((1, POOL_HALO, c), lambda bi, i: (bi, jnp.maximum(i * r - 1, 0), 0)),
            pl.BlockSpec((1, tp, c), lambda bi, i: (bi, i, 0)),
            pl.BlockSpec((1, POOL_HALO, c), lambda bi, i: (bi, jnp.minimum((i + 1) * r, s // POOL_HALO - 1), 0)),
            pl.BlockSpec(w_bd.shape, lambda bi, i: (0, 0)),
            pl.BlockSpec((1, c), lambda bi, i: (0, 0)),
        ],
        out_specs=pl.BlockSpec((1, tp, c), lambda bi, i: (bi, i, 0)),
        out_shape=jax.ShapeDtypeStruct((b, s, c), bf16),
        compiler_params=_cparams(("parallel", "parallel")),
        name="pool",
    )(u, u, u, w_bd, scale)


def _outproj_kernel(x_ref, yp_ref, og_ref, om_ref, wop_ref, wog_ref, wom_ref, gt_ref, g2_ref, sh_ref, sc_ref,
                    wrt_ref, x1_ref, h2_ref, aff_ref):
    tn = (((0,), (0,)), ((), ()))
    y = _dot(yp_ref[0], wop_ref[...])
    y = y + lax.dot_general(og_ref[0], wog_ref[...], tn, preferred_element_type=f32)
    y = y + lax.dot_general(om_ref[0], wom_ref[...], tn, preferred_element_type=f32)
    x1 = x_ref[0] + gt_ref[0] * y
    x1_ref[0] = x1
    h = x1 * lax.rsqrt(jnp.mean(x1 * x1, axis=-1, keepdims=True) + EPS) * g2_ref[...]
    h = h * (1.0 + sc_ref[0]) + sh_ref[0]
    h2_ref[0] = h.astype(bf16)
    logits = _dot3(h, wrt_ref[...])
    ex = jnp.exp(logits - jnp.max(logits, axis=-1, keepdims=True))
    aff_ref[0] = ex / jnp.sum(ex, axis=-1, keepdims=True)


def _outproj(x, ypool, og, om, wop, wog, wom, gt1, g2, sh2, sc2, w_router, *, tt):
    b, s, d = x.shape
    full = lambda shape: pl.BlockSpec(shape, lambda bi, i: (0,) * len(shape))
    vec = pl.BlockSpec((1, 1, d), lambda bi, i: (bi, 0, 0))
    n_e = w_router.shape[1]
    return pl.pallas_call(
        _outproj_kernel,
        grid=(b, s // tt),
        in_specs=[
            pl.BlockSpec((1, tt, d), lambda bi, i: (bi, i, 0)),
            pl.BlockSpec((1, tt, POOL_DIM), lambda bi, i: (bi, i, 0)),
            pl.BlockSpec((1, og.shape[1], tt), lambda bi, i: (bi, 0, i)),
            pl.BlockSpec((1, om.shape[1], tt), lambda bi, i: (bi, 0, i)),
            full(wop.shape), full(wog.shape), full(wom.shape),
            vec, full((1, d)), vec, vec, full(w_router.shape),
        ],
        out_specs=(
            pl.BlockSpec((1, tt, d), lambda bi, i: (bi, i, 0)),
            pl.BlockSpec((1, tt, d), lambda bi, i: (bi, i, 0)),
            pl.BlockSpec((1, tt, n_e), lambda bi, i: (bi, i, 0)),
        ),
        out_shape=(
            jax.ShapeDtypeStruct((b, s, d), f32),
            jax.ShapeDtypeStruct((b, s, d), bf16),
            jax.ShapeDtypeStruct((b, s, n_e), f32),
        ),
        compiler_params=_cparams(("parallel", "parallel"), V7X_VMEM_LIMIT_BYTES),
        name="outproj",
    )(x, ypool, og, om, wop, wog, wom, gt1, g2, sh2, sc2, w_router)


def _route_kernel(a_ref, posm_ref, pos_ref, *, cap):
    a = a_ref[0]
    n_e, nc, ln = a.shape
    bits = pltpu.bitcast(a, i32)

    def count(mask):
        c = jnp.sum(jnp.where(mask, 1.0, 0.0), axis=2, keepdims=True)
        return jnp.sum(c, axis=1, keepdims=True)

    thr = jnp.zeros((n_e, 1, 1), i32)
    for bit in range(30, -1, -1):
        cand = thr | (1 << bit)
        thr = jnp.where(count(bits >= cand) >= cap, cand, thr)
    gt = bits > thr
    eq = bits == thr
    need = cap - count(gt)

    r_i = lax.broadcasted_iota(i32, (ln, ln), 0)
    c_i = lax.broadcasted_iota(i32, (ln, ln), 1)
    tri_incl = jnp.where(r_i <= c_i, 1.0, 0.0).astype(bf16)
    r_c = lax.broadcasted_iota(i32, (nc, nc), 0)
    c_c = lax.broadcasted_iota(i32, (nc, nc), 1)
    tri_strict = jnp.where(c_c < r_c, 1.0, 0.0).astype(bf16)

    def excl_prefix(mask):
        x = jnp.where(mask, 1.0, 0.0)
        incl = _dot(x.astype(bf16).reshape(n_e * nc, ln), tri_incl).reshape(n_e, nc, ln)
        tot = jnp.broadcast_to(incl[:, :, ln - 1:ln], (n_e, nc, ln))
        tot_hi = tot.astype(bf16)
        offs = [_dot(tri_strict, tot_hi[e]) for e in range(n_e)]
        return jnp.stack(offs, axis=0) + incl - x

    sel = gt | (eq & (excl_prefix(eq) < need))
    pos = excl_prefix(sel).astype(i32)
    pos_ref[0] = pos
    posm_ref[0] = jnp.where(sel, pos, -1)


def _route(aff_r, *, cap):
    b, n_e, nc, ln = aff_r.shape
    spec = pl.BlockSpec((1, n_e, nc, ln), lambda bi: (bi, 0, 0, 0))
    return pl.pallas_call(
        functools.partial(_route_kernel, cap=cap),
        grid=(b,),
        in_specs=[spec],
        out_specs=(spec, spec),
        out_shape=(jax.ShapeDtypeStruct(aff_r.shape, i32), jax.ShapeDtypeStruct(aff_r.shape, i32)),
        compiler_params=_cparams(("parallel",), V7X_VMEM_LIMIT_BYTES),
        name="route",
    )(aff_r)


def _window_start(lo, width, cap):
    aligned = lax.shift_left(lax.shift_right_logical(lo, SLOT_ALIGN_LOG2), SLOT_ALIGN_LOG2)
    return pl.multiple_of(jnp.minimum(aligned, cap - width), 1 << SLOT_ALIGN_LOG2)


def _window_widths(cap):
    return min(FAST_SLOTS, cap), min(TOK_CHUNK + (1 << SLOT_ALIGN_LOG2), cap)


def _gather_kernel(offs_ref, h_ref, posm_ref, aff_ref, xe_ref, gate_ref, *, n_off, n_chunks):
    b, e = pl.program_id(0), pl.program_id(1)
    base = (b * pl.num_programs(1) + e) * n_off
    step = TOK_CHUNK // V7X_LANES
    cap = xe_ref.shape[2]
    fast_w, slow_w = _window_widths(cap)
    xe_ref[...] = jnp.zeros_like(xe_ref)
    gate_ref[...] = jnp.zeros_like(gate_ref)

    def misfit(c, bad):
        lo = offs_ref[base + c * step]
        hi = offs_ref[base + (c + 1) * step]
        return bad + (hi - _window_start(lo, fast_w, cap) > fast_w).astype(i32)

    bad = lax.fori_loop(0, n_chunks, misfit, jnp.int32(0))

    def run(width):
        slot_iota = lax.broadcasted_iota(i32, (width, TOK_CHUNK), 0)

        def chunk(c, carry):
            w = _window_start(offs_ref[base + c * step], width, cap)
            tok0 = pl.multiple_of(c * TOK_CHUNK, TOK_CHUNK)
            pr = posm_ref[0, 0, c]
            hit = pr == slot_iota + w
            rows = _dot(jnp.where(hit, 1.0, 0.0).astype(bf16), h_ref[0, pl.ds(tok0, TOK_CHUNK), :])
            xe_ref[0, 0, pl.ds(w, width), :] = xe_ref[0, 0, pl.ds(w, width), :] + rows.astype(xe_ref.dtype)
            gate_ref[0, 0, pl.ds(w, width), :] += jnp.sum(jnp.where(hit, aff_ref[0, 0, c], 0.0), axis=1, keepdims=True)
            return carry

        lax.fori_loop(0, n_chunks, chunk, 0, unroll=GATHER_UNROLL)

    @pl.when(bad == 0)
    def _():
        run(fast_w)

    @pl.when(bad != 0)
    def _():
        run(slow_w)


def _gather(offs, h2, posm_c, aff_c, *, cap):
    b, s, d = h2.shape
    n_e = posm_c.shape[1]
    n_chunks = s // TOK_CHUNK
    n_off = s // V7X_LANES + 1
    chunked = pl.BlockSpec((1, 1, n_chunks, 1, TOK_CHUNK), lambda bi, e, offs: (bi, e, 0, 0, 0))
    return pl.pallas_call(
        functools.partial(_gather_kernel, n_off=n_off, n_chunks=n_chunks),
        grid_spec=pltpu.PrefetchScalarGridSpec(
            num_scalar_prefetch=1,
            grid=(b, n_e),
            in_specs=[
                pl.BlockSpec((1, s, d), lambda bi, e, offs: (bi, 0, 0), pipeline_mode=pl.Buffered(1)),
                chunked, chunked,
            ],
            out_specs=(pl.BlockSpec((1, 1, cap, d), lambda bi, e, offs: (bi, e, 0, 0)),
                       pl.BlockSpec((1, 1, cap, 1), lambda bi, e, offs: (bi, e, 0, 0))),
        ),
        out_shape=(jax.ShapeDtypeStruct((b, n_e, cap, d), bf16), jax.ShapeDtypeStruct((b, n_e, cap, 1), f32)),
        compiler_params=_cparams(("arbitrary", "arbitrary"), V7X_VMEM_LIMIT_BYTES),
        name="gather",
    )(offs, h2, posm_c, aff_c)


def _ffn_kernel(x_ref, gate_ref, wg_ref, wu_ref, wd_ref, o_ref, acc_ref):
    @pl.when(pl.program_id(2) == 0)
    def _():
        acc_ref[...] = jnp.zeros_like(acc_ref)

    x = x_ref[0, 0]
    a = _dot(x, wg_ref[0, 0].astype(bf16))
    u = _dot(x, wu_ref[0, 0].astype(bf16))
    hmid = (a * (1.0 / (1.0 + jnp.exp(-a))) * u).astype(bf16)
    total = acc_ref[...] + _dot(hmid, wd_ref[0, 0].astype(bf16))
    acc_ref[...] = total
    o_ref[0, 0] = (total * gate_ref[0, 0]).astype(o_ref.dtype)


def _ffn(xe, gate, w_gate, w_up, w_down, layer):
    b, n_e, cap, d = xe.shape
    d_ff = w_gate.shape[-1]
    n_f = d_ff // FF_TILE
    return pl.pallas_call(
        _ffn_kernel,
        grid=(n_e, b, n_f),
        in_specs=[
            pl.BlockSpec((1, 1, cap, d), lambda e, bi, f: (bi, e, 0, 0)),
            pl.BlockSpec((1, 1, cap, 1), lambda e, bi, f: (bi, e, 0, 0)),
            pl.BlockSpec((1, 1, d, FF_TILE), lambda e, bi, f: (layer, e, 0, f)),
            pl.BlockSpec((1, 1, d, FF_TILE), lambda e, bi, f: (layer, e, 0, f)),
            pl.BlockSpec((1, 1, FF_TILE, d), lambda e, bi, f: (layer, e, f, 0)),
        ],
        out_specs=pl.BlockSpec((1, 1, cap, d), lambda e, bi, f: (bi, e, 0, 0)),
        out_shape=jax.ShapeDtypeStruct((b, n_e, cap, d), bf16),
        scratch_shapes=[pltpu.VMEM((cap, d), f32)],
        compiler_params=_cparams(("parallel", "parallel", "arbitrary"), V7X_VMEM_LIMIT_BYTES),
        name="expert_ffn",
    )(xe, gate, w_gate, w_up, w_down)


def _combine_kernel(offs_ref, x_ref, posm_ref, gt_ref, ye_ref, o_ref, *, n_off):
    b, i = pl.program_id(0), pl.program_id(2)
    n_e, cap = ye_ref.shape[1], ye_ref.shape[2]
    fast_w, slow_w = _window_widths(cap)
    step = TOK_CHUNK // V7X_LANES
    posm = posm_ref[0]
    los = [offs_ref[(b * n_e + e) * n_off + i * step] for e in range(n_e)]
    his = [offs_ref[(b * n_e + e) * n_off + (i + 1) * step] for e in range(n_e)]
    bad = functools.reduce(
        lambda a, c: a + c, [(his[e] - _window_start(los[e], fast_w, cap) > fast_w).astype(i32) for e in range(n_e)])
    paired = 2 * fast_w == V7X_LANES and n_e % 2 == 0

    def run_stacked():
        lane = lax.broadcasted_iota(i32, (TOK_CHUNK, V7X_LANES), 1)
        ws = [_window_start(los[e], fast_w, cap) for e in range(n_e)]
        tiles, rows = [], []
        for e in range(0, n_e, 2):
            target = jnp.where(lane < fast_w, posm[:, e:e + 1] - ws[e], posm[:, e + 1:e + 2] - ws[e + 1] + fast_w)
            tiles.append(jnp.where(target == lane, 1.0, 0.0).astype(bf16))
            rows += [ye_ref[0, e, pl.ds(ws[e], fast_w), :], ye_ref[0, e + 1, pl.ds(ws[e + 1], fast_w), :]]
        total = _dot(jnp.concatenate(tiles, axis=1), jnp.concatenate(rows, axis=0))
        o_ref[0] = x_ref[0] + gt_ref[0] * total

    def run_per_expert(width):
        slot_iota = lax.broadcasted_iota(i32, (TOK_CHUNK, width), 1)
        total = jnp.zeros((TOK_CHUNK, o_ref.shape[2]), f32)
        for e in range(n_e):
            w = _window_start(los[e], width, cap)
            onehot = jnp.where(posm[:, e:e + 1] == slot_iota + w, 1.0, 0.0).astype(bf16)
            total = total + _dot(onehot, ye_ref[0, e, pl.ds(w, width), :])
        o_ref[0] = x_ref[0] + gt_ref[0] * total

    @pl.when(bad == 0)
    def _():
        run_stacked() if paired else run_per_expert(fast_w)

    @pl.when(bad != 0)
    def _():
        run_per_expert(slow_w)


def _combine(offs, x1, posm_t, gt2, ye):
    b, s, d = x1.shape
    n_e, cap = ye.shape[1], ye.shape[2]
    n_off = s // V7X_LANES + 1
    return pl.pallas_call(
        functools.partial(_combine_kernel, n_off=n_off),
        grid_spec=pltpu.PrefetchScalarGridSpec(
            num_scalar_prefetch=1,
            grid=(b, d // COL_TILE, s // TOK_CHUNK),
            in_specs=[
                pl.BlockSpec((1, TOK_CHUNK, COL_TILE), lambda bi, j, i, offs: (bi, i, j)),
                pl.BlockSpec((1, TOK_CHUNK, n_e), lambda bi, j, i, offs: (bi, i, 0)),
                pl.BlockSpec((1, 1, COL_TILE), lambda bi, j, i, offs: (bi, 0, j)),
                pl.BlockSpec((1, n_e, cap, COL_TILE), lambda bi, j, i, offs: (bi, 0, 0, j),
                             pipeline_mode=pl.Buffered(1)),
            ],
            out_specs=pl.BlockSpec((1, TOK_CHUNK, COL_TILE), lambda bi, j, i, offs: (bi, i, j)),
        ),
        out_shape=jax.ShapeDtypeStruct((b, s, d), f32),
        compiler_params=_cparams(("arbitrary", "arbitrary", "arbitrary"), V7X_VMEM_LIMIT_BYTES),
        name="combine",
    )(offs, x1, posm_t, gt2, ye)


def _final_kernel(x_ref, g_ref, o_ref):
    x = x_ref[0]
    o_ref[0] = x * lax.rsqrt(jnp.mean(x * x, axis=-1, keepdims=True) + EPS) * g_ref[...]


def _final_norm(x, g, *, tt):
    b, s, d = x.shape
    return pl.pallas_call(
        _final_kernel,
        grid=(b, s // tt),
        in_specs=[pl.BlockSpec((1, tt, d), lambda bi, i: (bi, i, 0)), pl.BlockSpec((1, d), lambda bi, i: (0, 0))],
        out_specs=pl.BlockSpec((1, tt, d), lambda bi, i: (bi, i, 0)),
        out_shape=jax.ShapeDtypeStruct((b, s, d), f32),
        compiler_params=_cparams(("parallel", "parallel")),
        name="final_norm",
    )(x, g)


def _deinterleave(n):
    return np.concatenate([np.arange(0, n, 2), np.arange(1, n, 2)])


def _rope_tables_t(n, d_rot):
    n_rows = n // GRID_W
    row = jnp.repeat(jnp.arange(n_rows, dtype=f32), GRID_W)
    col = jnp.tile(jnp.arange(GRID_W, dtype=f32), n_rows)
    n_freq = d_rot // 4
    inv_freq = ROPE_THETA ** (-jnp.arange(n_freq, dtype=f32) / n_freq)
    ang = jnp.concatenate([row[:, None] * inv_freq, col[:, None] * inv_freq], axis=-1)
    return jnp.cos(ang).T, jnp.sin(ang).T


def _rest_columns():
    p64, p32 = _deinterleave(HEAD_DIM), _deinterleave(MLA_ROPE_DIM)
    cols = [OFF_GQA_Q + h * HEAD_DIM + p64 for h in range(GQA_HEADS)]
    cols += [OFF_GQA_K + h * HEAD_DIM + p64 for h in range(GQA_KV_HEADS)]
    cols += [np.arange(OFF_GQA_V, IN_DIM - MLA_ROPE_DIM), OFF_MLA_ROPE + p32]
    return np.concatenate(cols)


def _uq_columns():
    p32 = _deinterleave(MLA_ROPE_DIM)
    cols = []
    for h in range(MLA_HEADS):
        cols += [h * MLA_QK_DIM + np.arange(MLA_NOPE_DIM), h * MLA_QK_DIM + MLA_NOPE_DIM + p32]
    return np.concatenate(cols)


def _block_diag(w):
    g, c, _ = w.shape
    out = jnp.zeros((g * c, g * c), w.dtype)
    for i in range(g):
        out = out.at[i * c:(i + 1) * c, i * c:(i + 1) * c].set(w[i])
    return out


def _trunk(x, c, w_mod, b_mod, g_norm1, w_in, pool_w, pool_scale, gqa_q_gain, gqa_k_gain, mla_q_gain, mla_kv_gain,
           mla_w_uq, mla_w_ukv, w_out, g_norm2, w_router, w_gate, w_up, w_down, g_final):
    b, s, d = x.shape
    depth = w_mod.shape[0]
    cap = (EC_CAPACITY * s) // N_EXPERTS
    tt = min(TOK_TILE, s)
    tq = min(ATT_TQ, s)
    tk = min(ATT_TK, s)
    nc = s // V7X_LANES

    mod_rows = 8
    c_pad = jnp.zeros((mod_rows, d), f32).at[:b].set(c)
    mod = _modulation(c_pad, w_mod, b_mod)[:, :b].reshape(depth, b, 6, 1, d)

    cg, sg = _rope_tables_t(s, HEAD_DIM)
    cm, sm = _rope_tables_t(s, MLA_ROPE_DIM)
    p64 = _deinterleave(HEAD_DIM)
    rest_cols, uq_cols = _rest_columns(), _uq_columns()

    for l in range(depth):
        sh1, sc1, gt1, sh2, sc2, gt2 = (mod[l, :, k] for k in range(6))
        wp = w_in[l][:, :POOL_DIM].astype(bf16)
        wr = w_in[l][:, rest_cols].T.astype(bf16)
        wuq = mla_w_uq[l][:, uq_cols].T.astype(bf16)
        wukv = mla_w_ukv[l].T.astype(bf16)
        u, qg, kg, kng, vg, qm, km, knm, vm = _inproj(
            x, g_norm1[l][None], sh1, sc1, wp, wr,
            gqa_q_gain[l][p64][:, None], gqa_k_gain[l][p64][:, None], mla_q_gain[l][:, None], mla_kv_gain[l][:, None],
            wuq, wukv, cg, sg, cm, sm, tt=tt, tk=tk)
        og = _attention(qg, kg, kng, vg, group=GQA_GROUP, tq=min(GQA_TQ, s), n_sub=min(GQA_TQ, s) // tq)
        om = _attention(qm, km, knm, vm, group=1, tq=min(MLA_TQ, s), n_sub=min(MLA_TQ, s) // tq)
        ypool = _pool(u, _block_diag(pool_w[l]).astype(bf16), pool_scale[l][None], tp=tt)
        wo = w_out[l].astype(bf16)
        n_g = GQA_HEADS * HEAD_DIM
        x1, h2, aff = _outproj(x, ypool, og, om, wo[:POOL_DIM], wo[POOL_DIM:POOL_DIM + n_g], wo[POOL_DIM + n_g:],
                               gt1, g_norm2[l][None], sh2, sc2, w_router[l], tt=tt)

        aff_r = aff.transpose(0, 2, 1).reshape(b, N_EXPERTS, nc, V7X_LANES)
        posm, pos = _route(aff_r, cap=cap)
        offs = jnp.concatenate([pos[..., 0], jnp.full((b, N_EXPERTS, 1), cap, i32)], axis=-1).reshape(-1)
        posm_c = posm.reshape(b, N_EXPERTS, s // TOK_CHUNK, 1, TOK_CHUNK)
        posm_t = posm.reshape(b, N_EXPERTS, s).transpose(0, 2, 1)
        aff_c = aff_r.reshape(b, N_EXPERTS, s // TOK_CHUNK, 1, TOK_CHUNK)
        xe, gate = _gather(offs, h2, posm_c, aff_c, cap=cap)
        ye = _ffn(xe, gate, w_gate, w_up, w_down, l)
        x = _combine(offs, x1, posm_t, gt2, ye)
    return _final_norm(x, g_final[None], tt=tt)


def kernel(x, c, w_mod, b_mod, g_norm1, w_in, pool_w, pool_scale, gqa_q_gain, gqa_k_gain, mla_q_gain, mla_kv_gain,
           mla_w_uq, mla_w_ukv, w_out, g_norm2, w_router, w_gate, w_up, w_down, g_final):
    return _trunk(x, c, w_mod, b_mod, g_norm1, w_in, pool_w, pool_scale, gqa_q_gain, gqa_k_gain, mla_q_gain,
                  mla_kv_gain, mla_w_uq, mla_w_ukv, w_out, g_norm2, w_router, w_gate, w_up, w_down, g_final)
```

```python
import functools
import math

import numpy as np
import jax
import jax.numpy as jnp
from jax import lax
from jax.experimental import pallas as pl
from jax.experimental.pallas import tpu as pltpu

f32, bf16, i32 = jnp.float32, jnp.bfloat16, jnp.int32

D_MODEL = 1024
DEPTH = 4
GRID_W = 64
ROPE_THETA = 10000.0
EPS = 1e-6
POOL_DIM = 256
POOL_WINDOWS = (2, 4, 8, 16)
POOL_CH = 64
HEAD_DIM = 64
GQA_HEADS = 6
GQA_KV_HEADS = 2
GQA_GROUP = 3
MLA_HEADS = 6
MLA_NOPE_DIM = 64
MLA_ROPE_DIM = 32
MLA_QK_DIM = 96
MLA_V_DIM = 64
MLA_Q_RANK = 256
MLA_KV_RANK = 256
OFF_GQA_Q = POOL_DIM
OFF_GQA_K = OFF_GQA_Q + GQA_HEADS * HEAD_DIM
OFF_GQA_V = OFF_GQA_K + GQA_KV_HEADS * HEAD_DIM
OFF_MLA_Q = OFF_GQA_V + GQA_KV_HEADS * HEAD_DIM
OFF_MLA_KV = OFF_MLA_Q + MLA_Q_RANK
OFF_MLA_ROPE = OFF_MLA_KV + MLA_KV_RANK
IN_DIM = OFF_MLA_ROPE + MLA_ROPE_DIM
REST_DIM = IN_DIM - POOL_DIM
N_EXPERTS = 16
EC_CAPACITY = 2
D_FF = 2048

R_GQ = 0
R_GK = R_GQ + GQA_HEADS * HEAD_DIM
R_GV = R_GK + GQA_KV_HEADS * HEAD_DIM
R_MQ = R_GV + GQA_KV_HEADS * HEAD_DIM
R_MKV = R_MQ + MLA_Q_RANK
R_MR = R_MKV + MLA_KV_RANK

V7X_LANES = 128
V7X_VMEM_LIMIT_BYTES = 60000 * 1024
V_ROWS = 64
V7X_SUBLANES = 8
K_PAD = 16

TOK_TILE = 512
ATT_TQ = 512
GQA_TQ = 1024
MLA_TQ = 2048
ATT_TK = 512
TOK_CHUNK = 256
FAST_SLOTS = 64
SLOT_ALIGN_LOG2 = 4
GATHER_UNROLL = 8
CHUNKS_PER_TRIP = 8
FF_TILE = 512
COL_TILE = 512
LOG2E = math.log2(math.e)


def _cparams(sem, vmem=None):
    return pltpu.CompilerParams(dimension_semantics=sem, vmem_limit_bytes=vmem)


def _split_bf16(a):
    hi = a.astype(bf16)
    lo = (a - hi.astype(f32)).astype(bf16)
    return hi, lo


def _dot(a, b):
    return jnp.dot(a, b, preferred_element_type=f32)


def _dot3(a, b):
    ah, al = _split_bf16(a)
    bh, bl = _split_bf16(b)
    return _dot(ah, bh) + _dot(ah, bl) + _dot(al, bh)


def _mod_kernel(c_ref, w_ref, b_ref, o_ref):
    c = c_ref[...]
    act = c * (1.0 / (1.0 + jnp.exp(-c)))
    o_ref[0] = _dot3(act, w_ref[0]) + b_ref[0]


def _modulation(c_pad, w_mod, b_mod):
    depth, d, six_d = w_mod.shape
    rows = c_pad.shape[0]
    return pl.pallas_call(
        _mod_kernel,
        grid=(depth, six_d // d),
        in_specs=[
            pl.BlockSpec((rows, d), lambda l, j: (0, 0)),
            pl.BlockSpec((1, d, d), lambda l, j: (l, 0, j)),
            pl.BlockSpec((1, 1, d), lambda l, j: (l, 0, j)),
        ],
        out_specs=pl.BlockSpec((1, rows, d), lambda l, j: (l, 0, j)),
        out_shape=jax.ShapeDtypeStruct((depth, rows, six_d), f32),
        compiler_params=_cparams(("parallel", "parallel")),
        name="modulation",
    )(c_pad, w_mod, b_mod.reshape(depth, 1, six_d))


def _rms_rows(z, gain_col):
    r = lax.rsqrt(jnp.mean(z * z, axis=0, keepdims=True) + EPS)
    return z * r * gain_col


def _rope_rows(z, cos, sin):
    half = z.shape[0] // 2
    x1, x2 = z[:half], z[half:]
    return jnp.concatenate([x1 * cos - x2 * sin, x1 * sin + x2 * cos], axis=0)


def _inproj_kernel(x_ref, g_ref, sh_ref, sc_ref, wp_ref, wr_ref, gq_ref, gk_ref, gmq_ref, gmkv_ref,
                   wuq_ref, wukv_ref, cg_ref, sg_ref, cm_ref, sm_ref,
                   u_ref, qg_ref, kg_ref, kng_ref, vg_ref, qm_ref, km_ref, knm_ref, vm_ref, *, tk):
    x = x_ref[0]
    tt = x.shape[0]
    h = x * lax.rsqrt(jnp.mean(x * x, axis=-1, keepdims=True) + EPS) * g_ref[...]
    h = h * (1.0 + sc_ref[0]) + sh_ref[0]
    hb = h.astype(bf16)
    u_ref[0] = _dot(hb, wp_ref[...])
    zt = lax.dot_general(wr_ref[...], hb, (((1,), (1,)), ((), ())), preferred_element_type=f32)

    cg, sg, cm, sm = cg_ref[...], sg_ref[...], cm_ref[...], sm_ref[...]
    n_sub = tt // tk

    def put_k(ref, norm_ref, head, kt):
        kb = kt.astype(bf16)
        kf = kb.astype(f32)
        norm_ref[0, head] = jnp.sqrt(jnp.sum(kf * kf, axis=0, keepdims=True))
        pad = jnp.where(lax.broadcasted_iota(i32, (K_PAD, tt), 0) == 0, 1.0, 0.0)
        ke = jnp.concatenate([kt, pad], axis=0)
        ref[0, head] = ke.T.astype(bf16)

    def put_v(ref, head, vt):
        ve = vt.astype(bf16)
        for j in range(n_sub):
            ref[0, head, j] = ve[:, j * tk:(j + 1) * tk]

    gq = gq_ref[...] * (HEAD_DIM ** -0.5 * LOG2E)
    gk = gk_ref[...]
    for hd in range(GQA_HEADS):
        q = _rms_rows(zt[R_GQ + hd * HEAD_DIM:R_GQ + (hd + 1) * HEAD_DIM], gq)
        qg_ref[0, hd] = _rope_rows(q, cg, sg).astype(bf16)
    for hk in range(GQA_KV_HEADS):
        k = _rms_rows(zt[R_GK + hk * HEAD_DIM:R_GK + (hk + 1) * HEAD_DIM], gk)
        put_k(kg_ref, kng_ref, hk, _rope_rows(k, cg, sg))
        put_v(vg_ref, hk, zt[R_GV + hk * HEAD_DIM:R_GV + (hk + 1) * HEAD_DIM])

    cq = _rms_rows(zt[R_MQ:R_MQ + MLA_Q_RANK], gmq_ref[...]).astype(bf16)
    qm = _dot(wuq_ref[...], cq) * (MLA_QK_DIM ** -0.5 * LOG2E)
    ckv = _rms_rows(zt[R_MKV:R_MKV + MLA_KV_RANK], gmkv_ref[...]).astype(bf16)
    kv = _dot(wukv_ref[...], ckv)
    k_rope = _rope_rows(zt[R_MR:R_MR + MLA_ROPE_DIM], cm, sm)
    for hd in range(MLA_HEADS):
        qh = qm[hd * MLA_QK_DIM:(hd + 1) * MLA_QK_DIM]
        qr = _rope_rows(qh[MLA_NOPE_DIM:], cm, sm)
        qm_ref[0, hd] = jnp.concatenate([qh[:MLA_NOPE_DIM], qr], axis=0).astype(bf16)
        kvh = kv[hd * (MLA_NOPE_DIM + MLA_V_DIM):(hd + 1) * (MLA_NOPE_DIM + MLA_V_DIM)]
        kh = jnp.concatenate([kvh[:MLA_NOPE_DIM], k_rope], axis=0)
        put_k(km_ref, knm_ref, hd, kh)
        put_v(vm_ref, hd, kvh[MLA_NOPE_DIM:])


def _inproj(x, g1, sh1, sc1, wp, wr, gq, gk, gmq, gmkv, wuq, wukv, cg, sg, cm, sm, *, tt, tk):
    b, s, d = x.shape
    n_t = s // tt
    n_sub = tt // tk
    full = lambda shape: pl.BlockSpec(shape, lambda bi, i: (0,) * len(shape))
    vec = pl.BlockSpec((1, 1, d), lambda bi, i: (bi, 0, 0))
    rope_g = pl.BlockSpec((HEAD_DIM // 2, tt), lambda bi, i: (0, i))
    rope_m = pl.BlockSpec((MLA_ROPE_DIM // 2, tt), lambda bi, i: (0, i))
    out_shapes = (
        jax.ShapeDtypeStruct((b, s, POOL_DIM), f32),
        jax.ShapeDtypeStruct((b, GQA_HEADS, HEAD_DIM, s), bf16),
        jax.ShapeDtypeStruct((b, GQA_KV_HEADS, s, HEAD_DIM + K_PAD), bf16),
        jax.ShapeDtypeStruct((b, GQA_KV_HEADS, 1, s), f32),
        jax.ShapeDtypeStruct((b, GQA_KV_HEADS, s // tk, V_ROWS, tk), bf16),
        jax.ShapeDtypeStruct((b, MLA_HEADS, MLA_QK_DIM, s), bf16),
        jax.ShapeDtypeStruct((b, MLA_HEADS, s, MLA_QK_DIM + K_PAD), bf16),
        jax.ShapeDtypeStruct((b, MLA_HEADS, 1, s), f32),
        jax.ShapeDtypeStruct((b, MLA_HEADS, s // tk, V_ROWS, tk), bf16),
    )
    out_specs = (
        pl.BlockSpec((1, tt, POOL_DIM), lambda bi, i: (bi, i, 0)),
        pl.BlockSpec((1, GQA_HEADS, HEAD_DIM, tt), lambda bi, i: (bi, 0, 0, i)),
        pl.BlockSpec((1, GQA_KV_HEADS, tt, HEAD_DIM + K_PAD), lambda bi, i: (bi, 0, i, 0)),
        pl.BlockSpec((1, GQA_KV_HEADS, 1, tt), lambda bi, i: (bi, 0, 0, i)),
        pl.BlockSpec((1, GQA_KV_HEADS, n_sub, V_ROWS, tk), lambda bi, i: (bi, 0, i, 0, 0)),
        pl.BlockSpec((1, MLA_HEADS, MLA_QK_DIM, tt), lambda bi, i: (bi, 0, 0, i)),
        pl.BlockSpec((1, MLA_HEADS, tt, MLA_QK_DIM + K_PAD), lambda bi, i: (bi, 0, i, 0)),
        pl.BlockSpec((1, MLA_HEADS, 1, tt), lambda bi, i: (bi, 0, 0, i)),
        pl.BlockSpec((1, MLA_HEADS, n_sub, V_ROWS, tk), lambda bi, i: (bi, 0, i, 0, 0)),
    )
    return pl.pallas_call(
        functools.partial(_inproj_kernel, tk=tk),
        grid=(b, n_t),
        in_specs=[
            pl.BlockSpec((1, tt, d), lambda bi, i: (bi, i, 0)),
            full((1, d)), vec, vec,
            full(wp.shape), full(wr.shape), full(gq.shape), full(gk.shape), full(gmq.shape), full(gmkv.shape),
            full(wuq.shape), full(wukv.shape), rope_g, rope_g, rope_m, rope_m,
        ],
        out_specs=out_specs,
        out_shape=out_shapes,
        compiler_params=_cparams(("parallel", "parallel"), V7X_VMEM_LIMIT_BYTES),
        name="inproj",
    )(x, g1, sh1, sc1, wp, wr, gq, gk, gmq, gmkv, wuq, wukv, cg, sg, cm, sm)


SAFE_LOGIT_BOUND = 50.0


def _attn_kernel(q_ref, k_ref, kn_ref, v_ref, o_ref, s0_ref, m_ref, l_ref, acc_ref, *, tk, n_chunks, dv, n_sub):
    group, tq = q_ref.shape[1], q_ref.shape[3]
    ts = tq // n_sub
    streams = [(h, j) for h in range(group) for j in range(n_sub)]
    qts = [q_ref[0, h, :, j * ts:(j + 1) * ts] for h, j in streams]

    def k_chunk(c):
        return k_ref[0, 0, pl.ds(pl.multiple_of(c * tk, tk), tk), :]

    def finish():
        for n, (h, j) in enumerate(streams):
            l = jnp.sum(l_ref[n], axis=0, keepdims=True)
            o_ref[0, h * dv:(h + 1) * dv, j * ts:(j + 1) * ts] = (acc_ref[n] / l).astype(o_ref.dtype)

    def row_sums(p):
        return jnp.sum(p.reshape(tk // V7X_SUBLANES, V7X_SUBLANES, ts), axis=0)

    k_max = jnp.max(kn_ref[0, 0], axis=-1, keepdims=True)
    bounds = []
    for qt in qts:
        qf = qt.astype(f32)
        bounds.append(jnp.sqrt(jnp.sum(qf * qf, axis=0, keepdims=True)) * k_max)
    worst = functools.reduce(jnp.maximum, [jnp.max(u, axis=-1, keepdims=True) for u in bounds])
    safe = worst[0, 0] <= SAFE_LOGIT_BOUND

    @pl.when(safe)
    def _():
        row0 = lax.broadcasted_iota(i32, (K_PAD, ts), 0) == 0
        qes = [jnp.concatenate([qt, jnp.where(row0, -c, 0.0).astype(qt.dtype)], axis=0) for qt, c in zip(qts, bounds)]
        acc_ref[...] = jnp.zeros(acc_ref.shape, f32)
        l_ref[...] = jnp.zeros(l_ref.shape, f32)
        s0_ref[...] = _dot(k_chunk(0), qes[0])
        per_trip = math.gcd(n_chunks, CHUNKS_PER_TRIP)
        order = [(dc, n) for dc in range(per_trip) for n in range(len(streams))]

        def step(i, carry):
            c = per_trip * i
            kcs = [k_chunk(jnp.minimum(c + dc, n_chunks - 1)) for dc in range(per_trip + 1)]
            vcs = [v_ref[0, 0, c + dc] for dc in range(per_trip)]
            s_cur = s0_ref[...]
            for idx, (dc, n) in enumerate(order):
                dc2, n2 = order[idx + 1] if idx + 1 < len(order) else (per_trip, 0)
                s_next = _dot(kcs[dc2], qes[n2])
                p = jnp.exp2(s_cur)
                l_ref[n] += row_sums(p)
                acc_ref[n] += _dot(vcs[dc], p.astype(bf16))
                s_cur = s_next
            s0_ref[...] = s_cur
            return carry

        lax.fori_loop(0, n_chunks // per_trip, step, 0)
        finish()

    @pl.when(jnp.logical_not(safe))
    def _():
        m_ref[...] = jnp.full(m_ref.shape, -jnp.inf, f32)
        acc_ref[...] = jnp.zeros(acc_ref.shape, f32)
        l_ref[...] = jnp.zeros(l_ref.shape, f32)
        qzs = [jnp.concatenate([qt, jnp.zeros((K_PAD, ts), qt.dtype)], axis=0) for qt in qts]

        def step(c, carry):
            kc = k_chunk(c)
            vc = v_ref[0, 0, c]
            for n, qz in enumerate(qzs):
                s = _dot(kc, qz)
                m = m_ref[n]
                m_new = jnp.maximum(m, jnp.max(s, axis=0, keepdims=True))
                p = jnp.exp2(s - m_new)
                alpha = jnp.exp2(m - m_new)
                l_ref[n] = l_ref[n] * alpha + row_sums(p)
                acc_ref[n] = acc_ref[n] * alpha + _dot(vc, p.astype(bf16))
                m_ref[n] = m_new
            return carry

        lax.fori_loop(0, n_chunks, step, 0)
        finish()


def _attention(qt, k, kn, vt, *, group, tq, n_sub):
    b, hq, dq, s = qt.shape
    _, hk, n_chunks, v_rows, tk = vt.shape
    dv = MLA_V_DIM
    assert tq % n_sub == 0
    n_streams, ts = group * n_sub, tq // n_sub
    return pl.pallas_call(
        functools.partial(_attn_kernel, tk=tk, n_chunks=n_chunks, dv=dv, n_sub=n_sub),
        grid=(b, hk, s // tq),
        scratch_shapes=[
            pltpu.VMEM((tk, ts), f32),
            pltpu.VMEM((n_streams, 1, ts), f32),
            pltpu.VMEM((n_streams, V7X_SUBLANES, ts), f32),
            pltpu.VMEM((n_streams, v_rows, ts), f32),
        ],
        in_specs=[
            pl.BlockSpec((1, group, dq, tq), lambda bi, g, i: (bi, g, 0, i)),
            pl.BlockSpec((1, 1, s, dq + K_PAD), lambda bi, g, i: (bi, g, 0, 0)),
            pl.BlockSpec((1, 1, 1, s), lambda bi, g, i: (bi, g, 0, 0)),
            pl.BlockSpec((1, 1, n_chunks, v_rows, tk), lambda bi, g, i: (bi, g, 0, 0, 0)),
        ],
        out_specs=pl.BlockSpec((1, group * dv, tq), lambda bi, g, i: (bi, g, i)),
        out_shape=jax.ShapeDtypeStruct((b, hq * dv, s), bf16),
        compiler_params=_cparams(("parallel", "parallel", "parallel"), V7X_VMEM_LIMIT_BYTES),
        name="attention",
    )(qt, k, kn, vt)


POOL_HALO = 16


def _pool_kernel(up_ref, uc_ref, un_ref, w_ref, sc_ref, o_ref, *, seq):
    i = pl.program_id(1)
    n_t = pl.num_programs(1)
    cur = uc_ref[0]
    tp = cur.shape[0]
    prev = jnp.where(i > 0, up_ref[0], 0.0)
    nxt = jnp.where(i < n_t - 1, un_ref[0], 0.0)
    ext = jnp.concatenate([prev, cur, nxt], axis=0)
    t = i * tp + lax.broadcasted_iota(i32, (tp, 1), 0)
    lane = lax.broadcasted_iota(i32, (tp, POOL_DIM), 1)
    run, length = ext, 1
    wsum = cnt = None
    for gi, win in enumerate(POOL_WINDOWS):
        while length < win:
            run = run[:run.shape[0] - length] + run[length:]
            length *= 2
        lo = win // 2
        hi = win - 1 - lo
        mine = run[POOL_HALO - lo:POOL_HALO - lo + tp]
        n_valid = (jnp.minimum(t + hi, seq - 1) - jnp.maximum(t - lo, 0) + 1).astype(f32)
        if gi == 0:
            wsum, cnt = mine, jnp.broadcast_to(n_valid, (tp, POOL_DIM))
        else:
            in_later_group = lane >= gi * POOL_CH
            wsum = jnp.where(in_later_group, mine, wsum)
            cnt = jnp.where(in_later_group, n_valid, cnt)
    p = (wsum / cnt - cur).astype(bf16)
    o_ref[0] = (_dot(p, w_ref[...]) * sc_ref[...]).astype(o_ref.dtype)


def _pool(u, w_bd, scale, *, tp):
    b, s, c = u.shape
    n_t = s // tp
    r = tp // POOL_HALO
    return pl.pallas_call(
        functools.partial(_pool_kernel, seq=s),
        grid=(b, n_t),
        in_specs=[
            pl.BlockSpec((1, POOL_HALO, c), lambda bi, i: (bi, jnp.maximum(i * r - 1, 0), 0)),
            pl.BlockSpec((1, tp, c), lambda bi, i: (bi, i, 0)),
            pl.BlockSpec((1, POOL_HALO, c), lambda bi, i: (bi, jnp.minimum((i + 1) * r, s // POOL_HALO - 1), 0)),
            pl.BlockSpec(w_bd.shape, lambda bi, i: (0, 0)),
            pl.BlockSpec((1, c), lambda bi, i: (0, 0)),
        ],
        out_specs=pl.BlockSpec((1, tp, c), lambda bi, i: (bi, i, 0)),
        out_shape=jax.ShapeDtypeStruct((b, s, c), bf16),
        compiler_params=_cparams(("parallel", "parallel")),
        name="pool",
    )(u, u, u, w_bd, scale)


def _outproj_kernel(x_ref, yp_ref, og_ref, om_ref, wop_ref, wog_ref, wom_ref, gt_ref, g2_ref, sh_ref, sc_ref,
                    wrt_ref, x1_ref, h2_ref, aff_ref):
    tn = (((0,), (0,)), ((), ()))
    y = _dot(yp_ref[0], wop_ref[...])
    y = y + lax.dot_general(og_ref[0], wog_ref[...], tn, preferred_element_type=f32)
    y = y + lax.dot_general(om_ref[0], wom_ref[...], tn, preferred_element_type=f32)
    x1 = x_ref[0] + gt_ref[0] * y
    x1_ref[0] = x1
    h = x1 * lax.rsqrt(jnp.mean(x1 * x1, axis=-1, keepdims=True) + EPS) * g2_ref[...]
    h = h * (1.0 + sc_ref[0]) + sh_ref[0]
    h2_ref[0] = h.astype(bf16)
    logits = _dot3(h, wrt_ref[...])
    ex = jnp.exp(logits - jnp.max(logits, axis=-1, keepdims=True))
    aff_ref[0] = ex / jnp.sum(ex, axis=-1, keepdims=True)


def _outproj(x, ypool, og, om, wop, wog, wom, gt1, g2, sh2, sc2, w_router, *, tt):
    b, s, d = x.shape
    full = lambda shape: pl.BlockSpec(shape, lambda bi, i: (0,) * len(shape))
    vec = pl.BlockSpec((1, 1, d), lambda bi, i: (bi, 0, 0))
    n_e = w_router.shape[1]
    return pl.pallas_call(
        _outproj_kernel,
        grid=(b, s // tt),
        in_specs=[
            pl.BlockSpec((1, tt, d), lambda bi, i: (bi, i, 0)),
            pl.BlockSpec((1, tt, POOL_DIM), lambda bi, i: (bi, i, 0)),
            pl.BlockSpec((1, og.shape[1], tt), lambda bi, i: (bi, 0, i)),
            pl.BlockSpec((1, om.shape[1], tt), lambda bi, i: (bi, 0, i)),
            full(wop.shape), full(wog.shape), full(wom.shape),
            vec, full((1, d)), vec, vec, full(w_router.shape),
        ],
        out_specs=(
            pl.BlockSpec((1, tt, d), lambda bi, i: (bi, i, 0)),
            pl.BlockSpec((1, tt, d), lambda bi, i: (bi, i, 0)),
            pl.BlockSpec((1, tt, n_e), lambda bi, i: (bi, i, 0)),
        ),
        out_shape=(
            jax.ShapeDtypeStruct((b, s, d), f32),
            jax.ShapeDtypeStruct((b, s, d), bf16),
            jax.ShapeDtypeStruct((b, s, n_e), f32),
        ),
        compiler_params=_cparams(("parallel", "parallel"), V7X_VMEM_LIMIT_BYTES),
        name="outproj",
    )(x, ypool, og, om, wop, wog, wom, gt1, g2, sh2, sc2, w_router)


def _route_kernel(a_ref, posm_ref, pos_ref, *, cap):
    a = a_ref[0]
    n_e, nc, ln = a.shape
    bits = pltpu.bitcast(a, i32)

    def count(mask):
        c = jnp.sum(jnp.where(mask, 1.0, 0.0), axis=2, keepdims=True)
        return jnp.sum(c, axis=1, keepdims=True)

    thr = jnp.zeros((n_e, 1, 1), i32)
    for bit in range(30, -1, -1):
        cand = thr | (1 << bit)
        thr = jnp.where(count(bits >= cand) >= cap, cand, thr)
    gt = bits > thr
    eq = bits == thr
    need = cap - count(gt)

    r_i = lax.broadcasted_iota(i32, (ln, ln), 0)
    c_i = lax.broadcasted_iota(i32, (ln, ln), 1)
    tri_incl = jnp.where(r_i <= c_i, 1.0, 0.0).astype(bf16)
    r_c = lax.broadcasted_iota(i32, (nc, nc), 0)
    c_c = lax.broadcasted_iota(i32, (nc, nc), 1)
    tri_strict = jnp.where(c_c < r_c, 1.0, 0.0).astype(bf16)

    def excl_prefix(mask):
        x = jnp.where(mask, 1.0, 0.0)
        incl = _dot(x.astype(bf16).reshape(n_e * nc, ln), tri_incl).reshape(n_e, nc, ln)
        tot = jnp.broadcast_to(incl[:, :, ln - 1:ln], (n_e, nc, ln))
        tot_hi = tot.astype(bf16)
        offs = [_dot(tri_strict, tot_hi[e]) for e in range(n_e)]
        return jnp.stack(offs, axis=0) + incl - x

    sel = gt | (eq & (excl_prefix(eq) < need))
    pos = excl_prefix(sel).astype(i32)
    pos_ref[0] = pos
    posm_ref[0] = jnp.where(sel, pos, -1)


def _route(aff_r, *, cap):
    b, n_e, nc, ln = aff_r.shape
    spec = pl.BlockSpec((1, n_e, nc, ln), lambda bi: (bi, 0, 0, 0))
    return pl.pallas_call(
        functools.partial(_route_kernel, cap=cap),
        grid=(b,),
        in_specs=[spec],
        out_specs=(spec, spec),
        out_shape=(jax.ShapeDtypeStruct(aff_r.shape, i32), jax.ShapeDtypeStruct(aff_r.shape, i32)),
        compiler_params=_cparams(("parallel",), V7X_VMEM_LIMIT_BYTES),
        name="route",
    )(aff_r)


def _window_start(lo, width, cap):
    aligned = lax.shift_left(lax.shift_right_logical(lo, SLOT_ALIGN_LOG2), SLOT_ALIGN_LOG2)
    return pl.multiple_of(jnp.minimum(aligned, cap - width), 1 << SLOT_ALIGN_LOG2)


def _window_widths(cap):
    return min(FAST_SLOTS, cap), min(TOK_CHUNK + (1 << SLOT_ALIGN_LOG2), cap)


def _gather_kernel(offs_ref, h_ref, posm_ref, aff_ref, xe_ref, gate_ref, *, n_off, n_chunks):
    b, e = pl.program_id(0), pl.program_id(1)
    base = (b * pl.num_programs(1) + e) * n_off
    step = TOK_CHUNK // V7X_LANES
    cap = xe_ref.shape[2]
    fast_w, slow_w = _window_widths(cap)
    xe_ref[...] = jnp.zeros_like(xe_ref)
    gate_ref[...] = jnp.zeros_like(gate_ref)

    def misfit(c, bad):
        lo = offs_ref[base + c * step]
        hi = offs_ref[base + (c + 1) * step]
        return bad + (hi - _window_start(lo, fast_w, cap) > fast_w).astype(i32)

    bad = lax.fori_loop(0, n_chunks, misfit, jnp.int32(0))

    def run(width):
        slot_iota = lax.broadcasted_iota(i32, (width, TOK_CHUNK), 0)

        def chunk(c, carry):
            w = _window_start(offs_ref[base + c * step], width, cap)
            tok0 = pl.multiple_of(c * TOK_CHUNK, TOK_CHUNK)
            pr = posm_ref[0, 0, c]
            hit = pr == slot_iota + w
            rows = _dot(jnp.where(hit, 1.0, 0.0).astype(bf16), h_ref[0, pl.ds(tok0, TOK_CHUNK), :])
            xe_ref[0, 0, pl.ds(w, width), :] = xe_ref[0, 0, pl.ds(w, width), :] + rows.astype(xe_ref.dtype)
            gate_ref[0, 0, pl.ds(w, width), :] += jnp.sum(jnp.where(hit, aff_ref[0, 0, c], 0.0), axis=1, keepdims=True)
            return carry

        lax.fori_loop(0, n_chunks, chunk, 0, unroll=GATHER_UNROLL)

    @pl.when(bad == 0)
    def _():
        run(fast_w)

    @pl.when(bad != 0)
    def _():
        run(slow_w)


def _gather(offs, h2, posm_c, aff_c, *, cap):
    b, s, d = h2.shape
    n_e = posm_c.shape[1]
    n_chunks = s // TOK_CHUNK
    n_off = s // V7X_LANES + 1
    chunked = pl.BlockSpec((1, 1, n_chunks, 1, TOK_CHUNK), lambda bi, e, offs: (bi, e, 0, 0, 0))
    return pl.pallas_call(
        functools.partial(_gather_kernel, n_off=n_off, n_chunks=n_chunks),
        grid_spec=pltpu.PrefetchScalarGridSpec(
            num_scalar_prefetch=1,
            grid=(b, n_e),
            in_specs=[
                pl.BlockSpec((1, s, d), lambda bi, e, offs: (bi, 0, 0), pipeline_mode=pl.Buffered(1)),
                chunked, chunked,
            ],
            out_specs=(pl.BlockSpec((1, 1, cap, d), lambda bi, e, offs: (bi, e, 0, 0)),
                       pl.BlockSpec((1, 1, cap, 1), lambda bi, e, offs: (bi, e, 0, 0))),
        ),
        out_shape=(jax.ShapeDtypeStruct((b, n_e, cap, d), bf16), jax.ShapeDtypeStruct((b, n_e, cap, 1), f32)),
        compiler_params=_cparams(("arbitrary", "arbitrary"), V7X_VMEM_LIMIT_BYTES),
        name="gather",
    )(offs, h2, posm_c, aff_c)


def _ffn_kernel(x_ref, gate_ref, wg_ref, wu_ref, wd_ref, o_ref, acc_ref):
    @pl.when(pl.program_id(2) == 0)
    def _():
        acc_ref[...] = jnp.zeros_like(acc_ref)

    x = x_ref[0, 0]
    a = _dot(x, wg_ref[0, 0].astype(bf16))
    u = _dot(x, wu_ref[0, 0].astype(bf16))
    hmid = (a * (1.0 / (1.0 + jnp.exp(-a))) * u).astype(bf16)
    total = acc_ref[...] + _dot(hmid, wd_ref[0, 0].astype(bf16))
    acc_ref[...] = total
    o_ref[0, 0] = (total * gate_ref[0, 0]).astype(o_ref.dtype)


def _ffn(xe, gate, w_gate, w_up, w_down, layer):
    b, n_e, cap, d = xe.shape
    d_ff = w_gate.shape[-1]
    n_f = d_ff // FF_TILE
    return pl.pallas_call(
        _ffn_kernel,
        grid=(n_e, b, n_f),
        in_specs=[
            pl.BlockSpec((1, 1, cap, d), lambda e, bi, f: (bi, e, 0, 0)),
            pl.BlockSpec((1, 1, cap, 1), lambda e, bi, f: (bi, e, 0, 0)),
            pl.BlockSpec((1, 1, d, FF_TILE), lambda e, bi, f: (layer, e, 0, f)),
            pl.BlockSpec((1, 1, d, FF_TILE), lambda e, bi, f: (layer, e, 0, f)),
            pl.BlockSpec((1, 1, FF_TILE, d), lambda e, bi, f: (layer, e, f, 0)),
        ],
        out_specs=pl.BlockSpec((1, 1, cap, d), lambda e, bi, f: (bi, e, 0, 0)),
        out_shape=jax.ShapeDtypeStruct((b, n_e, cap, d), bf16),
        scratch_shapes=[pltpu.VMEM((cap, d), f32)],
        compiler_params=_cparams(("parallel", "parallel", "arbitrary"), V7X_VMEM_LIMIT_BYTES),
        name="expert_ffn",
    )(xe, gate, w_gate, w_up, w_down)


def _combine_kernel(offs_ref, x_ref, posm_ref, gt_ref, ye_ref, o_ref, *, n_off):
    b, i = pl.program_id(0), pl.program_id(2)
    n_e, cap = ye_ref.shape[1], ye_ref.shape[2]
    fast_w, slow_w = _window_widths(cap)
    step = TOK_CHUNK // V7X_LANES
    posm = posm_ref[0]
    los = [offs_ref[(b * n_e + e) * n_off + i * step] for e in range(n_e)]
    his = [offs_ref[(b * n_e + e) * n_off + (i + 1) * step] for e in range(n_e)]
    bad = functools.reduce(
        lambda a, c: a + c, [(his[e] - _window_start(los[e], fast_w, cap) > fast_w).astype(i32) for e in range(n_e)])
    paired = 2 * fast_w == V7X_LANES and n_e % 2 == 0

    def run_stacked():
        lane = lax.broadcasted_iota(i32, (TOK_CHUNK, V7X_LANES), 1)
        ws = [_window_start(los[e], fast_w, cap) for e in range(n_e)]
        tiles, rows = [], []
        for e in range(0, n_e, 2):
            target = jnp.where(lane < fast_w, posm[:, e:e + 1] - ws[e], posm[:, e + 1:e + 2] - ws[e + 1] + fast_w)
            tiles.append(jnp.where(target == lane, 1.0, 0.0).astype(bf16))
            rows += [ye_ref[0, e, pl.ds(ws[e], fast_w), :], ye_ref[0, e + 1, pl.ds(ws[e + 1], fast_w), :]]
        total = _dot(jnp.concatenate(tiles, axis=1), jnp.concatenate(rows, axis=0))
        o_ref[0] = x_ref[0] + gt_ref[0] * total

    def run_per_expert(width):
        slot_iota = lax.broadcasted_iota(i32, (TOK_CHUNK, width), 1)
        total = jnp.zeros((TOK_CHUNK, o_ref.shape[2]), f32)
        for e in range(n_e):
            w = _window_start(los[e], width, cap)
            onehot = jnp.where(posm[:, e:e + 1] == slot_iota + w, 1.0, 0.0).astype(bf16)
            total = total + _dot(onehot, ye_ref[0, e, pl.ds(w, width), :])
        o_ref[0] = x_ref[0] + gt_ref[0] * total

    @pl.when(bad == 0)
    def _():
        run_stacked() if paired else run_per_expert(fast_w)

    @pl.when(bad != 0)
    def _():
        run_per_expert(slow_w)


def _combine(offs, x1, posm_t, gt2, ye):
    b, s, d = x1.shape
    n_e, cap = ye.shape[1], ye.shape[2]
    n_off = s // V7X_LANES + 1
    return pl.pallas_call(
        functools.partial(_combine_kernel, n_off=n_off),
        grid_spec=pltpu.PrefetchScalarGridSpec(
            num_scalar_prefetch=1,
            grid=(b, d // COL_TILE, s // TOK_CHUNK),
            in_specs=[
                pl.BlockSpec((1, TOK_CHUNK, COL_TILE), lambda bi, j, i, offs: (bi, i, j)),
                pl.BlockSpec((1, TOK_CHUNK, n_e), lambda bi, j, i, offs: (bi, i, 0)),
                pl.BlockSpec((1, 1, COL_TILE), lambda bi, j, i, offs: (bi, 0, j)),
                pl.BlockSpec((1, n_e, cap, COL_TILE), lambda bi, j, i, offs: (bi, 0, 0, j),
                             pipeline_mode=pl.Buffered(1)),
            ],
            out_specs=pl.BlockSpec((1, TOK_CHUNK, COL_TILE), lambda bi, j, i, offs: (bi, i, j)),
        ),
        out_shape=jax.ShapeDtypeStruct((b, s, d), f32),
        compiler_params=_cparams(("arbitrary", "arbitrary", "arbitrary"), V7X_VMEM_LIMIT_BYTES),
        name="combine",
    )(offs, x1, posm_t, gt2, ye)


def _final_kernel(x_ref, g_ref, o_ref):
    x = x_ref[0]
    o_ref[0] = x * lax.rsqrt(jnp.mean(x * x, axis=-1, keepdims=True) + EPS) * g_ref[...]


def _final_norm(x, g, *, tt):
    b, s, d = x.shape
    return pl.pallas_call(
        _final_kernel,
        grid=(b, s // tt),
        in_specs=[pl.BlockSpec((1, tt, d), lambda bi, i: (bi, i, 0)), pl.BlockSpec((1, d), lambda bi, i: (0, 0))],
        out_specs=pl.BlockSpec((1, tt, d), lambda bi, i: (bi, i, 0)),
        out_shape=jax.ShapeDtypeStruct((b, s, d), f32),
        compiler_params=_cparams(("parallel", "parallel")),
        name="final_norm",
    )(x, g)


def _deinterleave(n):
    return np.concatenate([np.arange(0, n, 2), np.arange(1, n, 2)])


def _rope_tables_t(n, d_rot):
    n_rows = n // GRID_W
    row = jnp.repeat(jnp.arange(n_rows, dtype=f32), GRID_W)
    col = jnp.tile(jnp.arange(GRID_W, dtype=f32), n_rows)
    n_freq = d_rot // 4
    inv_freq = ROPE_THETA ** (-jnp.arange(n_freq, dtype=f32) / n_freq)
    ang = jnp.concatenate([row[:, None] * inv_freq, col[:, None] * inv_freq], axis=-1)
    return jnp.cos(ang).T, jnp.sin(ang).T


def _rest_columns():
    p64, p32 = _deinterleave(HEAD_DIM), _deinterleave(MLA_ROPE_DIM)
    cols = [OFF_GQA_Q + h * HEAD_DIM + p64 for h in range(GQA_HEADS)]
    cols += [OFF_GQA_K + h * HEAD_DIM + p64 for h in range(GQA_KV_HEADS)]
    cols += [np.arange(OFF_GQA_V, IN_DIM - MLA_ROPE_DIM), OFF_MLA_ROPE + p32]
    return np.concatenate(cols)


def _uq_columns():
    p32 = _deinterleave(MLA_ROPE_DIM)
    cols = []
    for h in range(MLA_HEADS):
        cols += [h * MLA_QK_DIM + np.arange(MLA_NOPE_DIM), h * MLA_QK_DIM + MLA_NOPE_DIM + p32]
    return np.concatenate(cols)


def _block_diag(w):
    g, c, _ = w.shape
    out = jnp.zeros((g * c, g * c), w.dtype)
    for i in range(g):
        out = out.at[i * c:(i + 1) * c, i * c:(i + 1) * c].set(w[i])
    return out


def _trunk(x, c, w_mod, b_mod, g_norm1, w_in, pool_w, pool_scale, gqa_q_gain, gqa_k_gain, mla_q_gain, mla_kv_gain,
           mla_w_uq, mla_w_ukv, w_out, g_norm2, w_router, w_gate, w_up, w_down, g_final):
    b, s, d = x.shape
    depth = w_mod.shape[0]
    cap = (EC_CAPACITY * s) // N_EXPERTS
    tt = min(TOK_TILE, s)
    tq = min(ATT_TQ, s)
    tk = min(ATT_TK, s)
    nc = s // V7X_LANES

    mod_rows = 8
    c_pad = jnp.zeros((mod_rows, d), f32).at[:b].set(c)
    mod = _modulation(c_pad, w_mod, b_mod)[:, :b].reshape(depth, b, 6, 1, d)

    cg, sg = _rope_tables_t(s, HEAD_DIM)
    cm, sm = _rope_tables_t(s, MLA_ROPE_DIM)
    p64 = _deinterleave(HEAD_DIM)
    rest_cols, uq_cols = _rest_columns(), _uq_columns()

    wp_all = w_in[:, :, :POOL_DIM].astype(bf16)
    wr_all = jnp.swapaxes(w_in[:, :, rest_cols], 1, 2).astype(bf16)
    wuq_all = jnp.swapaxes(mla_w_uq[:, :, uq_cols], 1, 2).astype(bf16)
    wukv_all = jnp.swapaxes(mla_w_ukv, 1, 2).astype(bf16)
    gq_all, gk_all = gqa_q_gain[:, p64, None], gqa_k_gain[:, p64, None]
    gmq_all, gmkv_all = mla_q_gain[:, :, None], mla_kv_gain[:, :, None]
    pw_all = jax.vmap(_block_diag)(pool_w).astype(bf16)
    wo_all = w_out.astype(bf16)
    n_g = GQA_HEADS * HEAD_DIM

    for l in range(depth):
        sh1, sc1, gt1, sh2, sc2, gt2 = (mod[l, :, k] for k in range(6))
        u, qg, kg, kng, vg, qm, km, knm, vm = _inproj(
            x, g_norm1[l][None], sh1, sc1, wp_all[l], wr_all[l], gq_all[l], gk_all[l], gmq_all[l], gmkv_all[l],
            wuq_all[l], wukv_all[l], cg, sg, cm, sm, tt=tt, tk=tk)
        og = _attention(qg, kg, kng, vg, group=GQA_GROUP, tq=min(GQA_TQ, s), n_sub=min(GQA_TQ, s) // tq)
        om = _attention(qm, km, knm, vm, group=1, tq=min(MLA_TQ, s), n_sub=min(MLA_TQ, s) // tq)
        ypool = _pool(u, pw_all[l], pool_scale[l][None], tp=tt)
        wo = wo_all[l]
        x1, h2, aff = _outproj(x, ypool, og, om, wo[:POOL_DIM], wo[POOL_DIM:POOL_DIM + n_g], wo[POOL_DIM + n_g:],
                               gt1, g_norm2[l][None], sh2, sc2, w_router[l], tt=tt)

        aff_r = aff.transpose(0, 2, 1).reshape(b, N_EXPERTS, nc, V7X_LANES)
        posm, pos = _route(aff_r, cap=cap)
        offs = jnp.concatenate([pos[..., 0], jnp.full((b, N_EXPERTS, 1), cap, i32)], axis=-1).reshape(-1)
        posm_c = posm.reshape(b, N_EXPERTS, s // TOK_CHUNK, 1, TOK_CHUNK)
        posm_t = posm.reshape(b, N_EXPERTS, s).transpose(0, 2, 1)
        aff_c = aff_r.reshape(b, N_EXPERTS, s // TOK_CHUNK, 1, TOK_CHUNK)
        xe, gate = _gather(offs, h2, posm_c, aff_c, cap=cap)
        ye = _ffn(xe, gate, w_gate, w_up, w_down, l)
        x = _combine(offs, x1, posm_t, gt2, ye)
    return _final_norm(x, g_final[None], tt=tt)


def kernel(x, c, w_mod, b_mod, g_norm1, w_in, pool_w, pool_scale, gqa_q_gain, gqa_k_gain, mla_q_gain, mla_kv_gain,
           mla_w_uq, mla_w_ukv, w_out, g_norm2, w_router, w_gate, w_up, w_down, g_final):
    return _trunk(x, c, w_mod, b_mod, g_norm1, w_in, pool_w, pool_scale, gqa_q_gain, gqa_k_gain, mla_q_gain,
                  mla_kv_gain, mla_w_uq, mla_w_ukv, w_out, g_norm2, w_router, w_gate, w_up, w_down, g_final)
```

```python
import functools
import math

import numpy as np
import jax
import jax.numpy as jnp
from jax import lax
from jax.experimental import pallas as pl
from jax.experimental.pallas import tpu as pltpu

f32, bf16, i32 = jnp.float32, jnp.bfloat16, jnp.int32

D_MODEL = 1024
DEPTH = 4
GRID_W = 64
ROPE_THETA = 10000.0
EPS = 1e-6
POOL_DIM = 256
POOL_WINDOWS = (2, 4, 8, 16)
POOL_CH = 64
HEAD_DIM = 64
GQA_HEADS = 6
GQA_KV_HEADS = 2
GQA_GROUP = 3
MLA_HEADS = 6
MLA_NOPE_DIM = 64
MLA_ROPE_DIM = 32
MLA_QK_DIM = 96
MLA_V_DIM = 64
MLA_Q_RANK = 256
MLA_KV_RANK = 256
OFF_GQA_Q = POOL_DIM
OFF_GQA_K = OFF_GQA_Q + GQA_HEADS * HEAD_DIM
OFF_GQA_V = OFF_GQA_K + GQA_KV_HEADS * HEAD_DIM
OFF_MLA_Q = OFF_GQA_V + GQA_KV_HEADS * HEAD_DIM
OFF_MLA_KV = OFF_MLA_Q + MLA_Q_RANK
OFF_MLA_ROPE = OFF_MLA_KV + MLA_KV_RANK
IN_DIM = OFF_MLA_ROPE + MLA_ROPE_DIM
REST_DIM = IN_DIM - POOL_DIM
N_EXPERTS = 16
EC_CAPACITY = 2
D_FF = 2048

R_GQ = 0
R_GK = R_GQ + GQA_HEADS * HEAD_DIM
R_GV = R_GK + GQA_KV_HEADS * HEAD_DIM
R_MQ = R_GV + GQA_KV_HEADS * HEAD_DIM
R_MKV = R_MQ + MLA_Q_RANK
R_MR = R_MKV + MLA_KV_RANK

V7X_LANES = 128
V7X_VMEM_LIMIT_BYTES = 60000 * 1024
V_ROWS = 64
V7X_SUBLANES = 8
K_PAD = 16

TOK_TILE = 512
ATT_TQ = 512
GQA_TQ = 1024
MLA_TQ = 2048
ATT_TK = 512
TOK_CHUNK = 256
FAST_SLOTS = 64
SLOT_ALIGN_LOG2 = 4
GATHER_UNROLL = 8
CHUNKS_PER_TRIP = 8
FFN_ROW_SPLIT = 2
OUTPROJ_ROW_SPLIT = 2
FF_TILE = 512
COL_TILE = 512
COMBINE_TOK = 1024
LOG2E = math.log2(math.e)


def _cparams(sem, vmem=None):
    return pltpu.CompilerParams(dimension_semantics=sem, vmem_limit_bytes=vmem)


def _split_bf16(a):
    hi = a.astype(bf16)
    lo = (a - hi.astype(f32)).astype(bf16)
    return hi, lo


def _dot(a, b):
    return jnp.dot(a, b, preferred_element_type=f32)


def _dot3(a, b):
    ah, al = _split_bf16(a)
    bh, bl = _split_bf16(b)
    return _dot(ah, bh) + _dot(ah, bl) + _dot(al, bh)


def _mod_kernel(c_ref, w_ref, b_ref, o_ref):
    c = c_ref[...]
    act = c * (1.0 / (1.0 + jnp.exp(-c)))
    o_ref[0] = _dot3(act, w_ref[0]) + b_ref[0]


def _modulation(c_pad, w_mod, b_mod):
    depth, d, six_d = w_mod.shape
    rows = c_pad.shape[0]
    return pl.pallas_call(
        _mod_kernel,
        grid=(depth, six_d // d),
        in_specs=[
            pl.BlockSpec((rows, d), lambda l, j: (0, 0)),
            pl.BlockSpec((1, d, d), lambda l, j: (l, 0, j)),
            pl.BlockSpec((1, 1, d), lambda l, j: (l, 0, j)),
        ],
        out_specs=pl.BlockSpec((1, rows, d), lambda l, j: (l, 0, j)),
        out_shape=jax.ShapeDtypeStruct((depth, rows, six_d), f32),
        compiler_params=_cparams(("parallel", "parallel")),
        name="modulation",
    )(c_pad, w_mod, b_mod.reshape(depth, 1, six_d))


def _rms_rows(z, gain_col):
    r = lax.rsqrt(jnp.mean(z * z, axis=0, keepdims=True) + EPS)
    return z * r * gain_col


def _rope_rows(z, cos, sin):
    half = z.shape[0] // 2
    x1, x2 = z[:half], z[half:]
    return jnp.concatenate([x1 * cos - x2 * sin, x1 * sin + x2 * cos], axis=0)


def _inproj_kernel(x_ref, g_ref, sh_ref, sc_ref, wp_ref, wr_ref, gq_ref, gk_ref, gmq_ref, gmkv_ref,
                   wuq_ref, wukv_ref, cg_ref, sg_ref, cm_ref, sm_ref,
                   u_ref, qg_ref, kg_ref, kng_ref, vg_ref, qm_ref, km_ref, knm_ref, vm_ref, *, tk):
    x = x_ref[0]
    tt = x.shape[0]
    h = x * lax.rsqrt(jnp.mean(x * x, axis=-1, keepdims=True) + EPS) * g_ref[...]
    h = h * (1.0 + sc_ref[0]) + sh_ref[0]
    hb = h.astype(bf16)
    u_ref[0] = _dot(hb, wp_ref[...])
    zt = lax.dot_general(wr_ref[...], hb, (((1,), (1,)), ((), ())), preferred_element_type=f32)

    cg, sg, cm, sm = cg_ref[...], sg_ref[...], cm_ref[...], sm_ref[...]
    n_sub = tt // tk

    def put_k(ref, norm_ref, head, kt):
        kb = kt.astype(bf16)
        kf = kb.astype(f32)
        norm_ref[0, head] = jnp.sqrt(jnp.sum(kf * kf, axis=0, keepdims=True))
        pad = jnp.where(lax.broadcasted_iota(i32, (K_PAD, tt), 0) == 0, 1.0, 0.0)
        ke = jnp.concatenate([kt, pad], axis=0)
        ref[0, head] = ke.T.astype(bf16)

    def put_v(ref, head, vt):
        ve = vt.astype(bf16)
        for j in range(n_sub):
            ref[0, head, j] = ve[:, j * tk:(j + 1) * tk]

    gq = gq_ref[...] * (HEAD_DIM ** -0.5 * LOG2E)
    gk = gk_ref[...]
    for hd in range(GQA_HEADS):
        q = _rms_rows(zt[R_GQ + hd * HEAD_DIM:R_GQ + (hd + 1) * HEAD_DIM], gq)
        qg_ref[0, hd] = _rope_rows(q, cg, sg).astype(bf16)
    for hk in range(GQA_KV_HEADS):
        k = _rms_rows(zt[R_GK + hk * HEAD_DIM:R_GK + (hk + 1) * HEAD_DIM], gk)
        put_k(kg_ref, kng_ref, hk, _rope_rows(k, cg, sg))
        put_v(vg_ref, hk, zt[R_GV + hk * HEAD_DIM:R_GV + (hk + 1) * HEAD_DIM])

    cq = _rms_rows(zt[R_MQ:R_MQ + MLA_Q_RANK], gmq_ref[...]).astype(bf16)
    qm = _dot(wuq_ref[...], cq) * (MLA_QK_DIM ** -0.5 * LOG2E)
    ckv = _rms_rows(zt[R_MKV:R_MKV + MLA_KV_RANK], gmkv_ref[...]).astype(bf16)
    kv = _dot(wukv_ref[...], ckv)
    k_rope = _rope_rows(zt[R_MR:R_MR + MLA_ROPE_DIM], cm, sm)
    for hd in range(MLA_HEADS):
        qh = qm[hd * MLA_QK_DIM:(hd + 1) * MLA_QK_DIM]
        qr = _rope_rows(qh[MLA_NOPE_DIM:], cm, sm)
        qm_ref[0, hd] = jnp.concatenate([qh[:MLA_NOPE_DIM], qr], axis=0).astype(bf16)
        kvh = kv[hd * (MLA_NOPE_DIM + MLA_V_DIM):(hd + 1) * (MLA_NOPE_DIM + MLA_V_DIM)]
        kh = jnp.concatenate([kvh[:MLA_NOPE_DIM], k_rope], axis=0)
        put_k(km_ref, knm_ref, hd, kh)
        put_v(vm_ref, hd, kvh[MLA_NOPE_DIM:])


def _inproj(x, g1, sh1, sc1, wp, wr, gq, gk, gmq, gmkv, wuq, wukv, cg, sg, cm, sm, *, tt, tk):
    b, s, d = x.shape
    n_t = s // tt
    n_sub = tt // tk
    full = lambda shape: pl.BlockSpec(shape, lambda bi, i: (0,) * len(shape))
    vec = pl.BlockSpec((1, 1, d), lambda bi, i: (bi, 0, 0))
    rope_g = pl.BlockSpec((HEAD_DIM // 2, tt), lambda bi, i: (0, i))
    rope_m = pl.BlockSpec((MLA_ROPE_DIM // 2, tt), lambda bi, i: (0, i))
    out_shapes = (
        jax.ShapeDtypeStruct((b, s, POOL_DIM), f32),
        jax.ShapeDtypeStruct((b, GQA_HEADS, HEAD_DIM, s), bf16),
        jax.ShapeDtypeStruct((b, GQA_KV_HEADS, s, HEAD_DIM + K_PAD), bf16),
        jax.ShapeDtypeStruct((b, GQA_KV_HEADS, 1, s), f32),
        jax.ShapeDtypeStruct((b, GQA_KV_HEADS, s // tk, V_ROWS, tk), bf16),
        jax.ShapeDtypeStruct((b, MLA_HEADS, MLA_QK_DIM, s), bf16),
        jax.ShapeDtypeStruct((b, MLA_HEADS, s, MLA_QK_DIM + K_PAD), bf16),
        jax.ShapeDtypeStruct((b, MLA_HEADS, 1, s), f32),
        jax.ShapeDtypeStruct((b, MLA_HEADS, s // tk, V_ROWS, tk), bf16),
    )
    out_specs = (
        pl.BlockSpec((1, tt, POOL_DIM), lambda bi, i: (bi, i, 0)),
        pl.BlockSpec((1, GQA_HEADS, HEAD_DIM, tt), lambda bi, i: (bi, 0, 0, i)),
        pl.BlockSpec((1, GQA_KV_HEADS, tt, HEAD_DIM + K_PAD), lambda bi, i: (bi, 0, i, 0)),
        pl.BlockSpec((1, GQA_KV_HEADS, 1, tt), lambda bi, i: (bi, 0, 0, i)),
        pl.BlockSpec((1, GQA_KV_HEADS, n_sub, V_ROWS, tk), lambda bi, i: (bi, 0, i, 0, 0)),
        pl.BlockSpec((1, MLA_HEADS, MLA_QK_DIM, tt), lambda bi, i: (bi, 0, 0, i)),
        pl.BlockSpec((1, MLA_HEADS, tt, MLA_QK_DIM + K_PAD), lambda bi, i: (bi, 0, i, 0)),
        pl.BlockSpec((1, MLA_HEADS, 1, tt), lambda bi, i: (bi, 0, 0, i)),
        pl.BlockSpec((1, MLA_HEADS, n_sub, V_ROWS, tk), lambda bi, i: (bi, 0, i, 0, 0)),
    )
    return pl.pallas_call(
        functools.partial(_inproj_kernel, tk=tk),
        grid=(b, n_t),
        in_specs=[
            pl.BlockSpec((1, tt, d), lambda bi, i: (bi, i, 0)),
            full((1, d)), vec, vec,
            full(wp.shape), full(wr.shape), full(gq.shape), full(gk.shape), full(gmq.shape), full(gmkv.shape),
            full(wuq.shape), full(wukv.shape), rope_g, rope_g, rope_m, rope_m,
        ],
        out_specs=out_specs,
        out_shape=out_shapes,
        compiler_params=_cparams(("parallel", "parallel"), V7X_VMEM_LIMIT_BYTES),
        name="inproj",
    )(x, g1, sh1, sc1, wp, wr, gq, gk, gmq, gmkv, wuq, wukv, cg, sg, cm, sm)


SAFE_LOGIT_BOUND = 50.0


def _attn_kernel(q_ref, k_ref, kn_ref, v_ref, o_ref, s0_ref, m_ref, l_ref, acc_ref, *, tk, n_chunks, dv, n_sub):
    group, tq = q_ref.shape[1], q_ref.shape[3]
    ts = tq // n_sub
    streams = [(h, j) for h in range(group) for j in range(n_sub)]
    qts = [q_ref[0, h, :, j * ts:(j + 1) * ts] for h, j in streams]

    def k_chunk(c):
        return k_ref[0, 0, pl.ds(pl.multiple_of(c * tk, tk), tk), :]

    def finish():
        for n, (h, j) in enumerate(streams):
            l = jnp.sum(l_ref[n], axis=0, keepdims=True)
            o_ref[0, h * dv:(h + 1) * dv, j * ts:(j + 1) * ts] = (acc_ref[n] / l).astype(o_ref.dtype)

    def row_sums(p):
        return jnp.sum(p.reshape(tk // V7X_SUBLANES, V7X_SUBLANES, ts), axis=0)

    k_max = jnp.max(kn_ref[0, 0], axis=-1, keepdims=True)
    bounds = []
    for qt in qts:
        qf = qt.astype(f32)
        bounds.append(jnp.sqrt(jnp.sum(qf * qf, axis=0, keepdims=True)) * k_max)
    worst = functools.reduce(jnp.maximum, [jnp.max(u, axis=-1, keepdims=True) for u in bounds])
    safe = worst[0, 0] <= SAFE_LOGIT_BOUND

    @pl.when(safe)
    def _():
        row0 = lax.broadcasted_iota(i32, (K_PAD, ts), 0) == 0
        qes = [jnp.concatenate([qt, jnp.where(row0, -c, 0.0).astype(qt.dtype)], axis=0) for qt, c in zip(qts, bounds)]
        acc_ref[...] = jnp.zeros(acc_ref.shape, f32)
        l_ref[...] = jnp.zeros(l_ref.shape, f32)
        s0_ref[...] = _dot(k_chunk(0), qes[0])
        per_trip = math.gcd(n_chunks, CHUNKS_PER_TRIP)
        order = [(dc, n) for dc in range(per_trip) for n in range(len(streams))]

        def step(i, carry):
            c = per_trip * i
            kcs = [k_chunk(jnp.minimum(c + dc, n_chunks - 1)) for dc in range(per_trip + 1)]
            vcs = [v_ref[0, 0, c + dc] for dc in range(per_trip)]
            s_cur = s0_ref[...]
            for idx, (dc, n) in enumerate(order):
                dc2, n2 = order[idx + 1] if idx + 1 < len(order) else (per_trip, 0)
                s_next = _dot(kcs[dc2], qes[n2])
                p = jnp.exp2(s_cur)
                l_ref[n] += row_sums(p)
                acc_ref[n] += _dot(vcs[dc], p.astype(bf16))
                s_cur = s_next
            s0_ref[...] = s_cur
            return carry

        lax.fori_loop(0, n_chunks // per_trip, step, 0)
        finish()

    @pl.when(jnp.logical_not(safe))
    def _():
        m_ref[...] = jnp.full(m_ref.shape, -jnp.inf, f32)
        acc_ref[...] = jnp.zeros(acc_ref.shape, f32)
        l_ref[...] = jnp.zeros(l_ref.shape, f32)
        qzs = [jnp.concatenate([qt, jnp.zeros((K_PAD, ts), qt.dtype)], axis=0) for qt in qts]

        def step(c, carry):
            kc = k_chunk(c)
            vc = v_ref[0, 0, c]
            for n, qz in enumerate(qzs):
                s = _dot(kc, qz)
                m = m_ref[n]
                m_new = jnp.maximum(m, jnp.max(s, axis=0, keepdims=True))
                p = jnp.exp2(s - m_new)
                alpha = jnp.exp2(m - m_new)
                l_ref[n] = l_ref[n] * alpha + row_sums(p)
                acc_ref[n] = acc_ref[n] * alpha + _dot(vc, p.astype(bf16))
                m_ref[n] = m_new
            return carry

        lax.fori_loop(0, n_chunks, step, 0)
        finish()


def _attention(qt, k, kn, vt, *, group, tq, n_sub):
    b, hq, dq, s = qt.shape
    _, hk, n_chunks, v_rows, tk = vt.shape
    dv = MLA_V_DIM
    assert tq % n_sub == 0
    n_streams, ts = group * n_sub, tq // n_sub
    return pl.pallas_call(
        functools.partial(_attn_kernel, tk=tk, n_chunks=n_chunks, dv=dv, n_sub=n_sub),
        grid=(b, hk, s // tq),
        scratch_shapes=[
            pltpu.VMEM((tk, ts), f32),
            pltpu.VMEM((n_streams, 1, ts), f32),
            pltpu.VMEM((n_streams, V7X_SUBLANES, ts), f32),
            pltpu.VMEM((n_streams, v_rows, ts), f32),
        ],
        in_specs=[
            pl.BlockSpec((1, group, dq, tq), lambda bi, g, i: (bi, g, 0, i)),
            pl.BlockSpec((1, 1, s, dq + K_PAD), lambda bi, g, i: (bi, g, 0, 0)),
            pl.BlockSpec((1, 1, 1, s), lambda bi, g, i: (bi, g, 0, 0)),
            pl.BlockSpec((1, 1, n_chunks, v_rows, tk), lambda bi, g, i: (bi, g, 0, 0, 0)),
        ],
        out_specs=pl.BlockSpec((1, group * dv, tq), lambda bi, g, i: (bi, g, i)),
        out_shape=jax.ShapeDtypeStruct((b, hq * dv, s), bf16),
        compiler_params=_cparams(("parallel", "parallel", "parallel"), V7X_VMEM_LIMIT_BYTES),
        name="attention",
    )(qt, k, kn, vt)


POOL_HALO = 16


def _pool_kernel(up_ref, uc_ref, un_ref, w_ref, sc_ref, o_ref, *, seq):
    i = pl.program_id(1)
    n_t = pl.num_programs(1)
    cur = uc_ref[0]
    tp = cur.shape[0]
    prev = jnp.where(i > 0, up_ref[0], 0.0)
    nxt = jnp.where(i < n_t - 1, un_ref[0], 0.0)
    ext = jnp.concatenate([prev, cur, nxt], axis=0)
    t = i * tp + lax.broadcasted_iota(i32, (tp, 1), 0)
    lane = lax.broadcasted_iota(i32, (tp, POOL_DIM), 1)
    run, length = ext, 1
    wsum = cnt = None
    for gi, win in enumerate(POOL_WINDOWS):
        while length < win:
            run = run[:run.shape[0] - length] + run[length:]
            length *= 2
        lo = win // 2
        hi = win - 1 - lo
        mine = run[POOL_HALO - lo:POOL_HALO - lo + tp]
        n_valid = (jnp.minimum(t + hi, seq - 1) - jnp.maximum(t - lo, 0) + 1).astype(f32)
        if gi == 0:
            wsum, cnt = mine, jnp.broadcast_to(n_valid, (tp, POOL_DIM))
        else:
            in_later_group = lane >= gi * POOL_CH
            wsum = jnp.where(in_later_group, mine, wsum)
            cnt = jnp.where(in_later_group, n_valid, cnt)
    p = (wsum / cnt - cur).astype(bf16)
    o_ref[0] = (_dot(p, w_ref[...]) * sc_ref[...]).astype(o_ref.dtype)


def _pool(u, w_bd, scale, *, tp):
    b, s, c = u.shape
    n_t = s // tp
    r = tp // POOL_HALO
    return pl.pallas_call(
        functools.partial(_pool_kernel, seq=s),
        grid=(b, n_t),
        in_specs=[
            pl.BlockSpec((1, POOL_HALO, c), lambda bi, i: (bi, jnp.maximum(i * r - 1, 0), 0)),
            pl.BlockSpec((1, tp, c), lambda bi, i: (bi, i, 0)),
            pl.BlockSpec((1, POOL_HALO, c), lambda bi, i: (bi, jnp.minimum((i + 1) * r, s // POOL_HALO - 1), 0)),
            pl.BlockSpec(w_bd.shape, lambda bi, i: (0, 0)),
            pl.BlockSpec((1, c), lambda bi, i: (0, 0)),
        ],
        out_specs=pl.BlockSpec((1, tp, c), lambda bi, i: (bi, i, 0)),
        out_shape=jax.ShapeDtypeStruct((b, s, c), bf16),
        compiler_params=_cparams(("parallel", "parallel")),
        name="pool",
    )(u, u, u, w_bd, scale)


def _outproj_kernel(x_ref, yp_ref, og_ref, om_ref, wop_ref, wog_ref, wom_ref, gt_ref, g2_ref, sh_ref, sc_ref,
                    wrt_ref, x1_ref, h2_ref, aff_ref):
    tn = (((0,), (0,)), ((), ()))
    n_e = aff_ref.shape[2]
    w_hi, w_lo = _split_bf16(wrt_ref[...])
    w_hilo = jnp.concatenate([w_hi, w_lo], axis=1)
    rows = x_ref.shape[1] // OUTPROJ_ROW_SPLIT
    for r in range(OUTPROJ_ROW_SPLIT):
        sl = pl.ds(r * rows, rows)
        ls = pl.ds(r * rows, rows)
        y = _dot(yp_ref[0, sl, :], wop_ref[...])
        y = y + lax.dot_general(og_ref[0, :, ls], wog_ref[...], tn, preferred_element_type=f32)
        y = y + lax.dot_general(om_ref[0, :, ls], wom_ref[...], tn, preferred_element_type=f32)
        x1 = x_ref[0, sl, :] + gt_ref[0] * y
        x1_ref[0, sl, :] = x1
        h = x1 * lax.rsqrt(jnp.mean(x1 * x1, axis=-1, keepdims=True) + EPS) * g2_ref[...]
        h = h * (1.0 + sc_ref[0]) + sh_ref[0]
        h2_ref[0, sl, :] = h.astype(bf16)
        h_hi, h_lo = _split_bf16(h)
        two = _dot(h_hi, w_hilo)
        logits = two[:, :n_e] + two[:, n_e:] + _dot(h_lo, w_hi)
        ex = jnp.exp(logits - jnp.max(logits, axis=-1, keepdims=True))
        aff_ref[0, sl, :] = ex / jnp.sum(ex, axis=-1, keepdims=True)


def _outproj(x, ypool, og, om, wop, wog, wom, gt1, g2, sh2, sc2, w_router, *, tt):
    b, s, d = x.shape
    full = lambda shape: pl.BlockSpec(shape, lambda bi, i: (0,) * len(shape))
    vec = pl.BlockSpec((1, 1, d), lambda bi, i: (bi, 0, 0))
    n_e = w_router.shape[1]
    return pl.pallas_call(
        _outproj_kernel,
        grid=(b, s // tt),
        in_specs=[
            pl.BlockSpec((1, tt, d), lambda bi, i: (bi, i, 0)),
            pl.BlockSpec((1, tt, POOL_DIM), lambda bi, i: (bi, i, 0)),
            pl.BlockSpec((1, og.shape[1], tt), lambda bi, i: (bi, 0, i)),
            pl.BlockSpec((1, om.shape[1], tt), lambda bi, i: (bi, 0, i)),
            full(wop.shape), full(wog.shape), full(wom.shape),
            vec, full((1, d)), vec, vec, full(w_router.shape),
        ],
        out_specs=(
            pl.BlockSpec((1, tt, d), lambda bi, i: (bi, i, 0)),
            pl.BlockSpec((1, tt, d), lambda bi, i: (bi, i, 0)),
            pl.BlockSpec((1, tt, n_e), lambda bi, i: (bi, i, 0)),
        ),
        out_shape=(
            jax.ShapeDtypeStruct((b, s, d), f32),
            jax.ShapeDtypeStruct((b, s, d), bf16),
            jax.ShapeDtypeStruct((b, s, n_e), f32),
        ),
        compiler_params=_cparams(("parallel", "parallel"), V7X_VMEM_LIMIT_BYTES),
        name="outproj",
    )(x, ypool, og, om, wop, wog, wom, gt1, g2, sh2, sc2, w_router)


def _route_kernel(a_ref, posm_ref, pos_ref, *, cap):
    a = a_ref[0]
    n_e, nc, ln = a.shape
    bits = pltpu.bitcast(a, i32)

    def count(mask):
        c = jnp.sum(jnp.where(mask, 1.0, 0.0), axis=2, keepdims=True)
        return jnp.sum(c, axis=1, keepdims=True)

    thr = jnp.zeros((n_e, 1, 1), i32)
    for bit in range(30, -1, -1):
        cand = thr | (1 << bit)
        thr = jnp.where(count(bits >= cand) >= cap, cand, thr)
    gt = bits > thr
    eq = bits == thr
    need = cap - count(gt)

    r_i = lax.broadcasted_iota(i32, (ln, ln), 0)
    c_i = lax.broadcasted_iota(i32, (ln, ln), 1)
    tri_incl = jnp.where(r_i <= c_i, 1.0, 0.0).astype(bf16)
    r_c = lax.broadcasted_iota(i32, (nc, nc), 0)
    c_c = lax.broadcasted_iota(i32, (nc, nc), 1)
    tri_strict = jnp.where(c_c < r_c, 1.0, 0.0).astype(bf16)

    def excl_prefix(mask):
        x = jnp.where(mask, 1.0, 0.0)
        incl = _dot(x.astype(bf16).reshape(n_e * nc, ln), tri_incl).reshape(n_e, nc, ln)
        tot = jnp.broadcast_to(incl[:, :, ln - 1:ln], (n_e, nc, ln))
        tot_hi = tot.astype(bf16)
        offs = [_dot(tri_strict, tot_hi[e]) for e in range(n_e)]
        return jnp.stack(offs, axis=0) + incl - x

    sel = gt | (eq & (excl_prefix(eq) < need))
    pos = excl_prefix(sel).astype(i32)
    pos_ref[0] = pos
    posm_ref[0] = jnp.where(sel, pos, -1)


def _route(aff_r, *, cap):
    b, n_e, nc, ln = aff_r.shape
    spec = pl.BlockSpec((1, n_e, nc, ln), lambda bi: (bi, 0, 0, 0))
    return pl.pallas_call(
        functools.partial(_route_kernel, cap=cap),
        grid=(b,),
        in_specs=[spec],
        out_specs=(spec, spec),
        out_shape=(jax.ShapeDtypeStruct(aff_r.shape, i32), jax.ShapeDtypeStruct(aff_r.shape, i32)),
        compiler_params=_cparams(("parallel",), V7X_VMEM_LIMIT_BYTES),
        name="route",
    )(aff_r)


def _window_start(lo, width, cap):
    aligned = lax.shift_left(lax.shift_right_logical(lo, SLOT_ALIGN_LOG2), SLOT_ALIGN_LOG2)
    return pl.multiple_of(jnp.minimum(aligned, cap - width), 1 << SLOT_ALIGN_LOG2)


def _window_widths(cap):
    return min(FAST_SLOTS, cap), min(TOK_CHUNK + (1 << SLOT_ALIGN_LOG2), cap)


def _gather_kernel(offs_ref, h_ref, posm_ref, aff_ref, xe_ref, gate_ref, *, n_off, n_chunks):
    b, e = pl.program_id(0), pl.program_id(1)
    base = (b * pl.num_programs(1) + e) * n_off
    step = TOK_CHUNK // V7X_LANES
    cap = xe_ref.shape[2]
    fast_w, slow_w = _window_widths(cap)
    xe_ref[...] = jnp.zeros_like(xe_ref)
    gate_ref[...] = jnp.zeros_like(gate_ref)

    def misfit(c, bad):
        lo = offs_ref[base + c * step]
        hi = offs_ref[base + (c + 1) * step]
        return bad + (hi - _window_start(lo, fast_w, cap) > fast_w).astype(i32)

    bad = lax.fori_loop(0, n_chunks, misfit, jnp.int32(0))

    def run(width):
        slot_iota = lax.broadcasted_iota(i32, (width, TOK_CHUNK), 0)

        def chunk(c, carry):
            w = _window_start(offs_ref[base + c * step], width, cap)
            tok0 = pl.multiple_of(c * TOK_CHUNK, TOK_CHUNK)
            pr = posm_ref[0, 0, c]
            hit = pr == slot_iota + w
            rows = _dot(jnp.where(hit, 1.0, 0.0).astype(bf16), h_ref[0, pl.ds(tok0, TOK_CHUNK), :])
            xe_ref[0, 0, pl.ds(w, width), :] = xe_ref[0, 0, pl.ds(w, width), :] + rows.astype(xe_ref.dtype)
            gate_ref[0, 0, pl.ds(w, width), :] += jnp.sum(jnp.where(hit, aff_ref[0, 0, c], 0.0), axis=1, keepdims=True)
            return carry

        lax.fori_loop(0, n_chunks, chunk, 0, unroll=GATHER_UNROLL)

    @pl.when(bad == 0)
    def _():
        run(fast_w)

    @pl.when(bad != 0)
    def _():
        run(slow_w)


def _gather(offs, h2, posm_c, aff_c, *, cap):
    b, s, d = h2.shape
    n_e = posm_c.shape[1]
    n_chunks = s // TOK_CHUNK
    n_off = s // V7X_LANES + 1
    chunked = pl.BlockSpec((1, 1, n_chunks, 1, TOK_CHUNK), lambda bi, e, offs: (bi, e, 0, 0, 0))
    return pl.pallas_call(
        functools.partial(_gather_kernel, n_off=n_off, n_chunks=n_chunks),
        grid_spec=pltpu.PrefetchScalarGridSpec(
            num_scalar_prefetch=1,
            grid=(b, n_e),
            in_specs=[
                pl.BlockSpec((1, s, d), lambda bi, e, offs: (bi, 0, 0), pipeline_mode=pl.Buffered(1)),
                chunked, chunked,
            ],
            out_specs=(pl.BlockSpec((1, 1, cap, d), lambda bi, e, offs: (bi, e, 0, 0)),
                       pl.BlockSpec((1, 1, cap, 1), lambda bi, e, offs: (bi, e, 0, 0))),
        ),
        out_shape=(jax.ShapeDtypeStruct((b, n_e, cap, d), bf16), jax.ShapeDtypeStruct((b, n_e, cap, 1), f32)),
        compiler_params=_cparams(("arbitrary", "arbitrary"), V7X_VMEM_LIMIT_BYTES),
        name="gather",
    )(offs, h2, posm_c, aff_c)


def _ffn_kernel(x_ref, gate_ref, wg_ref, wu_ref, wd_ref, o_ref, acc_ref):
    @pl.when(pl.program_id(2) == 0)
    def _():
        acc_ref[...] = jnp.zeros_like(acc_ref)

    wg, wu, wd = wg_ref[0, 0].astype(bf16), wu_ref[0, 0].astype(bf16), wd_ref[0, 0].astype(bf16)
    rows = x_ref.shape[2] // FFN_ROW_SPLIT
    for r in range(FFN_ROW_SPLIT):
        sl = pl.ds(r * rows, rows)
        x = x_ref[0, 0, sl, :]
        a = _dot(x, wg)
        u = _dot(x, wu)
        hmid = (a * (1.0 / (1.0 + jnp.exp(-a))) * u).astype(bf16)
        total = acc_ref[sl, :] + _dot(hmid, wd)
        acc_ref[sl, :] = total
        o_ref[0, 0, sl, :] = (total * gate_ref[0, 0, sl, :]).astype(o_ref.dtype)


def _ffn(xe, gate, w_gate, w_up, w_down, layer):
    b, n_e, cap, d = xe.shape
    d_ff = w_gate.shape[-1]
    n_f = d_ff // FF_TILE
    return pl.pallas_call(
        _ffn_kernel,
        grid=(n_e, b, n_f),
        in_specs=[
            pl.BlockSpec((1, 1, cap, d), lambda e, bi, f: (bi, e, 0, 0)),
            pl.BlockSpec((1, 1, cap, 1), lambda e, bi, f: (bi, e, 0, 0)),
            pl.BlockSpec((1, 1, d, FF_TILE), lambda e, bi, f: (layer, e, 0, f)),
            pl.BlockSpec((1, 1, d, FF_TILE), lambda e, bi, f: (layer, e, 0, f)),
            pl.BlockSpec((1, 1, FF_TILE, d), lambda e, bi, f: (layer, e, f, 0)),
        ],
        out_specs=pl.BlockSpec((1, 1, cap, d), lambda e, bi, f: (bi, e, 0, 0)),
        out_shape=jax.ShapeDtypeStruct((b, n_e, cap, d), bf16),
        scratch_shapes=[pltpu.VMEM((cap, d), f32)],
        compiler_params=_cparams(("parallel", "parallel", "arbitrary"), V7X_VMEM_LIMIT_BYTES),
        name="expert_ffn",
    )(xe, gate, w_gate, w_up, w_down)


def _combine_kernel(offs_ref, x_ref, posm_ref, gt_ref, ye_ref, o_ref, *, n_off):
    for sub in range(x_ref.shape[1] // TOK_CHUNK):
        _combine_chunk(offs_ref, x_ref, posm_ref, gt_ref, ye_ref, o_ref, n_off=n_off, sub=sub)


def _combine_chunk(offs_ref, x_ref, posm_ref, gt_ref, ye_ref, o_ref, *, n_off, sub):
    b = pl.program_id(0)
    i = pl.program_id(2) * (x_ref.shape[1] // TOK_CHUNK) + sub
    tok = pl.ds(sub * TOK_CHUNK, TOK_CHUNK)
    n_e, cap = ye_ref.shape[1], ye_ref.shape[2]
    fast_w, slow_w = _window_widths(cap)
    step = TOK_CHUNK // V7X_LANES
    posm = posm_ref[0, tok, :]
    los = [offs_ref[(b * n_e + e) * n_off + i * step] for e in range(n_e)]
    his = [offs_ref[(b * n_e + e) * n_off + (i + 1) * step] for e in range(n_e)]
    bad = functools.reduce(
        lambda a, c: a + c, [(his[e] - _window_start(los[e], fast_w, cap) > fast_w).astype(i32) for e in range(n_e)])
    paired = 2 * fast_w == V7X_LANES and n_e % 2 == 0

    def run_stacked():
        lane = lax.broadcasted_iota(i32, (TOK_CHUNK, V7X_LANES), 1)
        ws = [_window_start(los[e], fast_w, cap) for e in range(n_e)]
        tiles, rows = [], []
        for e in range(0, n_e, 2):
            target = jnp.where(lane < fast_w, posm[:, e:e + 1] - ws[e], posm[:, e + 1:e + 2] - ws[e + 1] + fast_w)
            tiles.append(jnp.where(target == lane, 1.0, 0.0).astype(bf16))
            rows += [ye_ref[0, e, pl.ds(ws[e], fast_w), :], ye_ref[0, e + 1, pl.ds(ws[e + 1], fast_w), :]]
        total = _dot(jnp.concatenate(tiles, axis=1), jnp.concatenate(rows, axis=0))
        o_ref[0, tok, :] = x_ref[0, tok, :] + gt_ref[0] * total

    def run_per_expert(width):
        slot_iota = lax.broadcasted_iota(i32, (TOK_CHUNK, width), 1)
        total = jnp.zeros((TOK_CHUNK, o_ref.shape[2]), f32)
        for e in range(n_e):
            w = _window_start(los[e], width, cap)
            onehot = jnp.where(posm[:, e:e + 1] == slot_iota + w, 1.0, 0.0).astype(bf16)
            total = total + _dot(onehot, ye_ref[0, e, pl.ds(w, width), :])
        o_ref[0, tok, :] = x_ref[0, tok, :] + gt_ref[0] * total

    @pl.when(bad == 0)
    def _():
        run_stacked() if paired else run_per_expert(fast_w)

    @pl.when(bad != 0)
    def _():
        run_per_expert(slow_w)


def _combine(offs, x1, posm_t, gt2, ye):
    b, s, d = x1.shape
    n_e, cap = ye.shape[1], ye.shape[2]
    n_off = s // V7X_LANES + 1
    tok = math.gcd(s, COMBINE_TOK)
    return pl.pallas_call(
        functools.partial(_combine_kernel, n_off=n_off),
        grid_spec=pltpu.PrefetchScalarGridSpec(
            num_scalar_prefetch=1,
            grid=(b, d // COL_TILE, s // tok),
            in_specs=[
                pl.BlockSpec((1, tok, COL_TILE), lambda bi, j, i, offs: (bi, i, j)),
                pl.BlockSpec((1, tok, n_e), lambda bi, j, i, offs: (bi, i, 0)),
                pl.BlockSpec((1, 1, COL_TILE), lambda bi, j, i, offs: (bi, 0, j)),
                pl.BlockSpec((1, n_e, cap, COL_TILE), lambda bi, j, i, offs: (bi, 0, 0, j),
                             pipeline_mode=pl.Buffered(1)),
            ],
            out_specs=pl.BlockSpec((1, tok, COL_TILE), lambda bi, j, i, offs: (bi, i, j)),
        ),
        out_shape=jax.ShapeDtypeStruct((b, s, d), f32),
        compiler_params=_cparams(("arbitrary", "arbitrary", "arbitrary"), V7X_VMEM_LIMIT_BYTES),
        name="combine",
    )(offs, x1, posm_t, gt2, ye)


def _final_kernel(x_ref, g_ref, o_ref):
    x = x_ref[0]
    o_ref[0] = x * lax.rsqrt(jnp.mean(x * x, axis=-1, keepdims=True) + EPS) * g_ref[...]


def _final_norm(x, g, *, tt):
    b, s, d = x.shape
    return pl.pallas_call(
        _final_kernel,
        grid=(b, s // tt),
        in_specs=[pl.BlockSpec((1, tt, d), lambda bi, i: (bi, i, 0)), pl.BlockSpec((1, d), lambda bi, i: (0, 0))],
        out_specs=pl.BlockSpec((1, tt, d), lambda bi, i: (bi, i, 0)),
        out_shape=jax.ShapeDtypeStruct((b, s, d), f32),
        compiler_params=_cparams(("parallel", "parallel")),
        name="final_norm",
    )(x, g)


def _deinterleave(n):
    return np.concatenate([np.arange(0, n, 2), np.arange(1, n, 2)])


def _rope_tables_t(n, d_rot):
    n_rows = n // GRID_W
    row = jnp.repeat(jnp.arange(n_rows, dtype=f32), GRID_W)
    col = jnp.tile(jnp.arange(GRID_W, dtype=f32), n_rows)
    n_freq = d_rot // 4
    inv_freq = ROPE_THETA ** (-jnp.arange(n_freq, dtype=f32) / n_freq)
    ang = jnp.concatenate([row[:, None] * inv_freq, col[:, None] * inv_freq], axis=-1)
    return jnp.cos(ang).T, jnp.sin(ang).T


def _rest_columns():
    p64, p32 = _deinterleave(HEAD_DIM), _deinterleave(MLA_ROPE_DIM)
    cols = [OFF_GQA_Q + h * HEAD_DIM + p64 for h in range(GQA_HEADS)]
    cols += [OFF_GQA_K + h * HEAD_DIM + p64 for h in range(GQA_KV_HEADS)]
    cols += [np.arange(OFF_GQA_V, IN_DIM - MLA_ROPE_DIM), OFF_MLA_ROPE + p32]
    return np.concatenate(cols)


def _uq_columns():
    p32 = _deinterleave(MLA_ROPE_DIM)
    cols = []
    for h in range(MLA_HEADS):
        cols += [h * MLA_QK_DIM + np.arange(MLA_NOPE_DIM), h * MLA_QK_DIM + MLA_NOPE_DIM + p32]
    return np.concatenate(cols)


def _block_diag(w):
    g, c, _ = w.shape
    out = jnp.zeros((g * c, g * c), w.dtype)
    for i in range(g):
        out = out.at[i * c:(i + 1) * c, i * c:(i + 1) * c].set(w[i])
    return out


def _trunk(x, c, w_mod, b_mod, g_norm1, w_in, pool_w, pool_scale, gqa_q_gain, gqa_k_gain, mla_q_gain, mla_kv_gain,
           mla_w_uq, mla_w_ukv, w_out, g_norm2, w_router, w_gate, w_up, w_down, g_final):
    b, s, d = x.shape
    depth = w_mod.shape[0]
    cap = (EC_CAPACITY * s) // N_EXPERTS
    tt = min(TOK_TILE, s)
    tq = min(ATT_TQ, s)
    tk = min(ATT_TK, s)
    nc = s // V7X_LANES

    mod_rows = 8
    c_pad = jnp.zeros((mod_rows, d), f32).at[:b].set(c)
    mod = _modulation(c_pad, w_mod, b_mod)[:, :b].reshape(depth, b, 6, 1, d)

    cg, sg = _rope_tables_t(s, HEAD_DIM)
    cm, sm = _rope_tables_t(s, MLA_ROPE_DIM)
    p64 = _deinterleave(HEAD_DIM)
    rest_cols, uq_cols = _rest_columns(), _uq_columns()

    wp_all = w_in[:, :, :POOL_DIM].astype(bf16)
    wr_all = jnp.swapaxes(w_in[:, :, rest_cols], 1, 2).astype(bf16)
    wuq_all = jnp.swapaxes(mla_w_uq[:, :, uq_cols], 1, 2).astype(bf16)
    wukv_all = jnp.swapaxes(mla_w_ukv, 1, 2).astype(bf16)
    gq_all, gk_all = gqa_q_gain[:, p64, None], gqa_k_gain[:, p64, None]
    gmq_all, gmkv_all = mla_q_gain[:, :, None], mla_kv_gain[:, :, None]
    pw_all = jax.vmap(_block_diag)(pool_w).astype(bf16)
    wo_all = w_out.astype(bf16)
    n_g = GQA_HEADS * HEAD_DIM

    for l in range(depth):
        sh1, sc1, gt1, sh2, sc2, gt2 = (mod[l, :, k] for k in range(6))
        u, qg, kg, kng, vg, qm, km, knm, vm = _inproj(
            x, g_norm1[l][None], sh1, sc1, wp_all[l], wr_all[l], gq_all[l], gk_all[l], gmq_all[l], gmkv_all[l],
            wuq_all[l], wukv_all[l], cg, sg, cm, sm, tt=tt, tk=tk)
        og = _attention(qg, kg, kng, vg, group=GQA_GROUP, tq=min(GQA_TQ, s), n_sub=min(GQA_TQ, s) // tq)
        om = _attention(qm, km, knm, vm, group=1, tq=min(MLA_TQ, s), n_sub=min(MLA_TQ, s) // tq)
        ypool = _pool(u, pw_all[l], pool_scale[l][None], tp=tt)
        wo = wo_all[l]
        x1, h2, aff = _outproj(x, ypool, og, om, wo[:POOL_DIM], wo[POOL_DIM:POOL_DIM + n_g], wo[POOL_DIM + n_g:],
                               gt1, g_norm2[l][None], sh2, sc2, w_router[l], tt=tt)

        aff_r = aff.transpose(0, 2, 1).reshape(b, N_EXPERTS, nc, V7X_LANES)
        posm, pos = _route(aff_r, cap=cap)
        offs = jnp.concatenate([pos[..., 0], jnp.full((b, N_EXPERTS, 1), cap, i32)], axis=-1).reshape(-1)
        posm_c = posm.reshape(b, N_EXPERTS, s // TOK_CHUNK, 1, TOK_CHUNK)
        posm_t = posm.reshape(b, N_EXPERTS, s).transpose(0, 2, 1)
        aff_c = aff_r.reshape(b, N_EXPERTS, s // TOK_CHUNK, 1, TOK_CHUNK)
        xe, gate = _gather(offs, h2, posm_c, aff_c, cap=cap)
        ye = _ffn(xe, gate, w_gate, w_up, w_down, l)
        x = _combine(offs, x1, posm_t, gt2, ye)
    return _final_norm(x, g_final[None], tt=tt)


def kernel(x, c, w_mod, b_mod, g_norm1, w_in, pool_w, pool_scale, gqa_q_gain, gqa_k_gain, mla_q_gain, mla_kv_gain,
           mla_w_uq, mla_w_ukv, w_out, g_norm2, w_router, w_gate, w_up, w_down, g_final):
    return _trunk(x, c, w_mod, b_mod, g_norm1, w_in, pool_w, pool_scale, gqa_q_gain, gqa_k_gain, mla_q_gain,
                  mla_kv_gain, mla_w_uq, mla_w_ukv, w_out, g_norm2, w_router, w_gate, w_up, w_down, g_final)
```

```python
import functools
import math

import numpy as np
import jax
import jax.numpy as jnp
from jax import lax
from jax.experimental import pallas as pl
from jax.experimental.pallas import tpu as pltpu

f32, bf16, i32 = jnp.float32, jnp.bfloat16, jnp.int32

D_MODEL = 1024
DEPTH = 4
GRID_W = 64
ROPE_THETA = 10000.0
EPS = 1e-6
POOL_DIM = 256
POOL_WINDOWS = (2, 4, 8, 16)
POOL_CH = 64
HEAD_DIM = 64
GQA_HEADS = 6
GQA_KV_HEADS = 2
GQA_GROUP = 3
MLA_HEADS = 6
MLA_NOPE_DIM = 64
MLA_ROPE_DIM = 32
MLA_QK_DIM = 96
MLA_V_DIM = 64
MLA_Q_RANK = 256
MLA_KV_RANK = 256
OFF_GQA_Q = POOL_DIM
OFF_GQA_K = OFF_GQA_Q + GQA_HEADS * HEAD_DIM
OFF_GQA_V = OFF_GQA_K + GQA_KV_HEADS * HEAD_DIM
OFF_MLA_Q = OFF_GQA_V + GQA_KV_HEADS * HEAD_DIM
OFF_MLA_KV = OFF_MLA_Q + MLA_Q_RANK
OFF_MLA_ROPE = OFF_MLA_KV + MLA_KV_RANK
IN_DIM = OFF_MLA_ROPE + MLA_ROPE_DIM
REST_DIM = IN_DIM - POOL_DIM
N_EXPERTS = 16
EC_CAPACITY = 2
D_FF = 2048

R_GQ = 0
R_GK = R_GQ + GQA_HEADS * HEAD_DIM
R_GV = R_GK + GQA_KV_HEADS * HEAD_DIM
R_MQ = R_GV + GQA_KV_HEADS * HEAD_DIM
R_MKV = R_MQ + MLA_Q_RANK
R_MR = R_MKV + MLA_KV_RANK

V7X_LANES = 128
V7X_VMEM_LIMIT_BYTES = 60000 * 1024
MXU_ROWS_PER_PUSH = 128
V_ROWS = MXU_ROWS_PER_PUSH
K_PAD = 16

TOK_TILE = 512
ATT_TQ = 512
GQA_TQ = 1024
MLA_TQ = 2048
ATT_TK = 512
TOK_CHUNK = 256
FAST_SLOTS = 64
SLOT_ALIGN_LOG2 = 4
GATHER_UNROLL = 8
CHUNKS_PER_TRIP = 8
FFN_ROW_SPLIT = 2
OUTPROJ_ROW_SPLIT = 2
FF_TILE = 512
COL_TILE = 512
COMBINE_TOK = 1024
LOG2E = math.log2(math.e)


def _cparams(sem, vmem=None):
    return pltpu.CompilerParams(dimension_semantics=sem, vmem_limit_bytes=vmem)


def _split_bf16(a):
    hi = a.astype(bf16)
    lo = (a - hi.astype(f32)).astype(bf16)
    return hi, lo


def _dot(a, b):
    return jnp.dot(a, b, preferred_element_type=f32)


def _dot3(a, b):
    ah, al = _split_bf16(a)
    bh, bl = _split_bf16(b)
    return _dot(ah, bh) + _dot(ah, bl) + _dot(al, bh)


def _mod_kernel(c_ref, w_ref, b_ref, o_ref):
    c = c_ref[...]
    act = c * (1.0 / (1.0 + jnp.exp(-c)))
    o_ref[0] = _dot3(act, w_ref[0]) + b_ref[0]


def _modulation(c_pad, w_mod, b_mod):
    depth, d, six_d = w_mod.shape
    rows = c_pad.shape[0]
    return pl.pallas_call(
        _mod_kernel,
        grid=(depth, six_d // d),
        in_specs=[
            pl.BlockSpec((rows, d), lambda l, j: (0, 0)),
            pl.BlockSpec((1, d, d), lambda l, j: (l, 0, j)),
            pl.BlockSpec((1, 1, d), lambda l, j: (l, 0, j)),
        ],
        out_specs=pl.BlockSpec((1, rows, d), lambda l, j: (l, 0, j)),
        out_shape=jax.ShapeDtypeStruct((depth, rows, six_d), f32),
        compiler_params=_cparams(("parallel", "parallel")),
        name="modulation",
    )(c_pad, w_mod, b_mod.reshape(depth, 1, six_d))


def _rms_rows(z, gain_col):
    r = lax.rsqrt(jnp.mean(z * z, axis=0, keepdims=True) + EPS)
    return z * r * gain_col


def _rope_rows(z, cos, sin):
    half = z.shape[0] // 2
    x1, x2 = z[:half], z[half:]
    return jnp.concatenate([x1 * cos - x2 * sin, x1 * sin + x2 * cos], axis=0)


def _inproj_kernel(x_ref, g_ref, sh_ref, sc_ref, wp_ref, wr_ref, gq_ref, gk_ref, gmq_ref, gmkv_ref,
                   wuq_ref, wukv_ref, cg_ref, sg_ref, cm_ref, sm_ref,
                   u_ref, qg_ref, kg_ref, kng_ref, vg_ref, qm_ref, km_ref, knm_ref, vm_ref, *, tk):
    x = x_ref[0]
    tt = x.shape[0]
    h = x * lax.rsqrt(jnp.mean(x * x, axis=-1, keepdims=True) + EPS) * g_ref[...]
    h = h * (1.0 + sc_ref[0]) + sh_ref[0]
    hb = h.astype(bf16)
    u_ref[0] = _dot(hb, wp_ref[...])
    zt = lax.dot_general(wr_ref[...], hb, (((1,), (1,)), ((), ())), preferred_element_type=f32)

    cg, sg, cm, sm = cg_ref[...], sg_ref[...], cm_ref[...], sm_ref[...]
    n_sub = tt // tk

    def put_k(ref, norm_ref, head, kt):
        kb = kt.astype(bf16)
        kf = kb.astype(f32)
        norm_ref[0, head] = jnp.sqrt(jnp.sum(kf * kf, axis=0, keepdims=True))
        pad = jnp.where(lax.broadcasted_iota(i32, (K_PAD, tt), 0) == 0, 1.0, 0.0)
        ke = jnp.concatenate([kt, pad], axis=0)
        ref[0, head] = ke.T.astype(bf16)

    def put_v(ref, head, vt):
        ve = jnp.concatenate([vt, jnp.ones((V_ROWS - MLA_V_DIM, tt), f32)], axis=0).astype(bf16)
        for j in range(n_sub):
            ref[0, head, j] = ve[:, j * tk:(j + 1) * tk]

    gq = gq_ref[...] * (HEAD_DIM ** -0.5 * LOG2E)
    gk = gk_ref[...]
    for hd in range(GQA_HEADS):
        q = _rms_rows(zt[R_GQ + hd * HEAD_DIM:R_GQ + (hd + 1) * HEAD_DIM], gq)
        qg_ref[0, hd] = _rope_rows(q, cg, sg).astype(bf16)
    for hk in range(GQA_KV_HEADS):
        k = _rms_rows(zt[R_GK + hk * HEAD_DIM:R_GK + (hk + 1) * HEAD_DIM], gk)
        put_k(kg_ref, kng_ref, hk, _rope_rows(k, cg, sg))
        put_v(vg_ref, hk, zt[R_GV + hk * HEAD_DIM:R_GV + (hk + 1) * HEAD_DIM])

    cq = _rms_rows(zt[R_MQ:R_MQ + MLA_Q_RANK], gmq_ref[...]).astype(bf16)
    qm = _dot(wuq_ref[...], cq) * (MLA_QK_DIM ** -0.5 * LOG2E)
    ckv = _rms_rows(zt[R_MKV:R_MKV + MLA_KV_RANK], gmkv_ref[...]).astype(bf16)
    kv = _dot(wukv_ref[...], ckv)
    k_rope = _rope_rows(zt[R_MR:R_MR + MLA_ROPE_DIM], cm, sm)
    for hd in range(MLA_HEADS):
        qh = qm[hd * MLA_QK_DIM:(hd + 1) * MLA_QK_DIM]
        qr = _rope_rows(qh[MLA_NOPE_DIM:], cm, sm)
        qm_ref[0, hd] = jnp.concatenate([qh[:MLA_NOPE_DIM], qr], axis=0).astype(bf16)
        kvh = kv[hd * (MLA_NOPE_DIM + MLA_V_DIM):(hd + 1) * (MLA_NOPE_DIM + MLA_V_DIM)]
        kh = jnp.concatenate([kvh[:MLA_NOPE_DIM], k_rope], axis=0)
        put_k(km_ref, knm_ref, hd, kh)
        put_v(vm_ref, hd, kvh[MLA_NOPE_DIM:])


def _inproj(x, g1, sh1, sc1, wp, wr, gq, gk, gmq, gmkv, wuq, wukv, cg, sg, cm, sm, *, tt, tk):
    b, s, d = x.shape
    n_t = s // tt
    n_sub = tt // tk
    full = lambda shape: pl.BlockSpec(shape, lambda bi, i: (0,) * len(shape))
    vec = pl.BlockSpec((1, 1, d), lambda bi, i: (bi, 0, 0))
    rope_g = pl.BlockSpec((HEAD_DIM // 2, tt), lambda bi, i: (0, i))
    rope_m = pl.BlockSpec((MLA_ROPE_DIM // 2, tt), lambda bi, i: (0, i))
    out_shapes = (
        jax.ShapeDtypeStruct((b, s, POOL_DIM), f32),
        jax.ShapeDtypeStruct((b, GQA_HEADS, HEAD_DIM, s), bf16),
        jax.ShapeDtypeStruct((b, GQA_KV_HEADS, s, HEAD_DIM + K_PAD), bf16),
        jax.ShapeDtypeStruct((b, GQA_KV_HEADS, 1, s), f32),
        jax.ShapeDtypeStruct((b, GQA_KV_HEADS, s // tk, V_ROWS, tk), bf16),
        jax.ShapeDtypeStruct((b, MLA_HEADS, MLA_QK_DIM, s), bf16),
        jax.ShapeDtypeStruct((b, MLA_HEADS, s, MLA_QK_DIM + K_PAD), bf16),
        jax.ShapeDtypeStruct((b, MLA_HEADS, 1, s), f32),
        jax.ShapeDtypeStruct((b, MLA_HEADS, s // tk, V_ROWS, tk), bf16),
    )
    out_specs = (
        pl.BlockSpec((1, tt, POOL_DIM), lambda bi, i: (bi, i, 0)),
        pl.BlockSpec((1, GQA_HEADS, HEAD_DIM, tt), lambda bi, i: (bi, 0, 0, i)),
        pl.BlockSpec((1, GQA_KV_HEADS, tt, HEAD_DIM + K_PAD), lambda bi, i: (bi, 0, i, 0)),
        pl.BlockSpec((1, GQA_KV_HEADS, 1, tt), lambda bi, i: (bi, 0, 0, i)),
        pl.BlockSpec((1, GQA_KV_HEADS, n_sub, V_ROWS, tk), lambda bi, i: (bi, 0, i, 0, 0)),
        pl.BlockSpec((1, MLA_HEADS, MLA_QK_DIM, tt), lambda bi, i: (bi, 0, 0, i)),
        pl.BlockSpec((1, MLA_HEADS, tt, MLA_QK_DIM + K_PAD), lambda bi, i: (bi, 0, i, 0)),
        pl.BlockSpec((1, MLA_HEADS, 1, tt), lambda bi, i: (bi, 0, 0, i)),
        pl.BlockSpec((1, MLA_HEADS, n_sub, V_ROWS, tk), lambda bi, i: (bi, 0, i, 0, 0)),
    )
    return pl.pallas_call(
        functools.partial(_inproj_kernel, tk=tk),
        grid=(b, n_t),
        in_specs=[
            pl.BlockSpec((1, tt, d), lambda bi, i: (bi, i, 0)),
            full((1, d)), vec, vec,
            full(wp.shape), full(wr.shape), full(gq.shape), full(gk.shape), full(gmq.shape), full(gmkv.shape),
            full(wuq.shape), full(wukv.shape), rope_g, rope_g, rope_m, rope_m,
        ],
        out_specs=out_specs,
        out_shape=out_shapes,
        compiler_params=_cparams(("parallel", "parallel"), V7X_VMEM_LIMIT_BYTES),
        name="inproj",
    )(x, g1, sh1, sc1, wp, wr, gq, gk, gmq, gmkv, wuq, wukv, cg, sg, cm, sm)


SAFE_LOGIT_BOUND = 50.0


def _attn_kernel(q_ref, k_ref, kn_ref, v_ref, o_ref, s0_ref, m_ref, acc_ref, *, tk, n_chunks, dv, n_sub):
    group, tq = q_ref.shape[1], q_ref.shape[3]
    ts = tq // n_sub
    streams = [(h, j) for h in range(group) for j in range(n_sub)]
    qts = [q_ref[0, h, :, j * ts:(j + 1) * ts] for h, j in streams]

    def k_chunk(c):
        return k_ref[0, 0, pl.ds(pl.multiple_of(c * tk, tk), tk), :]

    def finish():
        for n, (h, j) in enumerate(streams):
            acc = acc_ref[n]
            o_ref[0, h * dv:(h + 1) * dv, j * ts:(j + 1) * ts] = (acc[:dv] / acc[dv:dv + 1]).astype(o_ref.dtype)

    k_max = jnp.max(kn_ref[0, 0], axis=-1, keepdims=True)
    bounds = []
    for qt in qts:
        qf = qt.astype(f32)
        bounds.append(jnp.sqrt(jnp.sum(qf * qf, axis=0, keepdims=True)) * k_max)
    worst = functools.reduce(jnp.maximum, [jnp.max(u, axis=-1, keepdims=True) for u in bounds])
    safe = worst[0, 0] <= SAFE_LOGIT_BOUND

    @pl.when(safe)
    def _():
        row0 = lax.broadcasted_iota(i32, (K_PAD, ts), 0) == 0
        qes = [jnp.concatenate([qt, jnp.where(row0, -c, 0.0).astype(qt.dtype)], axis=0) for qt, c in zip(qts, bounds)]
        acc_ref[...] = jnp.zeros(acc_ref.shape, f32)
        s0_ref[...] = _dot(k_chunk(0), qes[0])
        per_trip = math.gcd(n_chunks, CHUNKS_PER_TRIP)
        order = [(dc, n) for dc in range(per_trip) for n in range(len(streams))]

        def step(i, carry):
            c = per_trip * i
            kcs = [k_chunk(jnp.minimum(c + dc, n_chunks - 1)) for dc in range(per_trip + 1)]
            vcs = [v_ref[0, 0, c + dc] for dc in range(per_trip)]
            s_cur = s0_ref[...]
            for idx, (dc, n) in enumerate(order):
                dc2, n2 = order[idx + 1] if idx + 1 < len(order) else (per_trip, 0)
                s_next = _dot(kcs[dc2], qes[n2])
                acc_ref[n] += _dot(vcs[dc], jnp.exp2(s_cur).astype(bf16))
                s_cur = s_next
            s0_ref[...] = s_cur
            return carry

        lax.fori_loop(0, n_chunks // per_trip, step, 0)
        finish()

    @pl.when(jnp.logical_not(safe))
    def _():
        m_ref[...] = jnp.full(m_ref.shape, -jnp.inf, f32)
        acc_ref[...] = jnp.zeros(acc_ref.shape, f32)
        qzs =[jnp.concatenate([qt, jnp.zeros((K_PAD, ts), qt.dtype)], axis=0) for qt in qts]

        def step(c, carry):
            kc = k_chunk(c)
            vc = v_ref[0, 0, c]
            for n, qz in enumerate(qzs):
                s = _dot(kc, qz)
                m = m_ref[n]
                m_new = jnp.maximum(m, jnp.max(s, axis=0, keepdims=True))
                p = jnp.exp2(s - m_new).astype(bf16)
                acc_ref[n] = acc_ref[n] * jnp.exp2(m - m_new) + _dot(vc, p)
                m_ref[n] = m_new
            return carry

        lax.fori_loop(0, n_chunks, step, 0)
        finish()


def _attention(qt, k, kn, vt, *, group, tq, n_sub):
    b, hq, dq, s = qt.shape
    _, hk, n_chunks, v_rows, tk = vt.shape
    dv = MLA_V_DIM
    assert tq % n_sub == 0
    n_streams, ts = group * n_sub, tq // n_sub
    return pl.pallas_call(
        functools.partial(_attn_kernel, tk=tk, n_chunks=n_chunks, dv=dv, n_sub=n_sub),
        grid=(b, hk, s // tq),
        scratch_shapes=[
            pltpu.VMEM((tk, ts), f32),
            pltpu.VMEM((n_streams, 1, ts), f32),
            pltpu.VMEM((n_streams, v_rows, ts), f32),
        ],
        in_specs=[
            pl.BlockSpec((1, group, dq, tq), lambda bi, g, i: (bi, g, 0, i)),
            pl.BlockSpec((1, 1, s, dq + K_PAD), lambda bi, g, i: (bi, g, 0, 0)),
            pl.BlockSpec((1, 1, 1, s), lambda bi, g, i: (bi, g, 0, 0)),
            pl.BlockSpec((1, 1, n_chunks, v_rows, tk), lambda bi, g, i: (bi, g, 0, 0, 0)),
        ],
        out_specs=pl.BlockSpec((1, group * dv, tq), lambda bi, g, i: (bi, g, i)),
        out_shape=jax.ShapeDtypeStruct((b, hq * dv, s), bf16),
        compiler_params=_cparams(("parallel", "parallel", "parallel"), V7X_VMEM_LIMIT_BYTES),
        name="attention",
    )(qt, k, kn, vt)


POOL_HALO = 16


def _pool_kernel(up_ref, uc_ref, un_ref, w_ref, sc_ref, o_ref, *, seq):
    i = pl.program_id(1)
    n_t = pl.num_programs(1)
    cur = uc_ref[0]
    tp = cur.shape[0]
    prev = jnp.where(i > 0, up_ref[0], 0.0)
    nxt = jnp.where(i < n_t - 1, un_ref[0], 0.0)
    ext = jnp.concatenate([prev, cur, nxt], axis=0)
    t = i * tp + lax.broadcasted_iota(i32, (tp, 1), 0)
    lane = lax.broadcasted_iota(i32, (tp, POOL_DIM), 1)
    run, length = ext, 1
    wsum = cnt = None
    for gi, win in enumerate(POOL_WINDOWS):
        while length < win:
            run = run[:run.shape[0] - length] + run[length:]
            length *= 2
        lo = win // 2
        hi = win - 1 - lo
        mine = run[POOL_HALO - lo:POOL_HALO - lo + tp]
        n_valid = (jnp.minimum(t + hi, seq - 1) - jnp.maximum(t - lo, 0) + 1).astype(f32)
        if gi == 0:
            wsum, cnt = mine, jnp.broadcast_to(n_valid, (tp, POOL_DIM))
        else:
            in_later_group = lane >= gi * POOL_CH
            wsum = jnp.where(in_later_group, mine, wsum)
            cnt = jnp.where(in_later_group, n_valid, cnt)
    p = (wsum / cnt - cur).astype(bf16)
    o_ref[0] = (_dot(p, w_ref[...]) * sc_ref[...]).astype(o_ref.dtype)


def _pool(u, w_bd, scale, *, tp):
    b, s, c = u.shape
    n_t = s // tp
    r = tp // POOL_HALO
    return pl.pallas_call(
        functools.partial(_pool_kernel, seq=s),
        grid=(b, n_t),
        in_specs=[
            pl.BlockSpec((1, POOL_HALO, c), lambda bi, i: (bi, jnp.maximum(i * r - 1, 0), 0)),
            pl.BlockSpec((1, tp, c), lambda bi, i: (bi, i, 0)),
            pl.BlockSpec((1, POOL_HALO, c), lambda bi, i: (bi, jnp.minimum((i + 1) * r, s // POOL_HALO - 1), 0)),
            pl.BlockSpec(w_bd.shape, lambda bi, i: (0, 0)),
            pl.BlockSpec((1, c), lambda bi, i: (0, 0)),
        ],
        out_specs=pl.BlockSpec((1, tp, c), lambda bi, i: (bi, i, 0)),
        out_shape=jax.ShapeDtypeStruct((b, s, c), bf16),
        compiler_params=_cparams(("parallel", "parallel")),
        name="pool",
    )(u, u, u, w_bd, scale)


def _outproj_kernel(x_ref, yp_ref, og_ref, om_ref, wop_ref, wog_ref, wom_ref, gt_ref, g2_ref, sh_ref, sc_ref,
                    wrt_ref, x1_ref, h2_ref, aff_ref):
    tn = (((0,), (0,)), ((), ()))
    n_e = aff_ref.shape[2]
    w_hi, w_lo = _split_bf16(wrt_ref[...])
    w_hilo = jnp.concatenate([w_hi, w_lo], axis=1)
    rows = x_ref.shape[1] // OUTPROJ_ROW_SPLIT
    for r in range(OUTPROJ_ROW_SPLIT):
        sl = pl.ds(r * rows, rows)
        ls = pl.ds(r * rows, rows)
        y = _dot(yp_ref[0, sl, :], wop_ref[...])
        y = y + lax.dot_general(og_ref[0, :, ls], wog_ref[...], tn, preferred_element_type=f32)
        y = y + lax.dot_general(om_ref[0, :, ls], wom_ref[...], tn, preferred_element_type=f32)
        x1 = x_ref[0, sl, :] + gt_ref[0] * y
        x1_ref[0, sl, :] = x1
        h = x1 * lax.rsqrt(jnp.mean(x1 * x1, axis=-1, keepdims=True) + EPS) * g2_ref[...]
        h = h * (1.0 + sc_ref[0]) + sh_ref[0]
        h2_ref[0, sl, :] = h.astype(bf16)
        h_hi, h_lo = _split_bf16(h)
        two = _dot(h_hi, w_hilo)
        logits = two[:, :n_e] + two[:, n_e:] + _dot(h_lo, w_hi)
        ex = jnp.exp(logits - jnp.max(logits, axis=-1, keepdims=True))
        aff_ref[0, sl, :] = ex / jnp.sum(ex, axis=-1, keepdims=True)


def _outproj(x, ypool, og, om, wop, wog, wom, gt1, g2, sh2, sc2, w_router, *, tt):
    b, s, d = x.shape
    full = lambda shape: pl.BlockSpec(shape, lambda bi, i: (0,) * len(shape))
    vec = pl.BlockSpec((1, 1, d), lambda bi, i: (bi, 0, 0))
    n_e = w_router.shape[1]
    return pl.pallas_call(
        _outproj_kernel,
        grid=(b, s // tt),
        in_specs=[
            pl.BlockSpec((1, tt, d), lambda bi, i: (bi, i, 0)),
            pl.BlockSpec((1, tt, POOL_DIM), lambda bi, i: (bi, i, 0)),
            pl.BlockSpec((1, og.shape[1], tt), lambda bi, i: (bi, 0, i)),
            pl.BlockSpec((1, om.shape[1], tt), lambda bi, i: (bi, 0, i)),
            full(wop.shape), full(wog.shape), full(wom.shape),
            vec, full((1, d)), vec, vec, full(w_router.shape),
        ],
        out_specs=(
            pl.BlockSpec((1, tt, d), lambda bi, i: (bi, i, 0)),
            pl.BlockSpec((1, tt, d), lambda bi, i: (bi, i, 0)),
            pl.BlockSpec((1, tt, n_e), lambda bi, i: (bi, i, 0)),
        ),
        out_shape=(
            jax.ShapeDtypeStruct((b, s, d), f32),
            jax.ShapeDtypeStruct((b, s, d), bf16),
            jax.ShapeDtypeStruct((b, s, n_e), f32),
        ),
        compiler_params=_cparams(("parallel", "parallel"), V7X_VMEM_LIMIT_BYTES),
        name="outproj",
    )(x, ypool, og, om, wop, wog, wom, gt1, g2, sh2, sc2, w_router)


def _route_kernel(a_ref, posm_ref, pos_ref, *, cap):
    a = a_ref[0]
    n_e, nc, ln = a.shape
    bits = pltpu.bitcast(a, i32)

    def count(mask):
        c = jnp.sum(jnp.where(mask, 1.0, 0.0), axis=2, keepdims=True)
        return jnp.sum(c, axis=1, keepdims=True)

    thr = jnp.zeros((n_e, 1, 1), i32)
    for bit in range(30, -1, -1):
        cand = thr | (1 << bit)
        thr = jnp.where(count(bits >= cand) >= cap, cand, thr)
    gt = bits > thr
    eq = bits == thr
    need = cap - count(gt)

    r_i = lax.broadcasted_iota(i32, (ln, ln), 0)
    c_i = lax.broadcasted_iota(i32, (ln, ln), 1)
    tri_incl = jnp.where(r_i <= c_i, 1.0, 0.0).astype(bf16)
    r_c = lax.broadcasted_iota(i32, (nc, nc), 0)
    c_c = lax.broadcasted_iota(i32, (nc, nc), 1)
    tri_strict = jnp.where(c_c < r_c, 1.0, 0.0).astype(bf16)

    def excl_prefix(mask):
        x = jnp.where(mask, 1.0, 0.0)
        incl = _dot(x.astype(bf16).reshape(n_e * nc, ln), tri_incl).reshape(n_e, nc, ln)
        tot = jnp.broadcast_to(incl[:, :, ln - 1:ln], (n_e, nc, ln))
        tot_hi = tot.astype(bf16)
        offs = [_dot(tri_strict, tot_hi[e]) for e in range(n_e)]
        return jnp.stack(offs, axis=0) + incl - x

    sel = gt | (eq & (excl_prefix(eq) < need))
    pos = excl_prefix(sel).astype(i32)
    pos_ref[0] = pos
    posm_ref[0] = jnp.where(sel, pos, -1)


def _route(aff_r, *, cap):
    b, n_e, nc, ln = aff_r.shape
    spec = pl.BlockSpec((1, n_e, nc, ln), lambda bi: (bi, 0, 0, 0))
    return pl.pallas_call(
        functools.partial(_route_kernel, cap=cap),
        grid=(b,),
        in_specs=[spec],
        out_specs=(spec, spec),
        out_shape=(jax.ShapeDtypeStruct(aff_r.shape, i32), jax.ShapeDtypeStruct(aff_r.shape, i32)),
        compiler_params=_cparams(("parallel",), V7X_VMEM_LIMIT_BYTES),
        name="route",
    )(aff_r)


def _window_start(lo, width, cap):
    aligned = lax.shift_left(lax.shift_right_logical(lo, SLOT_ALIGN_LOG2), SLOT_ALIGN_LOG2)
    return pl.multiple_of(jnp.minimum(aligned, cap - width), 1 << SLOT_ALIGN_LOG2)


def _window_widths(cap, usual=FAST_SLOTS):
    return min(usual, cap), min(TOK_CHUNK + (1 << SLOT_ALIGN_LOG2), cap)


def _gather_kernel(offs_ref, h_ref, posm_ref, aff_ref, xe_ref, gate_ref, *, n_off, n_chunks):
    b, e = pl.program_id(0), pl.program_id(1)
    base = (b * pl.num_programs(1) + e) * n_off
    step = TOK_CHUNK // V7X_LANES
    cap = xe_ref.shape[2]
    fast_w, slow_w = _window_widths(cap, usual=MXU_ROWS_PER_PUSH)
    xe_ref[...] = jnp.zeros_like(xe_ref)
    gate_ref[...] = jnp.zeros_like(gate_ref)

    def misfit(c, bad):
        lo = offs_ref[base + c * step]
        hi = offs_ref[base + (c + 1) * step]
        return bad + (hi - _window_start(lo, fast_w, cap) > fast_w).astype(i32)

    bad = lax.fori_loop(0, n_chunks, misfit, jnp.int32(0))

    def run(width):
        slot_iota = lax.broadcasted_iota(i32, (width, TOK_CHUNK), 0)

        def chunk(c, carry):
            w = _window_start(offs_ref[base + c * step], width, cap)
            tok0 = pl.multiple_of(c * TOK_CHUNK, TOK_CHUNK)
            pr = posm_ref[0, 0, c]
            hit = pr == slot_iota + w
            rows = _dot(jnp.where(hit, 1.0, 0.0).astype(bf16), h_ref[0, pl.ds(tok0, TOK_CHUNK), :])
            xe_ref[0, 0, pl.ds(w, width), :] = xe_ref[0, 0, pl.ds(w, width), :] + rows.astype(xe_ref.dtype)
            gate_ref[0, 0, pl.ds(w, width), :] += jnp.sum(jnp.where(hit, aff_ref[0, 0, c], 0.0), axis=1, keepdims=True)
            return carry

        lax.fori_loop(0, n_chunks, chunk, 0, unroll=GATHER_UNROLL)

    @pl.when(bad == 0)
    def _():
        run(fast_w)

    @pl.when(bad != 0)
    def _():
        run(slow_w)


def _gather(offs, h2, posm_c, aff_c, *, cap):
    b, s, d = h2.shape
    n_e = posm_c.shape[1]
    n_chunks = s // TOK_CHUNK
    n_off = s // V7X_LANES + 1
    chunked = pl.BlockSpec((1, 1, n_chunks, 1, TOK_CHUNK), lambda bi, e, offs: (bi, e, 0, 0, 0))
    return pl.pallas_call(
        functools.partial(_gather_kernel, n_off=n_off, n_chunks=n_chunks),
        grid_spec=pltpu.PrefetchScalarGridSpec(
            num_scalar_prefetch=1,
            grid=(b, n_e),
            in_specs=[
                pl.BlockSpec((1, s, d), lambda bi, e, offs: (bi, 0, 0), pipeline_mode=pl.Buffered(1)),
                chunked, chunked,
            ],
            out_specs=(pl.BlockSpec((1, 1, cap, d), lambda bi, e, offs: (bi, e, 0, 0)),
                       pl.BlockSpec((1, 1, cap, 1), lambda bi, e, offs: (bi, e, 0, 0))),
        ),
        out_shape=(jax.ShapeDtypeStruct((b, n_e, cap, d), bf16), jax.ShapeDtypeStruct((b, n_e, cap, 1), f32)),
        compiler_params=_cparams(("arbitrary", "arbitrary"), V7X_VMEM_LIMIT_BYTES),
        name="gather",
    )(offs, h2, posm_c, aff_c)


def _ffn_kernel(x_ref, gate_ref, wg_ref, wu_ref, wd_ref, o_ref, acc_ref):
    @pl.when(pl.program_id(2) == 0)
    def _():
        acc_ref[...] = jnp.zeros_like(acc_ref)

    wg, wu, wd = wg_ref[0, 0].astype(bf16), wu_ref[0, 0].astype(bf16), wd_ref[0, 0].astype(bf16)
    rows = x_ref.shape[2] // FFN_ROW_SPLIT
    for r in range(FFN_ROW_SPLIT):
        sl = pl.ds(r * rows, rows)
        x = x_ref[0, 0, sl, :]
        a = _dot(x, wg)
        u = _dot(x, wu)
        hmid = (a * (1.0 / (1.0 + jnp.exp(-a))) * u).astype(bf16)
        total = acc_ref[sl, :] + _dot(hmid, wd)
        acc_ref[sl, :] = total
        o_ref[0, 0, sl, :] = (total * gate_ref[0, 0, sl, :]).astype(o_ref.dtype)


def _ffn(xe, gate, w_gate, w_up, w_down, layer):
    b, n_e, cap, d = xe.shape
    d_ff = w_gate.shape[-1]
    n_f = d_ff // FF_TILE
    return pl.pallas_call(
        _ffn_kernel,
        grid=(n_e, b, n_f),
        in_specs=[
            pl.BlockSpec((1, 1, cap, d), lambda e, bi, f: (bi, e, 0, 0)),
            pl.BlockSpec((1, 1, cap, 1), lambda e, bi, f: (bi, e, 0, 0)),
            pl.BlockSpec((1, 1, d, FF_TILE), lambda e, bi, f: (layer, e, 0, f)),
            pl.BlockSpec((1, 1, d, FF_TILE), lambda e, bi, f: (layer, e, 0, f)),
            pl.BlockSpec((1, 1, FF_TILE, d), lambda e, bi, f: (layer, e, f, 0)),
        ],
        out_specs=pl.BlockSpec((1, 1, cap, d), lambda e, bi, f: (bi, e, 0, 0)),
        out_shape=jax.ShapeDtypeStruct((b, n_e, cap, d), bf16),
        scratch_shapes=[pltpu.VMEM((cap, d), f32)],
        compiler_params=_cparams(("parallel", "parallel", "arbitrary"), V7X_VMEM_LIMIT_BYTES),
        name="expert_ffn",
    )(xe, gate, w_gate, w_up, w_down)


def _combine_kernel(offs_ref, x_ref, posm_ref, gt_ref, ye_ref, o_ref, *, n_off):
    for sub in range(x_ref.shape[1] // TOK_CHUNK):
        _combine_chunk(offs_ref, x_ref, posm_ref, gt_ref, ye_ref, o_ref, n_off=n_off, sub=sub)


def _combine_chunk(offs_ref, x_ref, posm_ref, gt_ref, ye_ref, o_ref, *, n_off, sub):
    b = pl.program_id(0)
    i = pl.program_id(2) * (x_ref.shape[1] // TOK_CHUNK) + sub
    tok = pl.ds(sub * TOK_CHUNK, TOK_CHUNK)
    n_e, cap = ye_ref.shape[1], ye_ref.shape[2]
    fast_w, slow_w = _window_widths(cap)
    step = TOK_CHUNK // V7X_LANES
    posm = posm_ref[0, tok, :]
    los = [offs_ref[(b * n_e + e) * n_off + i * step] for e in range(n_e)]
    his = [offs_ref[(b * n_e + e) * n_off + (i + 1) * step] for e in range(n_e)]
    bad = functools.reduce(
        lambda a, c: a + c, [(his[e] - _window_start(los[e], fast_w, cap) > fast_w).astype(i32) for e in range(n_e)])
    paired = 2 * fast_w == V7X_LANES and n_e % 2 == 0

    def run_stacked():
        lane = lax.broadcasted_iota(i32, (TOK_CHUNK, V7X_LANES), 1)
        ws = [_window_start(los[e], fast_w, cap) for e in range(n_e)]
        tiles, rows = [], []
        for e in range(0, n_e, 2):
            target = jnp.where(lane < fast_w, posm[:, e:e + 1] - ws[e], posm[:, e + 1:e + 2] - ws[e + 1] + fast_w)
            tiles.append(jnp.where(target == lane, 1.0, 0.0).astype(bf16))
            rows += [ye_ref[0, e, pl.ds(ws[e], fast_w), :], ye_ref[0, e + 1, pl.ds(ws[e + 1], fast_w), :]]
        total = _dot(jnp.concatenate(tiles, axis=1), jnp.concatenate(rows, axis=0))
        o_ref[0, tok, :] = x_ref[0, tok, :] + gt_ref[0] * total

    def run_per_expert(width):
        slot_iota = lax.broadcasted_iota(i32, (TOK_CHUNK, width), 1)
        total = jnp.zeros((TOK_CHUNK, o_ref.shape[2]), f32)
        for e in range(n_e):
            w = _window_start(los[e], width, cap)
            onehot = jnp.where(posm[:, e:e + 1] == slot_iota + w, 1.0, 0.0).astype(bf16)
            total = total + _dot(onehot, ye_ref[0, e, pl.ds(w, width), :])
        o_ref[0, tok, :] = x_ref[0, tok, :] + gt_ref[0] * total

    @pl.when(bad == 0)
    def _():
        run_stacked() if paired else run_per_expert(fast_w)

    @pl.when(bad != 0)
    def _():
        run_per_expert(slow_w)


def _combine(offs, x1, posm_t, gt2, ye):
    b, s, d = x1.shape
    n_e, cap = ye.shape[1], ye.shape[2]
    n_off = s // V7X_LANES + 1
    tok = math.gcd(s, COMBINE_TOK)
    return pl.pallas_call(
        functools.partial(_combine_kernel, n_off=n_off),
        grid_spec=pltpu.PrefetchScalarGridSpec(
            num_scalar_prefetch=1,
            grid=(b, d // COL_TILE, s // tok),
            in_specs=[
                pl.BlockSpec((1, tok, COL_TILE), lambda bi, j, i, offs: (bi, i, j)),
                pl.BlockSpec((1, tok, n_e), lambda bi, j, i, offs: (bi, i, 0)),
                pl.BlockSpec((1, 1, COL_TILE), lambda bi, j, i, offs: (bi, 0, j)),
                pl.BlockSpec((1, n_e, cap, COL_TILE), lambda bi, j, i, offs: (bi, 0, 0, j),
                             pipeline_mode=pl.Buffered(1)),
            ],
            out_specs=pl.BlockSpec((1, tok, COL_TILE), lambda bi, j, i, offs: (bi, i, j)),
        ),
        out_shape=jax.ShapeDtypeStruct((b, s, d), f32),
        compiler_params=_cparams(("arbitrary", "arbitrary", "arbitrary"), V7X_VMEM_LIMIT_BYTES),
        name="combine",
    )(offs, x1, posm_t, gt2, ye)


def _final_kernel(x_ref, g_ref, o_ref):
    x = x_ref[0]
    o_ref[0] = x * lax.rsqrt(jnp.mean(x * x, axis=-1, keepdims=True) + EPS) * g_ref[...]


def _final_norm(x, g, *, tt):
    b, s, d = x.shape
    return pl.pallas_call(
        _final_kernel,
        grid=(b, s // tt),
        in_specs=[pl.BlockSpec((1, tt, d), lambda bi, i: (bi, i, 0)), pl.BlockSpec((1, d), lambda bi, i: (0, 0))],
        out_specs=pl.BlockSpec((1, tt, d), lambda bi, i: (bi, i, 0)),
        out_shape=jax.ShapeDtypeStruct((b, s, d), f32),
        compiler_params=_cparams(("parallel", "parallel")),
        name="final_norm",
    )(x, g)


def _deinterleave(n):
    return np.concatenate([np.arange(0, n, 2), np.arange(1, n, 2)])


def _rope_tables_t(n, d_rot):
    n_rows = n // GRID_W
    row = jnp.repeat(jnp.arange(n_rows, dtype=f32), GRID_W)
    col = jnp.tile(jnp.arange(GRID_W, dtype=f32), n_rows)
    n_freq = d_rot // 4
    inv_freq = ROPE_THETA ** (-jnp.arange(n_freq, dtype=f32) / n_freq)
    ang = jnp.concatenate([row[:, None] * inv_freq, col[:, None] * inv_freq], axis=-1)
    return jnp.cos(ang).T, jnp.sin(ang).T


def _rest_columns():
    p64, p32 = _deinterleave(HEAD_DIM), _deinterleave(MLA_ROPE_DIM)
    cols = [OFF_GQA_Q + h * HEAD_DIM + p64 for h in range(GQA_HEADS)]
    cols += [OFF_GQA_K + h * HEAD_DIM + p64 for h in range(GQA_KV_HEADS)]
    cols += [np.arange(OFF_GQA_V, IN_DIM - MLA_ROPE_DIM), OFF_MLA_ROPE + p32]
    return np.concatenate(cols)


def _uq_columns():
    p32 = _deinterleave(MLA_ROPE_DIM)
    cols = []
    for h in range(MLA_HEADS):
        cols += [h * MLA_QK_DIM + np.arange(MLA_NOPE_DIM), h * MLA_QK_DIM + MLA_NOPE_DIM + p32]
    return np.concatenate(cols)


def _block_diag(w):
    depth, g, c, _ = w.shape
    rows = []
    for i in range(g):
        blocks = [w[:, i] if j == i else jnp.zeros((depth, c, c), w.dtype) for j in range(g)]
        rows.append(jnp.concatenate(blocks, axis=2))
    return jnp.concatenate(rows, axis=1)


def _trunk(x, c, w_mod, b_mod, g_norm1, w_in, pool_w, pool_scale, gqa_q_gain, gqa_k_gain, mla_q_gain, mla_kv_gain,
           mla_w_uq, mla_w_ukv, w_out, g_norm2, w_router, w_gate, w_up, w_down, g_final):
    b, s, d = x.shape
    depth = w_mod.shape[0]
    cap = (EC_CAPACITY * s) // N_EXPERTS
    tt = min(TOK_TILE, s)
    tq = min(ATT_TQ, s)
    tk = min(ATT_TK, s)
    nc = s // V7X_LANES

    mod_rows = 8
    c_pad = jnp.zeros((mod_rows, d), f32).at[:b].set(c)
    mod = _modulation(c_pad, w_mod, b_mod)[:, :b].reshape(depth, b, 6, 1, d)

    cg, sg = _rope_tables_t(s, HEAD_DIM)
    cm, sm = _rope_tables_t(s, MLA_ROPE_DIM)
    p64 = _deinterleave(HEAD_DIM)
    rest_cols, uq_cols = _rest_columns(), _uq_columns()

    wp_all = w_in[:, :, :POOL_DIM].astype(bf16)
    wr_all = jnp.swapaxes(w_in[:, :, rest_cols], 1, 2).astype(bf16)
    wuq_all = jnp.swapaxes(mla_w_uq[:, :, uq_cols], 1, 2).astype(bf16)
    wukv_all = jnp.swapaxes(mla_w_ukv, 1, 2).astype(bf16)
    gq_all, gk_all = gqa_q_gain[:, p64, None], gqa_k_gain[:, p64, None]
    gmq_all, gmkv_all = mla_q_gain[:, :, None], mla_kv_gain[:, :, None]
    pw_all = _block_diag(pool_w).astype(bf16)
    wo_all = w_out.astype(bf16)
    n_g = GQA_HEADS * HEAD_DIM

    for l in range(depth):
        sh1, sc1, gt1, sh2, sc2, gt2 = (mod[l, :, k] for k in range(6))
        u, qg, kg, kng, vg, qm, km, knm, vm = _inproj(
            x, g_norm1[l][None], sh1, sc1, wp_all[l], wr_all[l], gq_all[l], gk_all[l], gmq_all[l], gmkv_all[l],
            wuq_all[l], wukv_all[l], cg, sg, cm, sm, tt=max(tt, tk), tk=tk)
        og = _attention(qg, kg, kng, vg, group=GQA_GROUP, tq=min(GQA_TQ, s), n_sub=min(GQA_TQ, s) // tq)
        om = _attention(qm, km, knm, vm, group=1, tq=min(MLA_TQ, s), n_sub=min(MLA_TQ, s) // tq)
        ypool = _pool(u, pw_all[l], pool_scale[l][None], tp=tt)
        wo = wo_all[l]
        x1, h2, aff = _outproj(x, ypool, og, om, wo[:POOL_DIM], wo[POOL_DIM:POOL_DIM + n_g], wo[POOL_DIM + n_g:],
                               gt1, g_norm2[l][None], sh2, sc2, w_router[l], tt=tt)

        aff_r = aff.transpose(0, 2, 1).reshape(b, N_EXPERTS, nc, V7X_LANES)
        posm, pos = _route(aff_r, cap=cap)
        offs = jnp.concatenate([pos[..., 0], jnp.full((b, N_EXPERTS, 1), cap, i32)], axis=-1).reshape(-1)
        posm_c = posm.reshape(b, N_EXPERTS, s // TOK_CHUNK, 1, TOK_CHUNK)
        posm_t = posm.reshape(b, N_EXPERTS, s).transpose(0, 2, 1)
        aff_c = aff_r.reshape(b, N_EXPERTS, s // TOK_CHUNK, 1, TOK_CHUNK)
        xe, gate = _gather(offs, h2, posm_c, aff_c, cap=cap)
        ye = _ffn(xe, gate, w_gate, w_up, w_down, l)
        x = _combine(offs, x1, posm_t, gt2, ye)
    return _final_norm(x, g_final[None], tt=tt)


def kernel(x, c, w_mod, b_mod, g_norm1, w_in, pool_w, pool_scale, gqa_q_gain, gqa_k_gain, mla_q_gain, mla_kv_gain,
           mla_w_uq, mla_w_ukv, w_out, g_norm2, w_router, w_gate, w_up, w_down, g_final):
    return _trunk(x, c, w_mod, b_mod, g_norm1, w_in, pool_w, pool_scale, gqa_q_gain, gqa_k_gain, mla_q_gain,
                  mla_kv_gain, mla_w_uq, mla_w_ukv, w_out, g_norm2, w_router, w_gate, w_up, w_down, g_final)
```

```python
import functools
import math

import numpy as np
import jax
import jax.numpy as jnp
from jax import lax
from jax.experimental import pallas as pl
from jax.experimental.pallas import tpu as pltpu

f32, bf16, i32 = jnp.float32, jnp.bfloat16, jnp.int32

D_MODEL = 1024
DEPTH = 4
GRID_W = 64
ROPE_THETA = 10000.0
EPS = 1e-6
POOL_DIM = 256
POOL_WINDOWS = (2, 4, 8, 16)
POOL_CH = 64
HEAD_DIM = 64
GQA_HEADS = 6
GQA_KV_HEADS = 2
GQA_GROUP = 3
MLA_HEADS = 6
MLA_NOPE_DIM = 64
MLA_ROPE_DIM = 32
MLA_QK_DIM = 96
MLA_V_DIM = 64
MLA_Q_RANK = 256
MLA_KV_RANK = 256
OFF_GQA_Q = POOL_DIM
OFF_GQA_K = OFF_GQA_Q + GQA_HEADS * HEAD_DIM
OFF_GQA_V = OFF_GQA_K + GQA_KV_HEADS * HEAD_DIM
OFF_MLA_Q = OFF_GQA_V + GQA_KV_HEADS * HEAD_DIM
OFF_MLA_KV = OFF_MLA_Q + MLA_Q_RANK
OFF_MLA_ROPE = OFF_MLA_KV + MLA_KV_RANK
IN_DIM = OFF_MLA_ROPE + MLA_ROPE_DIM
REST_DIM = IN_DIM - POOL_DIM
N_EXPERTS = 16
EC_CAPACITY = 2
D_FF = 2048

R_GQ = 0
R_GK = R_GQ + GQA_HEADS * HEAD_DIM
R_GV = R_GK + GQA_KV_HEADS * HEAD_DIM
R_MQ = R_GV + GQA_KV_HEADS * HEAD_DIM
R_MKV = R_MQ + MLA_Q_RANK
R_MR = R_MKV + MLA_KV_RANK

V7X_LANES = 128
V7X_VMEM_LIMIT_BYTES = 60000 * 1024
V7X_SUBLANES = 8
MXU_ROWS_PER_PUSH = 128
V_ROWS = MXU_ROWS_PER_PUSH
K_PAD = 16

TOK_TILE = 512
ATT_TQ = 512
GQA_TQ = 1024
MLA_TQ = 2048
ATT_TK = 512
TOK_CHUNK = 256
FAST_SLOTS = 64
SLOT_ALIGN_LOG2 = 4
GATHER_UNROLL = 8
CHUNKS_PER_TRIP = 8
FFN_ROW_SPLIT = 2
OUTPROJ_ROW_SPLIT = 2
FF_TILE = 512
COL_TILE = 512
COMBINE_TOK = 1024
LOG2E = math.log2(math.e)


def _cparams(sem, vmem=None):
    return pltpu.CompilerParams(dimension_semantics=sem, vmem_limit_bytes=vmem)


def _split_bf16(a):
    hi = a.astype(bf16)
    lo = (a - hi.astype(f32)).astype(bf16)
    return hi, lo


def _dot(a, b):
    return jnp.dot(a, b, preferred_element_type=f32)


def _dot3(a, b):
    ah, al = _split_bf16(a)
    bh, bl = _split_bf16(b)
    return _dot(ah, bh) + _dot(ah, bl) + _dot(al, bh)


def _mod_kernel(c_ref, w_ref, b_ref, o_ref):
    c = c_ref[...]
    act = c * (1.0 / (1.0 + jnp.exp(-c)))
    o_ref[0] = _dot3(act, w_ref[0]) + b_ref[0]


def _modulation(c_pad, w_mod, b_mod):
    depth, d, six_d = w_mod.shape
    rows = c_pad.shape[0]
    return pl.pallas_call(
        _mod_kernel,
        grid=(depth, six_d // d),
        in_specs=[
            pl.BlockSpec((rows, d), lambda l, j: (0, 0)),
            pl.BlockSpec((1, d, d), lambda l, j: (l, 0, j)),
            pl.BlockSpec((1, 1, d), lambda l, j: (l, 0, j)),
        ],
        out_specs=pl.BlockSpec((1, rows, d), lambda l, j: (l, 0, j)),
        out_shape=jax.ShapeDtypeStruct((depth, rows, six_d), f32),
        compiler_params=_cparams(("parallel", "parallel")),
        name="modulation",
    )(c_pad, w_mod, b_mod.reshape(depth, 1, six_d))


def _rms_rows(z, gain_col):
    r = lax.rsqrt(jnp.mean(z * z, axis=0, keepdims=True) + EPS)
    return z * r * gain_col


def _rope_rows(z, cos, sin):
    half = z.shape[0] // 2
    x1, x2 = z[:half], z[half:]
    return jnp.concatenate([x1 * cos - x2 * sin, x1 * sin + x2 * cos], axis=0)


def _inproj_kernel(x_ref, g_ref, sh_ref, sc_ref, wp_ref, wr_ref, gq_ref, gk_ref, gmq_ref, gmkv_ref,
                   wuq_ref, wukv_ref, cg_ref, sg_ref, cm_ref, sm_ref,
                   u_ref, qg_ref, kg_ref, kng_ref, vg_ref, qm_ref, km_ref, knm_ref, vm_ref, *, tk):
    x = x_ref[0]
    tt = x.shape[0]
    h = x * lax.rsqrt(jnp.mean(x * x, axis=-1, keepdims=True) + EPS) * g_ref[...]
    h = h * (1.0 + sc_ref[0]) + sh_ref[0]
    hb = h.astype(bf16)
    u_ref[0] = _dot(hb, wp_ref[...])
    zt = lax.dot_general(wr_ref[...], hb, (((1,), (1,)), ((), ())), preferred_element_type=f32)

    cg, sg, cm, sm = cg_ref[...], sg_ref[...], cm_ref[...], sm_ref[...]
    n_sub = tt // tk

    def put_k(ref, norm_ref, head, kt):
        kb = kt.astype(bf16)
        kf = kb.astype(f32)
        norm_ref[0, head] = jnp.sqrt(jnp.sum(kf * kf, axis=0, keepdims=True))
        pad = jnp.where(lax.broadcasted_iota(i32, (K_PAD, tt), 0) == 0, 1.0, 0.0)
        ke = jnp.concatenate([kt, pad], axis=0)
        ref[0, head] = ke.T.astype(bf16)

    v_pad = jnp.zeros((V_ROWS - MLA_V_DIM, tt), f32)

    def put_v(ref, head, vt):
        ve = jnp.concatenate([vt, v_pad], axis=0).astype(bf16)
        for j in range(n_sub):
            ref[0, head, j] = ve[:, j * tk:(j + 1) * tk]

    gq = gq_ref[...] * (HEAD_DIM ** -0.5 * LOG2E)
    gk = gk_ref[...]
    for hd in range(GQA_HEADS):
        q = _rms_rows(zt[R_GQ + hd * HEAD_DIM:R_GQ + (hd + 1) * HEAD_DIM], gq)
        qg_ref[0, hd] = _rope_rows(q, cg, sg).astype(bf16)
    for hk in range(GQA_KV_HEADS):
        k = _rms_rows(zt[R_GK + hk * HEAD_DIM:R_GK + (hk + 1) * HEAD_DIM], gk)
        put_k(kg_ref, kng_ref, hk, _rope_rows(k, cg, sg))
        put_v(vg_ref, hk, zt[R_GV + hk * HEAD_DIM:R_GV + (hk + 1) * HEAD_DIM])

    cq = _rms_rows(zt[R_MQ:R_MQ + MLA_Q_RANK], gmq_ref[...]).astype(bf16)
    qm = _dot(wuq_ref[...], cq) * (MLA_QK_DIM ** -0.5 * LOG2E)
    ckv = _rms_rows(zt[R_MKV:R_MKV + MLA_KV_RANK], gmkv_ref[...]).astype(bf16)
    kv = _dot(wukv_ref[...], ckv)
    k_rope = _rope_rows(zt[R_MR:R_MR + MLA_ROPE_DIM], cm, sm)
    for hd in range(MLA_HEADS):
        qh = qm[hd * MLA_QK_DIM:(hd + 1) * MLA_QK_DIM]
        qr = _rope_rows(qh[MLA_NOPE_DIM:], cm, sm)
        qm_ref[0, hd] = jnp.concatenate([qh[:MLA_NOPE_DIM], qr], axis=0).astype(bf16)
        kvh = kv[hd * (MLA_NOPE_DIM + MLA_V_DIM):(hd + 1) * (MLA_NOPE_DIM + MLA_V_DIM)]
        kh = jnp.concatenate([kvh[:MLA_NOPE_DIM], k_rope], axis=0)
        put_k(km_ref, knm_ref, hd, kh)
        put_v(vm_ref, hd, kvh[MLA_NOPE_DIM:])


def _inproj(x, g1, sh1, sc1, wp, wr, gq, gk, gmq, gmkv, wuq, wukv, cg, sg, cm, sm, *, tt, tk):
    b, s, d = x.shape
    n_t = s // tt
    n_sub = tt // tk
    full = lambda shape: pl.BlockSpec(shape, lambda bi, i: (0,) * len(shape))
    vec = pl.BlockSpec((1, 1, d), lambda bi, i: (bi, 0, 0))
    rope_g = pl.BlockSpec((HEAD_DIM // 2, tt), lambda bi, i: (0, i))
    rope_m = pl.BlockSpec((MLA_ROPE_DIM // 2, tt), lambda bi, i: (0, i))
    out_shapes = (
        jax.ShapeDtypeStruct((b, s, POOL_DIM), f32),
        jax.ShapeDtypeStruct((b, GQA_HEADS, HEAD_DIM, s), bf16),
        jax.ShapeDtypeStruct((b, GQA_KV_HEADS, s, HEAD_DIM + K_PAD), bf16),
        jax.ShapeDtypeStruct((b, GQA_KV_HEADS, 1, s), f32),
        jax.ShapeDtypeStruct((b, GQA_KV_HEADS, s // tk, V_ROWS, tk), bf16),
        jax.ShapeDtypeStruct((b, MLA_HEADS, MLA_QK_DIM, s), bf16),
        jax.ShapeDtypeStruct((b, MLA_HEADS, s, MLA_QK_DIM + K_PAD), bf16),
        jax.ShapeDtypeStruct((b, MLA_HEADS, 1, s), f32),
        jax.ShapeDtypeStruct((b, MLA_HEADS, s // tk, V_ROWS, tk), bf16),
    )
    out_specs = (
        pl.BlockSpec((1, tt, POOL_DIM), lambda bi, i: (bi, i, 0)),
        pl.BlockSpec((1, GQA_HEADS, HEAD_DIM, tt), lambda bi, i: (bi, 0, 0, i)),
        pl.BlockSpec((1, GQA_KV_HEADS, tt, HEAD_DIM + K_PAD), lambda bi, i: (bi, 0, i, 0)),
        pl.BlockSpec((1, GQA_KV_HEADS, 1, tt), lambda bi, i: (bi, 0, 0, i)),
        pl.BlockSpec((1, GQA_KV_HEADS, n_sub, V_ROWS, tk), lambda bi, i: (bi, 0, i, 0, 0)),
        pl.BlockSpec((1, MLA_HEADS, MLA_QK_DIM, tt), lambda bi, i: (bi, 0, 0, i)),
        pl.BlockSpec((1, MLA_HEADS, tt, MLA_QK_DIM + K_PAD), lambda bi, i: (bi, 0, i, 0)),
        pl.BlockSpec((1, MLA_HEADS, 1, tt), lambda bi, i: (bi, 0, 0, i)),
        pl.BlockSpec((1, MLA_HEADS, n_sub, V_ROWS, tk), lambda bi, i: (bi, 0, i, 0, 0)),
    )
    return pl.pallas_call(
        functools.partial(_inproj_kernel, tk=tk),
        grid=(b, n_t),
        in_specs=[
            pl.BlockSpec((1, tt, d), lambda bi, i: (bi, i, 0)),
            full((1, d)), vec, vec,
            full(wp.shape), full(wr.shape), full(gq.shape), full(gk.shape), full(gmq.shape), full(gmkv.shape),
            full(wuq.shape), full(wukv.shape), rope_g, rope_g, rope_m, rope_m,
        ],
        out_specs=out_specs,
        out_shape=out_shapes,
        compiler_params=_cparams(("parallel", "parallel"), V7X_VMEM_LIMIT_BYTES),
        name="inproj",
    )(x, g1, sh1, sc1, wp, wr, gq, gk, gmq, gmkv, wuq, wukv, cg, sg, cm, sm)


SAFE_LOGIT_BOUND = 50.0


def _attn_kernel(q_ref, k_ref, kn_ref, v_ref, o_ref, s0_ref, m_ref, l_ref, acc_ref, *, tk, n_chunks, dv, n_sub):
    group, tq = q_ref.shape[1], q_ref.shape[3]
    ts = tq // n_sub
    streams = [(h, j) for h in range(group) for j in range(n_sub)]
    qts = [q_ref[0, h, :, j * ts:(j + 1) * ts] for h, j in streams]

    def k_chunk(c):
        return k_ref[0, 0, pl.ds(pl.multiple_of(c * tk, tk), tk), :]

    def finish():
        for n, (h, j) in enumerate(streams):
            l = jnp.sum(l_ref[n], axis=0, keepdims=True)
            o_ref[0, h * dv:(h + 1) * dv, j * ts:(j + 1) * ts] = (acc_ref[n][:dv] / l).astype(o_ref.dtype)

    def row_sums(p):
        return jnp.sum(p.reshape(tk // V7X_SUBLANES, V7X_SUBLANES, ts), axis=0)

    k_max = jnp.max(kn_ref[0, 0], axis=-1, keepdims=True)
    bounds = []
    for qt in qts:
        qf = qt.astype(f32)
        bounds.append(jnp.sqrt(jnp.sum(qf * qf, axis=0, keepdims=True)) * k_max)
    worst = functools.reduce(jnp.maximum, [jnp.max(u, axis=-1, keepdims=True) for u in bounds])
    safe = worst[0, 0] <= SAFE_LOGIT_BOUND

    @pl.when(safe)
    def _():
        row0 = lax.broadcasted_iota(i32, (K_PAD, ts), 0) == 0
        qes = [jnp.concatenate([qt, jnp.where(row0, -c, 0.0).astype(qt.dtype)], axis=0) for qt, c in zip(qts, bounds)]
        acc_ref[...] = jnp.zeros(acc_ref.shape, f32)
        l_ref[...] = jnp.zeros(l_ref.shape, f32)
        s0_ref[...] = _dot(k_chunk(0), qes[0])
        per_trip = math.gcd(n_chunks, CHUNKS_PER_TRIP)
        order = [(dc, n) for dc in range(per_trip) for n in range(len(streams))]

        def step(i, carry):
            c = per_trip * i
            kcs = [k_chunk(jnp.minimum(c + dc, n_chunks - 1)) for dc in range(per_trip + 1)]
            vcs = [v_ref[0, 0, c + dc] for dc in range(per_trip)]
            s_cur = s0_ref[...]
            for idx, (dc, n) in enumerate(order):
                dc2, n2 = order[idx + 1] if idx + 1 < len(order) else (per_trip, 0)
                s_next = _dot(kcs[dc2], qes[n2])
                p = jnp.exp2(s_cur)
                l_ref[n] += row_sums(p)
                acc_ref[n] += _dot(vcs[dc], p.astype(bf16))
                s_cur = s_next
            s0_ref[...] = s_cur
            return carry

        lax.fori_loop(0, n_chunks // per_trip, step, 0)
        finish()

    @pl.when(jnp.logical_not(safe))
    def _():
        m_ref[...] = jnp.full(m_ref.shape, -jnp.inf, f32)
        acc_ref[...] = jnp.zeros(acc_ref.shape, f32)
        l_ref[...] = jnp.zeros(l_ref.shape, f32)
        qzs =[jnp.concatenate([qt, jnp.zeros((K_PAD, ts), qt.dtype)], axis=0) for qt in qts]

        def step(c, carry):
            kc = k_chunk(c)
            vc = v_ref[0, 0, c]
            for n, qz in enumerate(qzs):
                s = _dot(kc, qz)
                m = m_ref[n]
                m_new = jnp.maximum(m, jnp.max(s, axis=0, keepdims=True))
                p = jnp.exp2(s - m_new)
                alpha = jnp.exp2(m - m_new)
                l_ref[n] = l_ref[n] * alpha + row_sums(p)
                acc_ref[n] = acc_ref[n] * alpha + _dot(vc, p.astype(bf16))
                m_ref[n] = m_new
            return carry

        lax.fori_loop(0, n_chunks, step, 0)
        finish()


def _attention(qt, k, kn, vt, *, group, tq, n_sub):
    b, hq, dq, s = qt.shape
    _, hk, n_chunks, v_rows, tk = vt.shape
    dv = MLA_V_DIM
    assert tq % n_sub == 0
    n_streams, ts = group * n_sub, tq // n_sub
    return pl.pallas_call(
        functools.partial(_attn_kernel, tk=tk, n_chunks=n_chunks, dv=dv, n_sub=n_sub),
        grid=(b, hk, s // tq),
        scratch_shapes=[
            pltpu.VMEM((tk, ts), f32),
            pltpu.VMEM((n_streams, 1, ts), f32),
            pltpu.VMEM((n_streams, V7X_SUBLANES, ts), f32),
            pltpu.VMEM((n_streams, v_rows, ts), f32),
        ],
        in_specs=[
            pl.BlockSpec((1, group, dq, tq), lambda bi, g, i: (bi, g, 0, i)),
            pl.BlockSpec((1, 1, s, dq + K_PAD), lambda bi, g, i: (bi, g, 0, 0)),
            pl.BlockSpec((1, 1, 1, s), lambda bi, g, i: (bi, g, 0, 0)),
            pl.BlockSpec((1, 1, n_chunks, v_rows, tk), lambda bi, g, i: (bi, g, 0, 0, 0)),
        ],
        out_specs=pl.BlockSpec((1, group * dv, tq), lambda bi, g, i: (bi, g, i)),
        out_shape=jax.ShapeDtypeStruct((b, hq * dv, s), bf16),
        compiler_params=_cparams(("parallel", "parallel", "parallel"), V7X_VMEM_LIMIT_BYTES),
        name="attention",
    )(qt, k, kn, vt)


POOL_HALO = 16


def _pool_kernel(up_ref, uc_ref, un_ref, w_ref, sc_ref, o_ref, *, seq):
    i = pl.program_id(1)
    n_t = pl.num_programs(1)
    cur = uc_ref[0]
    tp = cur.shape[0]
    prev = jnp.where(i > 0, up_ref[0], 0.0)
    nxt = jnp.where(i < n_t - 1, un_ref[0], 0.0)
    ext = jnp.concatenate([prev, cur, nxt], axis=0)
    t = i * tp + lax.broadcasted_iota(i32, (tp, 1), 0)
    lane = lax.broadcasted_iota(i32, (tp, POOL_DIM), 1)
    run, length = ext, 1
    wsum = cnt = None
    for gi, win in enumerate(POOL_WINDOWS):
        while length < win:
            run = run[:run.shape[0] - length] + run[length:]
            length *= 2
        lo = win // 2
        hi = win - 1 - lo
        mine = run[POOL_HALO - lo:POOL_HALO - lo + tp]
        n_valid = (jnp.minimum(t + hi, seq - 1) - jnp.maximum(t - lo, 0) + 1).astype(f32)
        if gi == 0:
            wsum, cnt = mine, jnp.broadcast_to(n_valid, (tp, POOL_DIM))
        else:
            in_later_group = lane >= gi * POOL_CH
            wsum = jnp.where(in_later_group, mine, wsum)
            cnt = jnp.where(in_later_group, n_valid, cnt)
    p = (wsum / cnt - cur).astype(bf16)
    o_ref[0] = (_dot(p, w_ref[...]) * sc_ref[...]).astype(o_ref.dtype)


def _pool(u, w_bd, scale, *, tp):
    b, s, c = u.shape
    n_t = s // tp
    r = tp // POOL_HALO
    return pl.pallas_call(
        functools.partial(_pool_kernel, seq=s),
        grid=(b, n_t),
        in_specs=[
            pl.BlockSpec((1, POOL_HALO, c), lambda bi, i: (bi, jnp.maximum(i * r - 1, 0), 0)),
            pl.BlockSpec((1, tp, c), lambda bi, i: (bi, i, 0)),
            pl.BlockSpec((1, POOL_HALO, c), lambda bi, i: (bi, jnp.minimum((i + 1) * r, s // POOL_HALO - 1), 0)),
            pl.BlockSpec(w_bd.shape, lambda bi, i: (0, 0)),
            pl.BlockSpec((1, c), lambda bi, i: (0, 0)),
        ],
        out_specs=pl.BlockSpec((1, tp, c), lambda bi, i: (bi, i, 0)),
        out_shape=jax.ShapeDtypeStruct((b, s, c), bf16),
        compiler_params=_cparams(("parallel", "parallel")),
        name="pool",
    )(u, u, u, w_bd, scale)


def _outproj_kernel(x_ref, yp_ref, og_ref, om_ref, wop_ref, wog_ref, wom_ref, gt_ref, g2_ref, sh_ref, sc_ref,
                    wrt_ref, x1_ref, h2_ref, aff_ref):
    tn = (((0,), (0,)), ((), ()))
    n_e = aff_ref.shape[2]
    w_hi, w_lo = _split_bf16(wrt_ref[...])
    w_hilo = jnp.concatenate([w_hi, w_lo], axis=1)
    rows = x_ref.shape[1] // OUTPROJ_ROW_SPLIT
    for r in range(OUTPROJ_ROW_SPLIT):
        sl = pl.ds(r * rows, rows)
        ls = pl.ds(r * rows, rows)
        y = _dot(yp_ref[0, sl, :], wop_ref[...])
        y = y + lax.dot_general(og_ref[0, :, ls], wog_ref[...], tn, preferred_element_type=f32)
        y = y + lax.dot_general(om_ref[0, :, ls], wom_ref[...], tn, preferred_element_type=f32)
        x1 = x_ref[0, sl, :] + gt_ref[0] * y
        x1_ref[0, sl, :] = x1
        h = x1 * lax.rsqrt(jnp.mean(x1 * x1, axis=-1, keepdims=True) + EPS) * g2_ref[...]
        h = h * (1.0 + sc_ref[0]) + sh_ref[0]
        h2_ref[0, sl, :] = h.astype(bf16)
        h_hi, h_lo = _split_bf16(h)
        two = _dot(h_hi, w_hilo)
        logits = two[:, :n_e] + two[:, n_e:] + _dot(h_lo, w_hi)
        ex = jnp.exp(logits - jnp.max(logits, axis=-1, keepdims=True))
        aff_ref[0, sl, :] = ex / jnp.sum(ex, axis=-1, keepdims=True)


def _outproj(x, ypool, og, om, wop, wog, wom, gt1, g2, sh2, sc2, w_router, *, tt):
    b, s, d = x.shape
    full = lambda shape: pl.BlockSpec(shape, lambda bi, i: (0,) * len(shape))
    vec = pl.BlockSpec((1, 1, d), lambda bi, i: (bi, 0, 0))
    n_e = w_router.shape[1]
    return pl.pallas_call(
        _outproj_kernel,
        grid=(b, s // tt),
        in_specs=[
            pl.BlockSpec((1, tt, d), lambda bi, i: (bi, i, 0)),
            pl.BlockSpec((1, tt, POOL_DIM), lambda bi, i: (bi, i, 0)),
            pl.BlockSpec((1, og.shape[1], tt), lambda bi, i: (bi, 0, i)),
            pl.BlockSpec((1, om.shape[1], tt), lambda bi, i: (bi, 0, i)),
            full(wop.shape), full(wog.shape), full(wom.shape),
            vec, full((1, d)), vec, vec, full(w_router.shape),
        ],
        out_specs=(
            pl.BlockSpec((1, tt, d), lambda bi, i: (bi, i, 0)),
            pl.BlockSpec((1, tt, d), lambda bi, i: (bi, i, 0)),
            pl.BlockSpec((1, tt, n_e), lambda bi, i: (bi, i, 0)),
        ),
        out_shape=(
            jax.ShapeDtypeStruct((b, s, d), f32),
            jax.ShapeDtypeStruct((b, s, d), bf16),
            jax.ShapeDtypeStruct((b, s, n_e), f32),
        ),
        compiler_params=_cparams(("parallel", "parallel"), V7X_VMEM_LIMIT_BYTES),
        name="outproj",
    )(x, ypool, og, om, wop, wog, wom, gt1, g2, sh2, sc2, w_router)


def _route_kernel(a_ref, posm_ref, pos_ref, *, cap):
    a = a_ref[0]
    n_e, nc, ln = a.shape
    bits = pltpu.bitcast(a, i32)

    def count(mask):
        c = jnp.sum(jnp.where(mask, 1.0, 0.0), axis=2, keepdims=True)
        return jnp.sum(c, axis=1, keepdims=True)

    thr = jnp.zeros((n_e, 1, 1), i32)
    for bit in range(30, -1, -1):
        cand = thr | (1 << bit)
        thr = jnp.where(count(bits >= cand) >= cap, cand, thr)
    gt = bits > thr
    eq = bits == thr
    need = cap - count(gt)

    r_i = lax.broadcasted_iota(i32, (ln, ln), 0)
    c_i = lax.broadcasted_iota(i32, (ln, ln), 1)
    tri_incl = jnp.where(r_i <= c_i, 1.0, 0.0).astype(bf16)
    r_c = lax.broadcasted_iota(i32, (nc, nc), 0)
    c_c = lax.broadcasted_iota(i32, (nc, nc), 1)
    tri_strict = jnp.where(c_c < r_c, 1.0, 0.0).astype(bf16)

    def excl_prefix(mask):
        x = jnp.where(mask, 1.0, 0.0)
        incl = _dot(x.astype(bf16).reshape(n_e * nc, ln), tri_incl).reshape(n_e, nc, ln)
        tot = jnp.broadcast_to(incl[:, :, ln - 1:ln], (n_e, nc, ln))
        tot_hi = tot.astype(bf16)
        offs = [_dot(tri_strict, tot_hi[e]) for e in range(n_e)]
        return jnp.stack(offs, axis=0) + incl - x

    sel = gt | (eq & (excl_prefix(eq) < need))
    pos = excl_prefix(sel).astype(i32)
    pos_ref[0] = pos
    posm_ref[0] = jnp.where(sel, pos, -1)


def _route(aff_r, *, cap):
    b, n_e, nc, ln = aff_r.shape
    spec = pl.BlockSpec((1, n_e, nc, ln), lambda bi: (bi, 0, 0, 0))
    return pl.pallas_call(
        functools.partial(_route_kernel, cap=cap),
        grid=(b,),
        in_specs=[spec],
        out_specs=(spec, spec),
        out_shape=(jax.ShapeDtypeStruct(aff_r.shape, i32), jax.ShapeDtypeStruct(aff_r.shape, i32)),
        compiler_params=_cparams(("parallel",), V7X_VMEM_LIMIT_BYTES),
        name="route",
    )(aff_r)


def _window_start(lo, width, cap):
    aligned = lax.shift_left(lax.shift_right_logical(lo, SLOT_ALIGN_LOG2), SLOT_ALIGN_LOG2)
    return pl.multiple_of(jnp.minimum(aligned, cap - width), 1 << SLOT_ALIGN_LOG2)


def _window_widths(cap, usual=FAST_SLOTS):
    return min(usual, cap), min(TOK_CHUNK + (1 << SLOT_ALIGN_LOG2), cap)


def _gather_kernel(offs_ref, h_ref, posm_ref, aff_ref, xe_ref, gate_ref, *, n_off, n_chunks):
    b, e = pl.program_id(0), pl.program_id(1)
    base = (b * pl.num_programs(1) + e) * n_off
    step = TOK_CHUNK // V7X_LANES
    cap = xe_ref.shape[2]
    fast_w, slow_w = _window_widths(cap, usual=MXU_ROWS_PER_PUSH)
    xe_ref[...] = jnp.zeros_like(xe_ref)
    gate_ref[...] = jnp.zeros_like(gate_ref)

    def misfit(c, bad):
        lo = offs_ref[base + c * step]
        hi = offs_ref[base + (c + 1) * step]
        return bad + (hi - _window_start(lo, fast_w, cap) > fast_w).astype(i32)

    bad = lax.fori_loop(0, n_chunks, misfit, jnp.int32(0))

    def run(width):
        slot_iota = lax.broadcasted_iota(i32, (width, TOK_CHUNK), 0)

        def chunk(c, carry):
            w = _window_start(offs_ref[base + c * step], width, cap)
            tok0 = pl.multiple_of(c * TOK_CHUNK, TOK_CHUNK)
            pr = posm_ref[0, 0, c]
            hit = pr == slot_iota + w
            rows = _dot(jnp.where(hit, 1.0, 0.0).astype(bf16), h_ref[0, pl.ds(tok0, TOK_CHUNK), :])
            xe_ref[0, 0, pl.ds(w, width), :] = xe_ref[0, 0, pl.ds(w, width), :] + rows.astype(xe_ref.dtype)
            gate_ref[0, 0, pl.ds(w, width), :] += jnp.sum(jnp.where(hit, aff_ref[0, 0, c], 0.0), axis=1, keepdims=True)
            return carry

        lax.fori_loop(0, n_chunks, chunk, 0, unroll=GATHER_UNROLL)

    @pl.when(bad == 0)
    def _():
        run(fast_w)

    @pl.when(bad != 0)
    def _():
        run(slow_w)


def _gather(offs, h2, posm_c, aff_c, *, cap):
    b, s, d = h2.shape
    n_e = posm_c.shape[1]
    n_chunks = s // TOK_CHUNK
    n_off = s // V7X_LANES + 1
    chunked = pl.BlockSpec((1, 1, n_chunks, 1, TOK_CHUNK), lambda bi, e, offs: (bi, e, 0, 0, 0))
    return pl.pallas_call(
        functools.partial(_gather_kernel, n_off=n_off, n_chunks=n_chunks),
        grid_spec=pltpu.PrefetchScalarGridSpec(
            num_scalar_prefetch=1,
            grid=(b, n_e),
            in_specs=[
                pl.BlockSpec((1, s, d), lambda bi, e, offs: (bi, 0, 0), pipeline_mode=pl.Buffered(1)),
                chunked, chunked,
            ],
            out_specs=(pl.BlockSpec((1, 1, cap, d), lambda bi, e, offs: (bi, e, 0, 0)),
                       pl.BlockSpec((1, 1, cap, 1), lambda bi, e, offs: (bi, e, 0, 0))),
        ),
        out_shape=(jax.ShapeDtypeStruct((b, n_e, cap, d), bf16), jax.ShapeDtypeStruct((b, n_e, cap, 1), f32)),
        compiler_params=_cparams(("arbitrary", "arbitrary"), V7X_VMEM_LIMIT_BYTES),
        name="gather",
    )(offs, h2, posm_c, aff_c)


def _ffn_kernel(x_ref, gate_ref, wg_ref, wu_ref, wd_ref, o_ref, acc_ref):
    @pl.when(pl.program_id(2) == 0)
    def _():
        acc_ref[...] = jnp.zeros_like(acc_ref)

    wg, wu, wd = wg_ref[0, 0].astype(bf16), wu_ref[0, 0].astype(bf16), wd_ref[0, 0].astype(bf16)
    rows = x_ref.shape[2] // FFN_ROW_SPLIT
    for r in range(FFN_ROW_SPLIT):
        sl = pl.ds(r * rows, rows)
        x = x_ref[0, 0, sl, :]
        a = _dot(x, wg)
        u = _dot(x, wu)
        hmid = (a * (1.0 / (1.0 + jnp.exp(-a))) * u).astype(bf16)
        total = acc_ref[sl, :] + _dot(hmid, wd)
        acc_ref[sl, :] = total
        o_ref[0, 0, sl, :] = (total * gate_ref[0, 0, sl, :]).astype(o_ref.dtype)


def _ffn(xe, gate, w_gate, w_up, w_down, layer):
    b, n_e, cap, d = xe.shape
    d_ff = w_gate.shape[-1]
    n_f = d_ff // FF_TILE
    return pl.pallas_call(
        _ffn_kernel,
        grid=(n_e, b, n_f),
        in_specs=[
            pl.BlockSpec((1, 1, cap, d), lambda e, bi, f: (bi, e, 0, 0)),
            pl.BlockSpec((1, 1, cap, 1), lambda e, bi, f: (bi, e, 0, 0)),
            pl.BlockSpec((1, 1, d, FF_TILE), lambda e, bi, f: (layer, e, 0, f)),
            pl.BlockSpec((1, 1, d, FF_TILE), lambda e, bi, f: (layer, e, 0, f)),
            pl.BlockSpec((1, 1, FF_TILE, d), lambda e, bi, f: (layer, e, f, 0)),
        ],
        out_specs=pl.BlockSpec((1, 1, cap, d), lambda e, bi, f: (bi, e, 0, 0)),
        out_shape=jax.ShapeDtypeStruct((b, n_e, cap, d), bf16),
        scratch_shapes=[pltpu.VMEM((cap, d), f32)],
        compiler_params=_cparams(("parallel", "parallel", "arbitrary"), V7X_VMEM_LIMIT_BYTES),
        name="expert_ffn",
    )(xe, gate, w_gate, w_up, w_down)


def _combine_kernel(offs_ref, x_ref, posm_ref, gt_ref, ye_ref, o_ref, *, n_off):
    for sub in range(x_ref.shape[1] // TOK_CHUNK):
        _combine_chunk(offs_ref, x_ref, posm_ref, gt_ref, ye_ref, o_ref, n_off=n_off, sub=sub)


def _combine_chunk(offs_ref, x_ref, posm_ref, gt_ref, ye_ref, o_ref, *, n_off, sub):
    b = pl.program_id(0)
    i = pl.program_id(2) * (x_ref.shape[1] // TOK_CHUNK) + sub
    tok = pl.ds(sub * TOK_CHUNK, TOK_CHUNK)
    n_e, cap = ye_ref.shape[1], ye_ref.shape[2]
    fast_w, slow_w = _window_widths(cap)
    step = TOK_CHUNK // V7X_LANES
    posm = posm_ref[0, tok, :]
    los = [offs_ref[(b * n_e + e) * n_off + i * step] for e in range(n_e)]
    his = [offs_ref[(b * n_e + e) * n_off + (i + 1) * step] for e in range(n_e)]
    bad = functools.reduce(
        lambda a, c: a + c, [(his[e] - _window_start(los[e], fast_w, cap) > fast_w).astype(i32) for e in range(n_e)])
    paired = 2 * fast_w == V7X_LANES and n_e % 2 == 0

    def run_stacked():
        lane = lax.broadcasted_iota(i32, (TOK_CHUNK, V7X_LANES), 1)
        ws = [_window_start(los[e], fast_w, cap) for e in range(n_e)]
        tiles, rows = [], []
        for e in range(0, n_e, 2):
            target = jnp.where(lane < fast_w, posm[:, e:e + 1] - ws[e], posm[:, e + 1:e + 2] - ws[e + 1] + fast_w)
            tiles.append(jnp.where(target == lane, 1.0, 0.0).astype(bf16))
            rows += [ye_ref[0, e, pl.ds(ws[e], fast_w), :], ye_ref[0, e + 1, pl.ds(ws[e + 1], fast_w), :]]
        total = _dot(jnp.concatenate(tiles, axis=1), jnp.concatenate(rows, axis=0))
        o_ref[0, tok, :] = x_ref[0, tok, :] + gt_ref[0] * total

    def run_per_expert(width):
        slot_iota = lax.broadcasted_iota(i32, (TOK_CHUNK, width), 1)
        total = jnp.zeros((TOK_CHUNK, o_ref.shape[2]), f32)
        for e in range(n_e):
            w = _window_start(los[e], width, cap)
            onehot = jnp.where(posm[:, e:e + 1] == slot_iota + w, 1.0, 0.0).astype(bf16)
            total = total + _dot(onehot, ye_ref[0, e, pl.ds(w, width), :])
        o_ref[0, tok, :] = x_ref[0, tok, :] + gt_ref[0] * total

    @pl.when(bad == 0)
    def _():
        run_stacked() if paired else run_per_expert(fast_w)

    @pl.when(bad != 0)
    def _():
        run_per_expert(slow_w)


def _combine(offs, x1, posm_t, gt2, ye):
    b, s, d = x1.shape
    n_e, cap = ye.shape[1], ye.shape[2]
    n_off = s // V7X_LANES + 1
    tok = math.gcd(s, COMBINE_TOK)
    return pl.pallas_call(
        functools.partial(_combine_kernel, n_off=n_off),
        grid_spec=pltpu.PrefetchScalarGridSpec(
            num_scalar_prefetch=1,
            grid=(b, d // COL_TILE, s // tok),
            in_specs=[
                pl.BlockSpec((1, tok, COL_TILE), lambda bi, j, i, offs: (bi, i, j)),
                pl.BlockSpec((1, tok, n_e), lambda bi, j, i, offs: (bi, i, 0)),
                pl.BlockSpec((1, 1, COL_TILE), lambda bi, j, i, offs: (bi, 0, j)),
                pl.BlockSpec((1, n_e, cap, COL_TILE), lambda bi, j, i, offs: (bi, 0, 0, j),
                             pipeline_mode=pl.Buffered(1)),
            ],
            out_specs=pl.BlockSpec((1, tok, COL_TILE), lambda bi, j, i, offs: (bi, i, j)),
        ),
        out_shape=jax.ShapeDtypeStruct((b, s, d), f32),
        compiler_params=_cparams(("arbitrary", "arbitrary", "arbitrary"), V7X_VMEM_LIMIT_BYTES),
        name="combine",
    )(offs, x1, posm_t, gt2, ye)


def _final_kernel(x_ref, g_ref, o_ref):
    x = x_ref[0]
    o_ref[0] = x * lax.rsqrt(jnp.mean(x * x, axis=-1, keepdims=True) + EPS) * g_ref[...]


def _final_norm(x, g, *, tt):
    b, s, d = x.shape
    return pl.pallas_call(
        _final_kernel,
        grid=(b, s // tt),
        in_specs=[pl.BlockSpec((1, tt, d), lambda bi, i: (bi, i, 0)), pl.BlockSpec((1, d), lambda bi, i: (0, 0))],
        out_specs=pl.BlockSpec((1, tt, d), lambda bi, i: (bi, i, 0)),
        out_shape=jax.ShapeDtypeStruct((b, s, d), f32),
        compiler_params=_cparams(("parallel", "parallel")),
        name="final_norm",
    )(x, g)


def _deinterleave(n):
    return np.concatenate([np.arange(0, n, 2), np.arange(1, n, 2)])


def _rope_tables_t(n, d_rot):
    n_rows = n // GRID_W
    row = jnp.repeat(jnp.arange(n_rows, dtype=f32), GRID_W)
    col = jnp.tile(jnp.arange(GRID_W, dtype=f32), n_rows)
    n_freq = d_rot // 4
    inv_freq = ROPE_THETA ** (-jnp.arange(n_freq, dtype=f32) / n_freq)
    ang = jnp.concatenate([row[:, None] * inv_freq, col[:, None] * inv_freq], axis=-1)
    return jnp.cos(ang).T, jnp.sin(ang).T


def _rest_columns():
    p64, p32 = _deinterleave(HEAD_DIM), _deinterleave(MLA_ROPE_DIM)
    cols = [OFF_GQA_Q + h * HEAD_DIM + p64 for h in range(GQA_HEADS)]
    cols += [OFF_GQA_K + h * HEAD_DIM + p64 for h in range(GQA_KV_HEADS)]
    cols += [np.arange(OFF_GQA_V, IN_DIM - MLA_ROPE_DIM), OFF_MLA_ROPE + p32]
    return np.concatenate(cols)


def _uq_columns():
    p32 = _deinterleave(MLA_ROPE_DIM)
    cols = []
    for h in range(MLA_HEADS):
        cols += [h * MLA_QK_DIM + np.arange(MLA_NOPE_DIM), h * MLA_QK_DIM + MLA_NOPE_DIM + p32]
    return np.concatenate(cols)


def _block_diag(w):
    depth, g, c, _ = w.shape
    rows = []
    for i in range(g):
        blocks = [w[:, i] if j == i else jnp.zeros((depth, c, c), w.dtype) for j in range(g)]
        rows.append(jnp.concatenate(blocks, axis=2))
    return jnp.concatenate(rows, axis=1)


def _trunk(x, c, w_mod, b_mod, g_norm1, w_in, pool_w, pool_scale, gqa_q_gain, gqa_k_gain, mla_q_gain, mla_kv_gain,
           mla_w_uq, mla_w_ukv, w_out, g_norm2, w_router, w_gate, w_up, w_down, g_final):
    b, s, d = x.shape
    depth = w_mod.shape[0]
    cap = (EC_CAPACITY * s) // N_EXPERTS
    tt = min(TOK_TILE, s)
    tq = min(ATT_TQ, s)
    tk = min(ATT_TK, s)
    nc = s // V7X_LANES

    mod_rows = 8
    c_pad = jnp.zeros((mod_rows, d), f32).at[:b].set(c)
    mod = _modulation(c_pad, w_mod, b_mod)[:, :b].reshape(depth, b, 6, 1, d)

    cg, sg = _rope_tables_t(s, HEAD_DIM)
    cm, sm = _rope_tables_t(s, MLA_ROPE_DIM)
    p64 = _deinterleave(HEAD_DIM)
    rest_cols, uq_cols = _rest_columns(), _uq_columns()

    wp_all = w_in[:, :, :POOL_DIM].astype(bf16)
    wr_all = jnp.swapaxes(w_in[:, :, rest_cols], 1, 2).astype(bf16)
    wuq_all = jnp.swapaxes(mla_w_uq[:, :, uq_cols], 1, 2).astype(bf16)
    wukv_all = jnp.swapaxes(mla_w_ukv, 1, 2).astype(bf16)
    gq_all, gk_all = gqa_q_gain[:, p64, None], gqa_k_gain[:, p64, None]
    gmq_all, gmkv_all = mla_q_gain[:, :, None], mla_kv_gain[:, :, None]
    pw_all = _block_diag(pool_w).astype(bf16)
    wo_all = w_out.astype(bf16)
    n_g = GQA_HEADS * HEAD_DIM

    for l in range(depth):
        sh1, sc1, gt1, sh2, sc2, gt2 = (mod[l, :, k] for k in range(6))
        u, qg, kg, kng, vg, qm, km, knm, vm = _inproj(
            x, g_norm1[l][None], sh1, sc1, wp_all[l], wr_all[l], gq_all[l], gk_all[l], gmq_all[l], gmkv_all[l],
            wuq_all[l], wukv_all[l], cg, sg, cm, sm, tt=max(tt, tk), tk=tk)
        og = _attention(qg, kg, kng, vg, group=GQA_GROUP, tq=min(GQA_TQ, s), n_sub=min(GQA_TQ, s) // tq)
        om = _attention(qm, km, knm, vm, group=1, tq=min(MLA_TQ, s), n_sub=min(MLA_TQ, s) // tq)
        ypool = _pool(u, pw_all[l], pool_scale[l][None], tp=tt)
        wo = wo_all[l]
        x1, h2, aff = _outproj(x, ypool, og, om, wo[:POOL_DIM], wo[POOL_DIM:POOL_DIM + n_g], wo[POOL_DIM + n_g:],
                               gt1, g_norm2[l][None], sh2, sc2, w_router[l], tt=tt)

        aff_r = aff.transpose(0, 2, 1).reshape(b, N_EXPERTS, nc, V7X_LANES)
        posm, pos = _route(aff_r, cap=cap)
        offs = jnp.concatenate([pos[..., 0], jnp.full((b, N_EXPERTS, 1), cap, i32)], axis=-1).reshape(-1)
        posm_c = posm.reshape(b, N_EXPERTS, s // TOK_CHUNK, 1, TOK_CHUNK)
        posm_t = posm.reshape(b, N_EXPERTS, s).transpose(0, 2, 1)
        aff_c = aff_r.reshape(b, N_EXPERTS, s // TOK_CHUNK, 1, TOK_CHUNK)
        xe, gate = _gather(offs, h2, posm_c, aff_c, cap=cap)
        ye = _ffn(xe, gate, w_gate, w_up, w_down, l)
        x = _combine(offs, x1, posm_t, gt2, ye)
    return _final_norm(x, g_final[None], tt=tt)


def kernel(x, c, w_mod, b_mod, g_norm1, w_in, pool_w, pool_scale, gqa_q_gain, gqa_k_gain, mla_q_gain, mla_kv_gain,
           mla_w_uq, mla_w_ukv, w_out, g_norm2, w_router, w_gate, w_up, w_down, g_final):
    return _trunk(x, c, w_mod, b_mod, g_norm1, w_in, pool_w, pool_scale, gqa_q_gain, gqa_k_gain, mla_q_gain,
                  mla_kv_gain, mla_w_uq, mla_w_ukv, w_out, g_norm2, w_router, w_gate, w_up, w_down, g_final)
```

```python
import functools
import math

import numpy as np
import jax
import jax.numpy as jnp
from jax import lax
from jax.experimental import pallas as pl
from jax.experimental.pallas import tpu as pltpu

f32, bf16, i32 = jnp.float32, jnp.bfloat16, jnp.int32

D_MODEL = 1024
DEPTH = 4
GRID_W = 64
ROPE_THETA = 10000.0
EPS = 1e-6
POOL_DIM = 256
POOL_WINDOWS = (2, 4, 8, 16)
POOL_CH = 64
HEAD_DIM = 64
GQA_HEADS = 6
GQA_KV_HEADS = 2
GQA_GROUP = 3
MLA_HEADS = 6
MLA_NOPE_DIM = 64
MLA_ROPE_DIM = 32
MLA_QK_DIM = 96
MLA_V_DIM = 64
MLA_Q_RANK = 256
MLA_KV_RANK = 256
OFF_GQA_Q = POOL_DIM
OFF_GQA_K = OFF_GQA_Q + GQA_HEADS * HEAD_DIM
OFF_GQA_V = OFF_GQA_K + GQA_KV_HEADS * HEAD_DIM
OFF_MLA_Q = OFF_GQA_V + GQA_KV_HEADS * HEAD_DIM
OFF_MLA_KV = OFF_MLA_Q + MLA_Q_RANK
OFF_MLA_ROPE = OFF_MLA_KV + MLA_KV_RANK
IN_DIM = OFF_MLA_ROPE + MLA_ROPE_DIM
REST_DIM = IN_DIM - POOL_DIM
N_EXPERTS = 16
EC_CAPACITY = 2
D_FF = 2048

R_GQ = 0
R_GK = R_GQ + GQA_HEADS * HEAD_DIM
R_GV = R_GK + GQA_KV_HEADS * HEAD_DIM
R_MQ = R_GV + GQA_KV_HEADS * HEAD_DIM
R_MKV = R_MQ + MLA_Q_RANK
R_MR = R_MKV + MLA_KV_RANK

V7X_LANES = 128
V7X_VMEM_LIMIT_BYTES = 60000 * 1024
V7X_SUBLANES = 8
MXU_ROWS_PER_PUSH = 128
V_ROWS = MXU_ROWS_PER_PUSH
K_PAD = 16

TOK_TILE = 512
INPROJ_TILE = 1024
ATT_TQ = 512
GQA_TQ = 4096
MLA_TQ = 8192
ATT_TK = 512
TOK_CHUNK = 256
FAST_SLOTS = 64
SLOT_ALIGN_LOG2 = 4
GATHER_UNROLL = 64
GQA_CHUNKS_PER_TRIP = 4
MLA_CHUNKS_PER_TRIP = 4
FFN_ROW_SPLIT = 2
OUTPROJ_ROW_SPLIT = 4
OUTPROJ_TILE = 1024
FF_TILE = 512
COL_TILE = 512
COMBINE_TOK = 1024
LOG2E = math.log2(math.e)


def _cparams(sem, vmem=None):
    return pltpu.CompilerParams(dimension_semantics=sem, vmem_limit_bytes=vmem)


def _split_bf16(a):
    hi = a.astype(bf16)
    lo = (a - hi.astype(f32)).astype(bf16)
    return hi, lo


def _dot(a, b):
    return jnp.dot(a, b, preferred_element_type=f32)


def _dot3(a, b):
    ah, al = _split_bf16(a)
    bh, bl = _split_bf16(b)
    return _dot(ah, bh) + _dot(ah, bl) + _dot(al, bh)


def _mod_kernel(c_ref, w_ref, b_ref, o_ref):
    c = c_ref[...]
    act = c * (1.0 / (1.0 + jnp.exp(-c)))
    o_ref[0] = _dot3(act, w_ref[0]) + b_ref[0]


def _modulation(c_pad, w_mod, b_mod):
    depth, d, six_d = w_mod.shape
    rows = c_pad.shape[0]
    return pl.pallas_call(
        _mod_kernel,
        grid=(depth, six_d // d),
        in_specs=[
            pl.BlockSpec((rows, d), lambda l, j: (0, 0)),
            pl.BlockSpec((1, d, d), lambda l, j: (l, 0, j)),
            pl.BlockSpec((1, 1, d), lambda l, j: (l, 0, j)),
        ],
        out_specs=pl.BlockSpec((1, rows, d), lambda l, j: (l, 0, j)),
        out_shape=jax.ShapeDtypeStruct((depth, rows, six_d), f32),
        compiler_params=_cparams(("parallel", "parallel")),
        name="modulation",
    )(c_pad, w_mod, b_mod.reshape(depth, 1, six_d))


def _rms_rows(z, gain_col):
    r = lax.rsqrt(jnp.mean(z * z, axis=0, keepdims=True) + EPS)
    return z * r * gain_col


def _rope_rows(z, cos, sin):
    half = z.shape[0] // 2
    x1, x2 = z[:half], z[half:]
    return jnp.concatenate([x1 * cos - x2 * sin, x1 * sin + x2 * cos], axis=0)


def _inproj_kernel(x_ref, g_ref, sh_ref, sc_ref, wp_ref, wr_ref, gq_ref, gk_ref, gmq_ref, gmkv_ref,
                   wuq_ref, wukv_ref, cg_ref, sg_ref, cm_ref, sm_ref,
                   u_ref, qg_ref, kg_ref, kng_ref, vg_ref, qm_ref, km_ref, knm_ref, vm_ref, *, tk):
    x = x_ref[0]
    tt = x.shape[0]
    h = x * lax.rsqrt(jnp.mean(x * x, axis=-1, keepdims=True) + EPS) * g_ref[...]
    h = h * (1.0 + sc_ref[0]) + sh_ref[0]
    hb = h.astype(bf16)
    u_ref[0] = _dot(hb, wp_ref[...])
    zt = lax.dot_general(wr_ref[...], hb, (((1,), (1,)), ((), ())), preferred_element_type=f32)

    cg, sg, cm, sm = cg_ref[...], sg_ref[...], cm_ref[...], sm_ref[...]
    n_sub = tt // tk

    def put_k(ref, norm_ref, head, kt):
        kb = kt.astype(bf16)
        kf = kb.astype(f32)
        norm_ref[0, head] = jnp.sqrt(jnp.sum(kf * kf, axis=0, keepdims=True))
        pad = jnp.where(lax.broadcasted_iota(i32, (K_PAD, tt), 0) == 0, 1.0, 0.0)
        ke = jnp.concatenate([kt, pad], axis=0)
        ref[0, head] = ke.T.astype(bf16)

    v_pad = jnp.zeros((V_ROWS - MLA_V_DIM, tt), f32)

    def put_v(ref, head, vt):
        ve = jnp.concatenate([vt, v_pad], axis=0).astype(bf16)
        for j in range(n_sub):
            ref[0, head, j] = ve[:, j * tk:(j + 1) * tk]

    gq = gq_ref[...] * (HEAD_DIM ** -0.5 * LOG2E)
    gk = gk_ref[...]
    for hd in range(GQA_HEADS):
        q = _rms_rows(zt[R_GQ + hd * HEAD_DIM:R_GQ + (hd + 1) * HEAD_DIM], gq)
        qg_ref[0, hd] = _rope_rows(q, cg, sg).astype(bf16)
    for hk in range(GQA_KV_HEADS):
        k = _rms_rows(zt[R_GK + hk * HEAD_DIM:R_GK + (hk + 1) * HEAD_DIM], gk)
        put_k(kg_ref, kng_ref, hk, _rope_rows(k, cg, sg))
        put_v(vg_ref, hk, zt[R_GV + hk * HEAD_DIM:R_GV + (hk + 1) * HEAD_DIM])

    cq = _rms_rows(zt[R_MQ:R_MQ + MLA_Q_RANK], gmq_ref[...]).astype(bf16)
    qm = _dot(wuq_ref[...], cq) * (MLA_QK_DIM ** -0.5 * LOG2E)
    ckv = _rms_rows(zt[R_MKV:R_MKV + MLA_KV_RANK], gmkv_ref[...]).astype(bf16)
    kv = _dot(wukv_ref[...], ckv)
    k_rope = _rope_rows(zt[R_MR:R_MR + MLA_ROPE_DIM], cm, sm)
    for hd in range(MLA_HEADS):
        qh = qm[hd * MLA_QK_DIM:(hd + 1) * MLA_QK_DIM]
        qr = _rope_rows(qh[MLA_NOPE_DIM:], cm, sm)
        qm_ref[0, hd] = jnp.concatenate([qh[:MLA_NOPE_DIM], qr], axis=0).astype(bf16)
        kvh = kv[hd * (MLA_NOPE_DIM + MLA_V_DIM):(hd + 1) * (MLA_NOPE_DIM + MLA_V_DIM)]
        kh = jnp.concatenate([kvh[:MLA_NOPE_DIM], k_rope], axis=0)
        put_k(km_ref, knm_ref, hd, kh)
        put_v(vm_ref, hd, kvh[MLA_NOPE_DIM:])


def _inproj(x, g1, sh1, sc1, wp, wr, gq, gk, gmq, gmkv, wuq, wukv, cg, sg, cm, sm, *, tt, tk):
    b, s, d = x.shape
    n_t = s // tt
    n_sub = tt // tk
    full = lambda shape: pl.BlockSpec(shape, lambda bi, i: (0,) * len(shape))
    vec = pl.BlockSpec((1, 1, d), lambda bi, i: (bi, 0, 0))
    rope_g = pl.BlockSpec((HEAD_DIM // 2, tt), lambda bi, i: (0, i))
    rope_m = pl.BlockSpec((MLA_ROPE_DIM // 2, tt), lambda bi, i: (0, i))
    out_shapes = (
        jax.ShapeDtypeStruct((b, s, POOL_DIM), f32),
        jax.ShapeDtypeStruct((b, GQA_HEADS, HEAD_DIM, s), bf16),
        jax.ShapeDtypeStruct((b, GQA_KV_HEADS, s, HEAD_DIM + K_PAD), bf16),
        jax.ShapeDtypeStruct((b, GQA_KV_HEADS, 1, s), f32),
        jax.ShapeDtypeStruct((b, GQA_KV_HEADS, s // tk, V_ROWS, tk), bf16),
        jax.ShapeDtypeStruct((b, MLA_HEADS, MLA_QK_DIM, s), bf16),
        jax.ShapeDtypeStruct((b, MLA_HEADS, s, MLA_QK_DIM + K_PAD), bf16),
        jax.ShapeDtypeStruct((b, MLA_HEADS, 1, s), f32),
        jax.ShapeDtypeStruct((b, MLA_HEADS, s // tk, V_ROWS, tk), bf16),
    )
    out_specs = (
        pl.BlockSpec((1, tt, POOL_DIM), lambda bi, i: (bi, i, 0)),
        pl.BlockSpec((1, GQA_HEADS, HEAD_DIM, tt), lambda bi, i: (bi, 0, 0, i)),
        pl.BlockSpec((1, GQA_KV_HEADS, tt, HEAD_DIM + K_PAD), lambda bi, i: (bi, 0, i, 0)),
        pl.BlockSpec((1, GQA_KV_HEADS, 1, tt), lambda bi, i: (bi, 0, 0, i)),
        pl.BlockSpec((1, GQA_KV_HEADS, n_sub, V_ROWS, tk), lambda bi, i: (bi, 0, i, 0, 0)),
        pl.BlockSpec((1, MLA_HEADS, MLA_QK_DIM, tt), lambda bi, i: (bi, 0, 0, i)),
        pl.BlockSpec((1, MLA_HEADS, tt, MLA_QK_DIM + K_PAD), lambda bi, i: (bi, 0, i, 0)),
        pl.BlockSpec((1, MLA_HEADS, 1, tt), lambda bi, i: (bi, 0, 0, i)),
        pl.BlockSpec((1, MLA_HEADS, n_sub, V_ROWS, tk), lambda bi, i: (bi, 0, i, 0, 0)),
    )
    return pl.pallas_call(
        functools.partial(_inproj_kernel, tk=tk),
        grid=(b, n_t),
        in_specs=[
            pl.BlockSpec((1, tt, d), lambda bi, i: (bi, i, 0)),
            full((1, d)), vec, vec,
            full(wp.shape), full(wr.shape), full(gq.shape), full(gk.shape), full(gmq.shape), full(gmkv.shape),
            full(wuq.shape), full(wukv.shape), rope_g, rope_g, rope_m, rope_m,
        ],
        out_specs=out_specs,
        out_shape=out_shapes,
        compiler_params=_cparams(("parallel", "parallel"), V7X_VMEM_LIMIT_BYTES),
        name="inproj",
    )(x, g1, sh1, sc1, wp, wr, gq, gk, gmq, gmkv, wuq, wukv, cg, sg, cm, sm)


SAFE_LOGIT_BOUND = 50.0


def _attn_kernel(q_ref, k_ref, kn_ref, v_ref, o_ref, s0_ref, m_ref, l_ref, acc_ref, *, tk, n_chunks, dv, n_sub,
                 chunks_per_trip):
    group, tq = q_ref.shape[1], q_ref.shape[3]
    ts = tq // n_sub
    streams = [(h, j) for h in range(group) for j in range(n_sub)]
    qts = [q_ref[0, h, :, j * ts:(j + 1) * ts] for h, j in streams]

    def k_chunk(c):
        return k_ref[0, 0, pl.ds(pl.multiple_of(c * tk, tk), tk), :]

    def finish():
        for n, (h, j) in enumerate(streams):
            l = jnp.sum(l_ref[n], axis=0, keepdims=True)
            o_ref[0, h * dv:(h + 1) * dv, j * ts:(j + 1) * ts] = (acc_ref[n][:dv] / l).astype(o_ref.dtype)

    def row_sums(p):
        return jnp.sum(p.reshape(tk // V7X_SUBLANES, V7X_SUBLANES, ts), axis=0)

    k_max = jnp.max(kn_ref[0, 0], axis=-1, keepdims=True)
    bounds = []
    for qt in qts:
        qf = qt.astype(f32)
        bounds.append(jnp.sqrt(jnp.sum(qf * qf, axis=0, keepdims=True)) * k_max)
    worst = functools.reduce(jnp.maximum, [jnp.max(u, axis=-1, keepdims=True) for u in bounds])
    safe = worst[0, 0] <= SAFE_LOGIT_BOUND

    @pl.when(safe)
    def _():
        row0 = lax.broadcasted_iota(i32, (K_PAD, ts), 0) == 0
        qes = [jnp.concatenate([qt, jnp.where(row0, -c, 0.0).astype(qt.dtype)], axis=0) for qt, c in zip(qts, bounds)]
        acc_ref[...] = jnp.zeros(acc_ref.shape, f32)
        l_ref[...] = jnp.zeros(l_ref.shape, f32)
        s0_ref[...] = _dot(k_chunk(0), qes[0])
        per_trip = math.gcd(n_chunks, chunks_per_trip)
        order = [(dc, n) for dc in range(per_trip) for n in range(len(streams))]

        def step(i, carry):
            c = per_trip * i
            kcs = [k_chunk(jnp.minimum(c + dc, n_chunks - 1)) for dc in range(per_trip + 1)]
            vcs = [v_ref[0, 0, c + dc] for dc in range(per_trip)]
            s_cur = s0_ref[...]
            for idx, (dc, n) in enumerate(order):
                dc2, n2 = order[idx + 1] if idx + 1 < len(order) else (per_trip, 0)
                s_next = _dot(kcs[dc2], qes[n2])
                p = jnp.exp2(s_cur)
                l_ref[n] += row_sums(p)
                acc_ref[n] += _dot(vcs[dc], p.astype(bf16))
                s_cur = s_next
            s0_ref[...] = s_cur
            return carry

        lax.fori_loop(0, n_chunks // per_trip, step, 0)
        finish()

    @pl.when(jnp.logical_not(safe))
    def _():
        m_ref[...] = jnp.full(m_ref.shape, -jnp.inf, f32)
        acc_ref[...] = jnp.zeros(acc_ref.shape, f32)
        l_ref[...] = jnp.zeros(l_ref.shape, f32)
        qzs =[jnp.concatenate([qt, jnp.zeros((K_PAD, ts), qt.dtype)], axis=0) for qt in qts]

        def step(c, carry):
            kc = k_chunk(c)
            vc = v_ref[0, 0, c]
            for n, qz in enumerate(qzs):
                s = _dot(kc, qz)
                m = m_ref[n]
                m_new = jnp.maximum(m, jnp.max(s, axis=0, keepdims=True))
                p = jnp.exp2(s - m_new)
                alpha = jnp.exp2(m - m_new)
                l_ref[n] = l_ref[n] * alpha + row_sums(p)
                acc_ref[n] = acc_ref[n] * alpha + _dot(vc, p.astype(bf16))
                m_ref[n] = m_new
            return carry

        lax.fori_loop(0, n_chunks, step, 0)
        finish()


def _attention(qt, k, kn, vt, *, group, tq, n_sub, chunks_per_trip):
    b, hq, dq, s = qt.shape
    _, hk, n_chunks, v_rows, tk = vt.shape
    dv = MLA_V_DIM
    assert tq % n_sub == 0
    n_streams, ts = group * n_sub, tq // n_sub
    return pl.pallas_call(
        functools.partial(_attn_kernel, tk=tk, n_chunks=n_chunks, dv=dv, n_sub=n_sub, chunks_per_trip=chunks_per_trip),
        grid=(b, hk, s // tq),
        scratch_shapes=[
            pltpu.VMEM((tk, ts), f32),
            pltpu.VMEM((n_streams, 1, ts), f32),
            pltpu.VMEM((n_streams, V7X_SUBLANES, ts), f32),
            pltpu.VMEM((n_streams, v_rows, ts), f32),
        ],
        in_specs=[
            pl.BlockSpec((1, group, dq, tq), lambda bi, g, i: (bi, g, 0, i)),
            pl.BlockSpec((1, 1, s, dq + K_PAD), lambda bi, g, i: (bi, g, 0, 0)),
            pl.BlockSpec((1, 1, 1, s), lambda bi, g, i: (bi, g, 0, 0)),
            pl.BlockSpec((1, 1, n_chunks, v_rows, tk), lambda bi, g, i: (bi, g, 0, 0, 0)),
        ],
        out_specs=pl.BlockSpec((1, group * dv, tq), lambda bi, g, i: (bi, g, i)),
        out_shape=jax.ShapeDtypeStruct((b, hq * dv, s), bf16),
        compiler_params=_cparams(("parallel", "parallel", "parallel"), V7X_VMEM_LIMIT_BYTES),
        name="attention",
    )(qt, k, kn, vt)


POOL_HALO = 16


def _pool_kernel(up_ref, uc_ref, un_ref, w_ref, sc_ref, o_ref, *, seq):
    i = pl.program_id(1)
    n_t = pl.num_programs(1)
    cur = uc_ref[0]
    tp = cur.shape[0]
    prev = jnp.where(i > 0, up_ref[0], 0.0)
    nxt = jnp.where(i < n_t - 1, un_ref[0], 0.0)
    ext = jnp.concatenate([prev, cur, nxt], axis=0)
    t = i * tp + lax.broadcasted_iota(i32, (tp, 1), 0)
    lane = lax.broadcasted_iota(i32, (tp, POOL_DIM), 1)
    run, length = ext, 1
    wsum = cnt = None
    for gi, win in enumerate(POOL_WINDOWS):
        while length < win:
            run = run[:run.shape[0] - length] + run[length:]
            length *= 2
        lo = win // 2
        hi = win - 1 - lo
        mine = run[POOL_HALO - lo:POOL_HALO - lo + tp]
        n_valid = (jnp.minimum(t + hi, seq - 1) - jnp.maximum(t - lo, 0) + 1).astype(f32)
        if gi == 0:
            wsum, cnt = mine, jnp.broadcast_to(n_valid, (tp, POOL_DIM))
        else:
            in_later_group = lane >= gi * POOL_CH
            wsum = jnp.where(in_later_group, mine, wsum)
            cnt = jnp.where(in_later_group, n_valid, cnt)
    p = (wsum / cnt - cur).astype(bf16)
    o_ref[0] = (_dot(p, w_ref[...]) * sc_ref[...]).astype(o_ref.dtype)


def _pool(u, w_bd, scale, *, tp):
    b, s, c = u.shape
    n_t = s // tp
    r = tp // POOL_HALO
    return pl.pallas_call(
        functools.partial(_pool_kernel, seq=s),
        grid=(b, n_t),
        in_specs=[
            pl.BlockSpec((1, POOL_HALO, c), lambda bi, i: (bi, jnp.maximum(i * r - 1, 0), 0)),
            pl.BlockSpec((1, tp, c), lambda bi, i: (bi, i, 0)),
            pl.BlockSpec((1, POOL_HALO, c), lambda bi, i: (bi, jnp.minimum((i + 1) * r, s // POOL_HALO - 1), 0)),
            pl.BlockSpec(w_bd.shape, lambda bi, i: (0, 0)),
            pl.BlockSpec((1, c), lambda bi, i: (0, 0)),
        ],
        out_specs=pl.BlockSpec((1, tp, c), lambda bi, i: (bi, i, 0)),
        out_shape=jax.ShapeDtypeStruct((b, s, c), bf16),
        compiler_params=_cparams(("parallel", "parallel")),
        name="pool",
    )(u, u, u, w_bd, scale)


def _outproj_kernel(x_ref, yp_ref, og_ref, om_ref, wop_ref, wog_ref, wom_ref, gt_ref, g2_ref, sh_ref, sc_ref,
                    wrt_ref, x1_ref, h2_ref, aff_ref):
    tn = (((0,), (0,)), ((), ()))
    n_e = aff_ref.shape[2]
    w_hi, w_lo = _split_bf16(wrt_ref[...])
    w_hilo = jnp.concatenate([w_hi, w_lo], axis=1)
    rows = x_ref.shape[1] // OUTPROJ_ROW_SPLIT
    for r in range(OUTPROJ_ROW_SPLIT):
        sl = pl.ds(r * rows, rows)
        ls = pl.ds(r * rows, rows)
        y = _dot(yp_ref[0, sl, :], wop_ref[...])
        y = y + lax.dot_general(og_ref[0, :, ls], wog_ref[...], tn, preferred_element_type=f32)
        y = y + lax.dot_general(om_ref[0, :, ls], wom_ref[...], tn, preferred_element_type=f32)
        x1 = x_ref[0, sl, :] + gt_ref[0] * y
        x1_ref[0, sl, :] = x1
        h = x1 * lax.rsqrt(jnp.mean(x1 * x1, axis=-1, keepdims=True) + EPS) * g2_ref[...]
        h = h * (1.0 + sc_ref[0]) + sh_ref[0]
        h2_ref[0, sl, :] = h.astype(bf16)
        h_hi, h_lo = _split_bf16(h)
        two = _dot(h_hi, w_hilo)
        logits = two[:, :n_e] + two[:, n_e:] + _dot(h_lo, w_hi)
        ex = jnp.exp(logits - jnp.max(logits, axis=-1, keepdims=True))
        aff_ref[0, sl, :] = ex / jnp.sum(ex, axis=-1, keepdims=True)


def _outproj(x, ypool, og, om, wop, wog, wom, gt1, g2, sh2, sc2, w_router, *, tt):
    b, s, d = x.shape
    full = lambda shape: pl.BlockSpec(shape, lambda bi, i: (0,) * len(shape))
    vec = pl.BlockSpec((1, 1, d), lambda bi, i: (bi, 0, 0))
    n_e = w_router.shape[1]
    return pl.pallas_call(
        _outproj_kernel,
        grid=(b, s // tt),
        in_specs=[
            pl.BlockSpec((1, tt, d), lambda bi, i: (bi, i, 0)),
            pl.BlockSpec((1, tt, POOL_DIM), lambda bi, i: (bi, i, 0)),
            pl.BlockSpec((1, og.shape[1], tt), lambda bi, i: (bi, 0, i)),
            pl.BlockSpec((1, om.shape[1], tt), lambda bi, i: (bi, 0, i)),
            full(wop.shape), full(wog.shape), full(wom.shape),
            vec, full((1, d)), vec, vec, full(w_router.shape),
        ],
        out_specs=(
            pl.BlockSpec((1, tt, d), lambda bi, i: (bi, i, 0)),
            pl.BlockSpec((1, tt, d), lambda bi, i: (bi, i, 0)),
            pl.BlockSpec((1, tt, n_e), lambda bi, i: (bi, i, 0)),
        ),
        out_shape=(
            jax.ShapeDtypeStruct((b, s, d), f32),
            jax.ShapeDtypeStruct((b, s, d), bf16),
            jax.ShapeDtypeStruct((b, s, n_e), f32),
        ),
        compiler_params=_cparams(("parallel", "parallel"), V7X_VMEM_LIMIT_BYTES),
        name="outproj",
    )(x, ypool, og, om, wop, wog, wom, gt1, g2, sh2, sc2, w_router)


def _route_kernel(a_ref, posm_ref, pos_ref, *, cap):
    a = a_ref[0]
    n_e, nc, ln = a.shape
    bits = pltpu.bitcast(a, i32)

    def count(mask):
        c = jnp.sum(jnp.where(mask, 1.0, 0.0), axis=2, keepdims=True)
        return jnp.sum(c, axis=1, keepdims=True)

    thr = jnp.zeros((n_e, 1, 1), i32)
    for bit in range(30, -1, -1):
        cand = thr | (1 << bit)
        thr = jnp.where(count(bits >= cand) >= cap, cand, thr)
    gt = bits > thr
    eq = bits == thr
    need = cap - count(gt)

    r_i = lax.broadcasted_iota(i32, (ln, ln), 0)
    c_i = lax.broadcasted_iota(i32, (ln, ln), 1)
    tri_incl = jnp.where(r_i <= c_i, 1.0, 0.0).astype(bf16)
    r_c = lax.broadcasted_iota(i32, (nc, nc), 0)
    c_c = lax.broadcasted_iota(i32, (nc, nc), 1)
    tri_strict = jnp.where(c_c < r_c, 1.0, 0.0).astype(bf16)

    def excl_prefix(mask):
        x = jnp.where(mask, 1.0, 0.0)
        incl = _dot(x.astype(bf16).reshape(n_e * nc, ln), tri_incl).reshape(n_e, nc, ln)
        tot = jnp.broadcast_to(incl[:, :, ln - 1:ln], (n_e, nc, ln))
        tot_hi = tot.astype(bf16)
        offs = [_dot(tri_strict, tot_hi[e]) for e in range(n_e)]
        return jnp.stack(offs, axis=0) + incl - x

    sel = gt | (eq & (excl_prefix(eq) < need))
    pos = excl_prefix(sel).astype(i32)
    pos_ref[0] = pos
    posm_ref[0] = jnp.where(sel, pos, -1)


def _route(aff_r, *, cap):
    b, n_e, nc, ln = aff_r.shape
    spec = pl.BlockSpec((1, n_e, nc, ln), lambda bi: (bi, 0, 0, 0))
    return pl.pallas_call(
        functools.partial(_route_kernel, cap=cap),
        grid=(b,),
        in_specs=[spec],
        out_specs=(spec, spec),
        out_shape=(jax.ShapeDtypeStruct(aff_r.shape, i32), jax.ShapeDtypeStruct(aff_r.shape, i32)),
        compiler_params=_cparams(("parallel",), V7X_VMEM_LIMIT_BYTES),
        name="route",
    )(aff_r)


def _window_start(lo, width, cap):
    aligned = lax.shift_left(lax.shift_right_logical(lo, SLOT_ALIGN_LOG2), SLOT_ALIGN_LOG2)
    return pl.multiple_of(jnp.minimum(aligned, cap - width), 1 << SLOT_ALIGN_LOG2)


def _window_widths(cap, usual=FAST_SLOTS):
    return min(usual, cap), min(TOK_CHUNK + (1 << SLOT_ALIGN_LOG2), cap)


def _gather_kernel(offs_ref, h_ref, posm_ref, aff_ref, xe_ref, gate_ref, *, n_off, n_chunks):
    b, e = pl.program_id(0), pl.program_id(1)
    base = (b * pl.num_programs(1) + e) * n_off
    step = TOK_CHUNK // V7X_LANES
    cap = xe_ref.shape[2]
    fast_w, slow_w = _window_widths(cap, usual=MXU_ROWS_PER_PUSH)
    xe_ref[...] = jnp.zeros_like(xe_ref)
    gate_ref[...] = jnp.zeros_like(gate_ref)

    def misfit(c, bad):
        lo = offs_ref[base + c * step]
        hi = offs_ref[base + (c + 1) * step]
        return bad + (hi - _window_start(lo, fast_w, cap) > fast_w).astype(i32)

    bad = lax.fori_loop(0, n_chunks, misfit, jnp.int32(0))

    def run(width):
        slot_iota = lax.broadcasted_iota(i32, (width, TOK_CHUNK), 0)

        def chunk(c, carry):
            w = _window_start(offs_ref[base + c * step], width, cap)
            tok0 = pl.multiple_of(c * TOK_CHUNK, TOK_CHUNK)
            pr = posm_ref[0, 0, c]
            hit = pr == slot_iota + w
            rows = _dot(jnp.where(hit, 1.0, 0.0).astype(bf16), h_ref[0, pl.ds(tok0, TOK_CHUNK), :])
            xe_ref[0, 0, pl.ds(w, width), :] = xe_ref[0, 0, pl.ds(w, width), :] + rows.astype(xe_ref.dtype)
            gate_ref[0, 0, pl.ds(w, width), :] += jnp.sum(jnp.where(hit, aff_ref[0, 0, c], 0.0), axis=1, keepdims=True)
            return carry

        lax.fori_loop(0, n_chunks, chunk, 0, unroll=math.gcd(n_chunks, GATHER_UNROLL))

    @pl.when(bad == 0)
    def _():
        run(fast_w)

    @pl.when(bad != 0)
    def _():
        run(slow_w)


def _gather(offs, h2, posm_c, aff_c, *, cap):
    b, s, d = h2.shape
    n_e = posm_c.shape[1]
    n_chunks = s // TOK_CHUNK
    n_off = s // V7X_LANES + 1
    chunked = pl.BlockSpec((1, 1, n_chunks, 1, TOK_CHUNK), lambda bi, e, offs: (bi, e, 0, 0, 0))
    return pl.pallas_call(
        functools.partial(_gather_kernel, n_off=n_off, n_chunks=n_chunks),
        grid_spec=pltpu.PrefetchScalarGridSpec(
            num_scalar_prefetch=1,
            grid=(b, n_e),
            in_specs=[
                pl.BlockSpec((1, s, d), lambda bi, e, offs: (bi, 0, 0), pipeline_mode=pl.Buffered(1)),
                chunked, chunked,
            ],
            out_specs=(pl.BlockSpec((1, 1, cap, d), lambda bi, e, offs: (bi, e, 0, 0)),
                       pl.BlockSpec((1, 1, cap, 1), lambda bi, e, offs: (bi, e, 0, 0))),
        ),
        out_shape=(jax.ShapeDtypeStruct((b, n_e, cap, d), bf16), jax.ShapeDtypeStruct((b, n_e, cap, 1), f32)),
        compiler_params=_cparams(("arbitrary", "arbitrary"), V7X_VMEM_LIMIT_BYTES),
        name="gather",
    )(offs, h2, posm_c, aff_c)


def _ffn_kernel(x_ref, gate_ref, wg_ref, wu_ref, wd_ref, o_ref, acc_ref):
    @pl.when(pl.program_id(2) == 0)
    def _():
        acc_ref[...] = jnp.zeros_like(acc_ref)

    wg, wu, wd = wg_ref[0, 0].astype(bf16), wu_ref[0, 0].astype(bf16), wd_ref[0, 0].astype(bf16)
    rows = x_ref.shape[2] // FFN_ROW_SPLIT
    for r in range(FFN_ROW_SPLIT):
        sl = pl.ds(r * rows, rows)
        x = x_ref[0, 0, sl, :]
        a = _dot(x, wg)
        u = _dot(x, wu)
        hmid = (a * (1.0 / (1.0 + jnp.exp(-a))) * u).astype(bf16)
        total = acc_ref[sl, :] + _dot(hmid, wd)
        acc_ref[sl, :] = total
        o_ref[0, 0, sl, :] = (total * gate_ref[0, 0, sl, :]).astype(o_ref.dtype)


def _ffn(xe, gate, w_gate, w_up, w_down, layer):
    b, n_e, cap, d = xe.shape
    d_ff = w_gate.shape[-1]
    n_f = d_ff // FF_TILE
    return pl.pallas_call(
        _ffn_kernel,
        grid=(n_e, b, n_f),
        in_specs=[
            pl.BlockSpec((1, 1, cap, d), lambda e, bi, f: (bi, e, 0, 0)),
            pl.BlockSpec((1, 1, cap, 1), lambda e, bi, f: (bi, e, 0, 0)),
            pl.BlockSpec((1, 1, d, FF_TILE), lambda e, bi, f: (layer, e, 0, f)),
            pl.BlockSpec((1, 1, d, FF_TILE), lambda e, bi, f: (layer, e, 0, f)),
            pl.BlockSpec((1, 1, FF_TILE, d), lambda e, bi, f: (layer, e, f, 0)),
        ],
        out_specs=pl.BlockSpec((1, 1, cap, d), lambda e, bi, f: (bi, e, 0, 0)),
        out_shape=jax.ShapeDtypeStruct((b, n_e, cap, d), bf16),
        scratch_shapes=[pltpu.VMEM((cap, d), f32)],
        compiler_params=_cparams(("parallel", "parallel", "arbitrary"), V7X_VMEM_LIMIT_BYTES),
        name="expert_ffn",
    )(xe, gate, w_gate, w_up, w_down)


def _combine_kernel(offs_ref, x_ref, posm_ref, gt_ref, ye_ref, o_ref, *, n_off):
    for sub in range(x_ref.shape[1] // TOK_CHUNK):
        _combine_chunk(offs_ref, x_ref, posm_ref, gt_ref, ye_ref, o_ref, n_off=n_off, sub=sub)


def _combine_chunk(offs_ref, x_ref, posm_ref, gt_ref, ye_ref, o_ref, *, n_off, sub):
    b = pl.program_id(0)
    i = pl.program_id(2) * (x_ref.shape[1] // TOK_CHUNK) + sub
    tok = pl.ds(sub * TOK_CHUNK, TOK_CHUNK)
    n_e, cap = ye_ref.shape[1], ye_ref.shape[2]
    fast_w, slow_w = _window_widths(cap)
    step = TOK_CHUNK // V7X_LANES
    posm = posm_ref[0, tok, :]
    los = [offs_ref[(b * n_e + e) * n_off + i * step] for e in range(n_e)]
    his = [offs_ref[(b * n_e + e) * n_off + (i + 1) * step] for e in range(n_e)]
    bad = functools.reduce(
        lambda a, c: a + c, [(his[e] - _window_start(los[e], fast_w, cap) > fast_w).astype(i32) for e in range(n_e)])
    paired = 2 * fast_w == V7X_LANES and n_e % 2 == 0

    def run_stacked():
        lane = lax.broadcasted_iota(i32, (TOK_CHUNK, V7X_LANES), 1)
        ws = [_window_start(los[e], fast_w, cap) for e in range(n_e)]
        tiles, rows = [], []
        for e in range(0, n_e, 2):
            target = jnp.where(lane < fast_w, posm[:, e:e + 1] - ws[e], posm[:, e + 1:e + 2] - ws[e + 1] + fast_w)
            tiles.append(jnp.where(target == lane, 1.0, 0.0).astype(bf16))
            rows += [ye_ref[0, e, pl.ds(ws[e], fast_w), :], ye_ref[0, e + 1, pl.ds(ws[e + 1], fast_w), :]]
        total = _dot(jnp.concatenate(tiles, axis=1), jnp.concatenate(rows, axis=0))
        o_ref[0, tok, :] = x_ref[0, tok, :] + gt_ref[0] * total

    def run_per_expert(width):
        slot_iota = lax.broadcasted_iota(i32, (TOK_CHUNK, width), 1)
        total = jnp.zeros((TOK_CHUNK, o_ref.shape[2]), f32)
        for e in range(n_e):
            w = _window_start(los[e], width, cap)
            onehot = jnp.where(posm[:, e:e + 1] == slot_iota + w, 1.0, 0.0).astype(bf16)
            total = total + _dot(onehot, ye_ref[0, e, pl.ds(w, width), :])
        o_ref[0, tok, :] = x_ref[0, tok, :] + gt_ref[0] * total

    @pl.when(bad == 0)
    def _():
        run_stacked() if paired else run_per_expert(fast_w)

    @pl.when(bad != 0)
    def _():
        run_per_expert(slow_w)


def _combine(offs, x1, posm_t, gt2, ye):
    b, s, d = x1.shape
    n_e, cap = ye.shape[1], ye.shape[2]
    n_off = s // V7X_LANES + 1
    tok = math.gcd(s, COMBINE_TOK)
    return pl.pallas_call(
        functools.partial(_combine_kernel, n_off=n_off),
        grid_spec=pltpu.PrefetchScalarGridSpec(
            num_scalar_prefetch=1,
            grid=(b, d // COL_TILE, s // tok),
            in_specs=[
                pl.BlockSpec((1, tok, COL_TILE), lambda bi, j, i, offs: (bi, i, j)),
                pl.BlockSpec((1, tok, n_e), lambda bi, j, i, offs: (bi, i, 0)),
                pl.BlockSpec((1, 1, COL_TILE), lambda bi, j, i, offs: (bi, 0, j)),
                pl.BlockSpec((1, n_e, cap, COL_TILE), lambda bi, j, i, offs: (bi, 0, 0, j),
                             pipeline_mode=pl.Buffered(1)),
            ],
            out_specs=pl.BlockSpec((1, tok, COL_TILE), lambda bi, j, i, offs: (bi, i, j)),
        ),
        out_shape=jax.ShapeDtypeStruct((b, s, d), f32),
        compiler_params=_cparams(("arbitrary", "arbitrary", "arbitrary"), V7X_VMEM_LIMIT_BYTES),
        name="combine",
    )(offs, x1, posm_t, gt2, ye)


def _final_kernel(x_ref, g_ref, o_ref):
    x = x_ref[0]
    o_ref[0] = x * lax.rsqrt(jnp.mean(x * x, axis=-1, keepdims=True) + EPS) * g_ref[...]


def _final_norm(x, g, *, tt):
    b, s, d = x.shape
    return pl.pallas_call(
        _final_kernel,
        grid=(b, s // tt),
        in_specs=[pl.BlockSpec((1, tt, d), lambda bi, i: (bi, i, 0)), pl.BlockSpec((1, d), lambda bi, i: (0, 0))],
        out_specs=pl.BlockSpec((1, tt, d), lambda bi, i: (bi, i, 0)),
        out_shape=jax.ShapeDtypeStruct((b, s, d), f32),
        compiler_params=_cparams(("parallel", "parallel")),
        name="final_norm",
    )(x, g)


def _deinterleave(n):
    return np.concatenate([np.arange(0, n, 2), np.arange(1, n, 2)])


def _rope_tables_t(n, d_rot):
    n_rows = n // GRID_W
    row = jnp.repeat(jnp.arange(n_rows, dtype=f32), GRID_W)
    col = jnp.tile(jnp.arange(GRID_W, dtype=f32), n_rows)
    n_freq = d_rot // 4
    inv_freq = ROPE_THETA ** (-jnp.arange(n_freq, dtype=f32) / n_freq)
    ang = jnp.concatenate([row[:, None] * inv_freq, col[:, None] * inv_freq], axis=-1)
    return jnp.cos(ang).T, jnp.sin(ang).T


def _rest_columns():
    p64, p32 = _deinterleave(HEAD_DIM), _deinterleave(MLA_ROPE_DIM)
    cols = [OFF_GQA_Q + h * HEAD_DIM + p64 for h in range(GQA_HEADS)]
    cols += [OFF_GQA_K + h * HEAD_DIM + p64 for h in range(GQA_KV_HEADS)]
    cols += [np.arange(OFF_GQA_V, IN_DIM - MLA_ROPE_DIM), OFF_MLA_ROPE + p32]
    return np.concatenate(cols)


def _uq_columns():
    p32 = _deinterleave(MLA_ROPE_DIM)
    cols = []
    for h in range(MLA_HEADS):
        cols += [h * MLA_QK_DIM + np.arange(MLA_NOPE_DIM), h * MLA_QK_DIM + MLA_NOPE_DIM + p32]
    return np.concatenate(cols)


def _block_diag(w):
    depth, g, c, _ = w.shape
    rows = []
    for i in range(g):
        blocks = [w[:, i] if j == i else jnp.zeros((depth, c, c), w.dtype) for j in range(g)]
        rows.append(jnp.concatenate(blocks, axis=2))
    return jnp.concatenate(rows, axis=1)


def _trunk(x, c, w_mod, b_mod, g_norm1, w_in, pool_w, pool_scale, gqa_q_gain, gqa_k_gain, mla_q_gain, mla_kv_gain,
           mla_w_uq, mla_w_ukv, w_out, g_norm2, w_router, w_gate, w_up, w_down, g_final):
    b, s, d = x.shape
    depth = w_mod.shape[0]
    cap = (EC_CAPACITY * s) // N_EXPERTS
    tt = min(TOK_TILE, s)
    tq = min(ATT_TQ, s)
    tk = min(ATT_TK, s)
    nc = s // V7X_LANES

    mod_rows = 8
    c_pad = jnp.zeros((mod_rows, d), f32).at[:b].set(c)
    mod = _modulation(c_pad, w_mod, b_mod)[:, :b].reshape(depth, b, 6, 1, d)

    cg, sg = _rope_tables_t(s, HEAD_DIM)
    cm, sm = _rope_tables_t(s, MLA_ROPE_DIM)
    p64 = _deinterleave(HEAD_DIM)
    rest_cols, uq_cols = _rest_columns(), _uq_columns()

    wp_all = w_in[:, :, :POOL_DIM].astype(bf16)
    wr_all = jnp.swapaxes(w_in[:, :, rest_cols], 1, 2).astype(bf16)
    wuq_all = jnp.swapaxes(mla_w_uq[:, :, uq_cols], 1, 2).astype(bf16)
    wukv_all = jnp.swapaxes(mla_w_ukv, 1, 2).astype(bf16)
    gq_all, gk_all = gqa_q_gain[:, p64, None], gqa_k_gain[:, p64, None]
    gmq_all, gmkv_all = mla_q_gain[:, :, None], mla_kv_gain[:, :, None]
    pw_all = _block_diag(pool_w).astype(bf16)
    wo_all = w_out.astype(bf16)
    n_g = GQA_HEADS * HEAD_DIM

    for l in range(depth):
        sh1, sc1, gt1, sh2, sc2, gt2 = (mod[l, :, k] for k in range(6))
        u, qg, kg, kng, vg, qm, km, knm, vm = _inproj(
            x, g_norm1[l][None], sh1, sc1, wp_all[l], wr_all[l], gq_all[l], gk_all[l], gmq_all[l], gmkv_all[l],
            wuq_all[l], wukv_all[l], cg, sg, cm, sm, tt=max(min(INPROJ_TILE, s), tk), tk=tk)
        og = _attention(qg, kg, kng, vg, group=GQA_GROUP, tq=min(GQA_TQ, s), n_sub=min(GQA_TQ, s) // tq,
                        chunks_per_trip=GQA_CHUNKS_PER_TRIP)
        om = _attention(qm, km, knm, vm, group=1, tq=min(MLA_TQ, s), n_sub=min(MLA_TQ, s) // tq,
                        chunks_per_trip=MLA_CHUNKS_PER_TRIP)
        ypool = _pool(u, pw_all[l], pool_scale[l][None], tp=tt)
        wo = wo_all[l]
        x1, h2, aff = _outproj(x, ypool, og, om, wo[:POOL_DIM], wo[POOL_DIM:POOL_DIM + n_g], wo[POOL_DIM + n_g:],
                               gt1, g_norm2[l][None], sh2, sc2, w_router[l], tt=min(OUTPROJ_TILE, s))

        aff_r = aff.transpose(0, 2, 1).reshape(b, N_EXPERTS, nc, V7X_LANES)
        posm, pos = _route(aff_r, cap=cap)
        offs = jnp.concatenate([pos[..., 0], jnp.full((b, N_EXPERTS, 1), cap, i32)], axis=-1).reshape(-1)
        posm_c = posm.reshape(b, N_EXPERTS, s // TOK_CHUNK, 1, TOK_CHUNK)
        posm_t = posm.reshape(b, N_EXPERTS, s).transpose(0, 2, 1)
        aff_c = aff_r.reshape(b, N_EXPERTS, s // TOK_CHUNK, 1, TOK_CHUNK)
        xe, gate = _gather(offs, h2, posm_c, aff_c, cap=cap)
        ye = _ffn(xe, gate, w_gate, w_up, w_down, l)
        x = _combine(offs, x1, posm_t, gt2, ye)
    return _final_norm(x, g_final[None], tt=tt)


def kernel(x, c, w_mod, b_mod, g_norm1, w_in, pool_w, pool_scale, gqa_q_gain, gqa_k_gain, mla_q_gain, mla_kv_gain,
           mla_w_uq, mla_w_ukv, w_out, g_norm2, w_router, w_gate, w_up, w_down, g_final):
    return _trunk(x, c, w_mod, b_mod, g_norm1, w_in, pool_w, pool_scale, gqa_q_gain, gqa_k_gain, mla_q_gain,
                  mla_kv_gain, mla_w_uq, mla_w_ukv, w_out, g_norm2, w_router, w_gate, w_up, w_down, g_final)
```

```python
import functools
import math

import numpy as np
import jax
import jax.numpy as jnp
from jax import lax
from jax.experimental import pallas as pl
from jax.experimental.pallas import tpu as pltpu

f32, bf16, i32 = jnp.float32, jnp.bfloat16, jnp.int32

D_MODEL = 1024
DEPTH = 4
GRID_W = 64
ROPE_THETA = 10000.0
EPS = 1e-6
POOL_DIM = 256
POOL_WINDOWS = (2, 4, 8, 16)
POOL_CH = 64
HEAD_DIM = 64
GQA_HEADS = 6
GQA_KV_HEADS = 2
GQA_GROUP = 3
MLA_HEADS = 6
MLA_NOPE_DIM = 64
MLA_ROPE_DIM = 32
MLA_QK_DIM = 96
MLA_V_DIM = 64
MLA_Q_RANK = 256
MLA_KV_RANK = 256
OFF_GQA_Q = POOL_DIM
OFF_GQA_K = OFF_GQA_Q + GQA_HEADS * HEAD_DIM
OFF_GQA_V = OFF_GQA_K + GQA_KV_HEADS * HEAD_DIM
OFF_MLA_Q = OFF_GQA_V + GQA_KV_HEADS * HEAD_DIM
OFF_MLA_KV = OFF_MLA_Q + MLA_Q_RANK
OFF_MLA_ROPE = OFF_MLA_KV + MLA_KV_RANK
IN_DIM = OFF_MLA_ROPE + MLA_ROPE_DIM
REST_DIM = IN_DIM - POOL_DIM
N_EXPERTS = 16
EC_CAPACITY = 2
D_FF = 2048

R_GQ = 0
R_GK = R_GQ + GQA_HEADS * HEAD_DIM
R_GV = R_GK + GQA_KV_HEADS * HEAD_DIM
R_MQ = R_GV + GQA_KV_HEADS * HEAD_DIM
R_MKV = R_MQ + MLA_Q_RANK
R_MR = R_MKV + MLA_KV_RANK

V7X_LANES = 128
V7X_VMEM_LIMIT_BYTES = 60000 * 1024
V7X_SUBLANES = 8
MXU_ROWS_PER_PUSH = 128
V_ROWS = MXU_ROWS_PER_PUSH
K_PAD = 16

TOK_TILE = 512
INPROJ_TILE = 1024
ATT_TQ = 512
GQA_TQ = 4096
MLA_TQ = 8192
ATT_TK = 512
TOK_CHUNK = 256
FAST_SLOTS = 64
SLOT_ALIGN_LOG2 = 4
GATHER_UNROLL = 64
GQA_CHUNKS_PER_TRIP = 4
MLA_CHUNKS_PER_TRIP = 4
FFN_ROW_SPLIT = 2
OUTPROJ_ROW_SPLIT = 4
OUTPROJ_TILE = 1024
FF_TILE = 512
COL_TILE = 512
COMBINE_TOK = 1024
LOG2E = math.log2(math.e)


def _cparams(sem, vmem=None):
    return pltpu.CompilerParams(dimension_semantics=sem, vmem_limit_bytes=vmem)


def _split_bf16(a):
    hi = a.astype(bf16)
    lo = (a - hi.astype(f32)).astype(bf16)
    return hi, lo


def _dot(a, b):
    return jnp.dot(a, b, preferred_element_type=f32)


def _dot3(a, b):
    ah, al = _split_bf16(a)
    bh, bl = _split_bf16(b)
    return _dot(ah, bh) + _dot(ah, bl) + _dot(al, bh)


def _mod_kernel(c_ref, w_ref, b_ref, o_ref):
    c = c_ref[...]
    act = c * (1.0 / (1.0 + jnp.exp(-c)))
    o_ref[0] = _dot3(act, w_ref[0]) + b_ref[0]


def _modulation(c_pad, w_mod, b_mod):
    depth, d, six_d = w_mod.shape
    rows = c_pad.shape[0]
    return pl.pallas_call(
        _mod_kernel,
        grid=(depth, six_d // d),
        in_specs=[
            pl.BlockSpec((rows, d), lambda l, j: (0, 0)),
            pl.BlockSpec((1, d, d), lambda l, j: (l, 0, j)),
            pl.BlockSpec((1, 1, d), lambda l, j: (l, 0, j)),
        ],
        out_specs=pl.BlockSpec((1, rows, d), lambda l, j: (l, 0, j)),
        out_shape=jax.ShapeDtypeStruct((depth, rows, six_d), f32),
        compiler_params=_cparams(("parallel", "parallel")),
        name="modulation",
    )(c_pad, w_mod, b_mod.reshape(depth, 1, six_d))


def _rms_rows(z, gain_col):
    r = lax.rsqrt(jnp.mean(z * z, axis=0, keepdims=True) + EPS)
    return z * r * gain_col


def _rope_rows(z, cos, sin):
    half = z.shape[0] // 2
    x1, x2 = z[:half], z[half:]
    return jnp.concatenate([x1 * cos - x2 * sin, x1 * sin + x2 * cos], axis=0)


def _inproj_kernel(x_ref, g_ref, sh_ref, sc_ref, wp_ref, wr_ref, gq_ref, gk_ref, gmq_ref, gmkv_ref,
                   wuq_ref, wukv_ref, cg_ref, sg_ref, cm_ref, sm_ref,
                   u_ref, qg_ref, kg_ref, kng_ref, vg_ref, qm_ref, km_ref, knm_ref, vm_ref, *, tk):
    x = x_ref[0]
    tt = x.shape[0]
    h = x * lax.rsqrt(jnp.mean(x * x, axis=-1, keepdims=True) + EPS) * g_ref[...]
    h = h * (1.0 + sc_ref[0]) + sh_ref[0]
    hb = h.astype(bf16)
    u_ref[0] = _dot(hb, wp_ref[...])
    zt = lax.dot_general(wr_ref[...], hb, (((1,), (1,)), ((), ())), preferred_element_type=f32)

    cg, sg, cm, sm = cg_ref[...], sg_ref[...], cm_ref[...], sm_ref[...]
    n_sub = tt // tk

    def put_k(ref, norm_ref, head, kt):
        kb = kt.astype(bf16)
        kf = kb.astype(f32)
        norm_ref[0, head] = jnp.sqrt(jnp.sum(kf * kf, axis=0, keepdims=True))
        pad = jnp.where(lax.broadcasted_iota(i32, (K_PAD, tt), 0) == 0, 1.0, 0.0)
        ke = jnp.concatenate([kt, pad], axis=0)
        ref[0, head] = ke.T.astype(bf16)

    v_pad = jnp.zeros((V_ROWS - MLA_V_DIM, tt), f32)

    def put_v(ref, head, vt):
        ve = jnp.concatenate([vt, v_pad], axis=0).astype(bf16)
        for j in range(n_sub):
            ref[0, head, j] = ve[:, j * tk:(j + 1) * tk]

    gq = gq_ref[...] * (HEAD_DIM ** -0.5 * LOG2E)
    gk = gk_ref[...]
    for hd in range(GQA_HEADS):
        q = _rms_rows(zt[R_GQ + hd * HEAD_DIM:R_GQ + (hd + 1) * HEAD_DIM], gq)
        qg_ref[0, hd] = _rope_rows(q, cg, sg).astype(bf16)
    for hk in range(GQA_KV_HEADS):
        k = _rms_rows(zt[R_GK + hk * HEAD_DIM:R_GK + (hk + 1) * HEAD_DIM], gk)
        put_k(kg_ref, kng_ref, hk, _rope_rows(k, cg, sg))
        put_v(vg_ref, hk, zt[R_GV + hk * HEAD_DIM:R_GV + (hk + 1) * HEAD_DIM])

    cq = _rms_rows(zt[R_MQ:R_MQ + MLA_Q_RANK], gmq_ref[...]).astype(bf16)
    qm = _dot(wuq_ref[...], cq) * (MLA_QK_DIM ** -0.5 * LOG2E)
    ckv = _rms_rows(zt[R_MKV:R_MKV + MLA_KV_RANK], gmkv_ref[...]).astype(bf16)
    kv = _dot(wukv_ref[...], ckv)
    k_rope = _rope_rows(zt[R_MR:R_MR + MLA_ROPE_DIM], cm, sm)
    for hd in range(MLA_HEADS):
        qh = qm[hd * MLA_QK_DIM:(hd + 1) * MLA_QK_DIM]
        qr = _rope_rows(qh[MLA_NOPE_DIM:], cm, sm)
        qm_ref[0, hd] = jnp.concatenate([qh[:MLA_NOPE_DIM], qr], axis=0).astype(bf16)
        kvh = kv[hd * (MLA_NOPE_DIM + MLA_V_DIM):(hd + 1) * (MLA_NOPE_DIM + MLA_V_DIM)]
        kh = jnp.concatenate([kvh[:MLA_NOPE_DIM], k_rope], axis=0)
        put_k(km_ref, knm_ref, hd, kh)
        put_v(vm_ref, hd, kvh[MLA_NOPE_DIM:])


def _inproj(x, g1, sh1, sc1, wp, wr, gq, gk, gmq, gmkv, wuq, wukv, cg, sg, cm, sm, *, tt, tk):
    b, s, d = x.shape
    n_t = s // tt
    n_sub = tt // tk
    full = lambda shape: pl.BlockSpec(shape, lambda bi, i: (0,) * len(shape))
    vec = pl.BlockSpec((1, 1, d), lambda bi, i: (bi, 0, 0))
    rope_g = pl.BlockSpec((HEAD_DIM // 2, tt), lambda bi, i: (0, i))
    rope_m = pl.BlockSpec((MLA_ROPE_DIM // 2, tt), lambda bi, i: (0, i))
    out_shapes = (
        jax.ShapeDtypeStruct((b, s, POOL_DIM), f32),
        jax.ShapeDtypeStruct((b, GQA_HEADS, HEAD_DIM, s), bf16),
        jax.ShapeDtypeStruct((b, GQA_KV_HEADS, s, HEAD_DIM + K_PAD), bf16),
        jax.ShapeDtypeStruct((b, GQA_KV_HEADS, 1, s), f32),
        jax.ShapeDtypeStruct((b, GQA_KV_HEADS, s // tk, V_ROWS, tk), bf16),
        jax.ShapeDtypeStruct((b, MLA_HEADS, MLA_QK_DIM, s), bf16),
        jax.ShapeDtypeStruct((b, MLA_HEADS, s, MLA_QK_DIM + K_PAD), bf16),
        jax.ShapeDtypeStruct((b, MLA_HEADS, 1, s), f32),
        jax.ShapeDtypeStruct((b, MLA_HEADS, s // tk, V_ROWS, tk), bf16),
    )
    out_specs = (
        pl.BlockSpec((1, tt, POOL_DIM), lambda bi, i: (bi, i, 0)),
        pl.BlockSpec((1, GQA_HEADS, HEAD_DIM, tt), lambda bi, i: (bi, 0, 0, i)),
        pl.BlockSpec((1, GQA_KV_HEADS, tt, HEAD_DIM + K_PAD), lambda bi, i: (bi, 0, i, 0)),
        pl.BlockSpec((1, GQA_KV_HEADS, 1, tt), lambda bi, i: (bi, 0, 0, i)),
        pl.BlockSpec((1, GQA_KV_HEADS, n_sub, V_ROWS, tk), lambda bi, i: (bi, 0, i, 0, 0)),
        pl.BlockSpec((1, MLA_HEADS, MLA_QK_DIM, tt), lambda bi, i: (bi, 0, 0, i)),
        pl.BlockSpec((1, MLA_HEADS, tt, MLA_QK_DIM + K_PAD), lambda bi, i: (bi, 0, i, 0)),
        pl.BlockSpec((1, MLA_HEADS, 1, tt), lambda bi, i: (bi, 0, 0, i)),
        pl.BlockSpec((1, MLA_HEADS, n_sub, V_ROWS, tk), lambda bi, i: (bi, 0, i, 0, 0)),
    )
    return pl.pallas_call(
        functools.partial(_inproj_kernel, tk=tk),
        grid=(b, n_t),
        in_specs=[
            pl.BlockSpec((1, tt, d), lambda bi, i: (bi, i, 0)),
            full((1, d)), vec, vec,
            full(wp.shape), full(wr.shape), full(gq.shape), full(gk.shape), full(gmq.shape), full(gmkv.shape),
            full(wuq.shape), full(wukv.shape), rope_g, rope_g, rope_m, rope_m,
        ],
        out_specs=out_specs,
        out_shape=out_shapes,
        compiler_params=_cparams(("parallel", "parallel"), V7X_VMEM_LIMIT_BYTES),
        name="inproj",
    )(x, g1, sh1, sc1, wp, wr, gq, gk, gmq, gmkv, wuq, wukv, cg, sg, cm, sm)


SAFE_LOGIT_BOUND = 50.0


def _attn_kernel(q_ref, k_ref, kn_ref, v_ref, o_ref, s0_ref, m_ref, l_ref, acc_ref, *, tk, n_chunks, dv, n_sub,
                 chunks_per_trip):
    group, tq = q_ref.shape[1], q_ref.shape[3]
    ts = tq // n_sub
    streams = [(h, j) for h in range(group) for j in range(n_sub)]
    qts = [q_ref[0, h, :, j * ts:(j + 1) * ts] for h, j in streams]

    def k_chunk(c):
        return k_ref[0, 0, pl.ds(pl.multiple_of(c * tk, tk), tk), :]

    def finish():
        for n, (h, j) in enumerate(streams):
            l = jnp.sum(l_ref[n], axis=0, keepdims=True)
            o_ref[0, h * dv:(h + 1) * dv, j * ts:(j + 1) * ts] = (acc_ref[n] / l).astype(o_ref.dtype)

    def row_sums(p):
        return jnp.sum(p.reshape(tk // V7X_SUBLANES, V7X_SUBLANES, ts), axis=0)

    k_max = jnp.max(kn_ref[0, 0], axis=-1, keepdims=True)
    bounds = []
    for qt in qts:
        qf = qt.astype(f32)
        bounds.append(jnp.sqrt(jnp.sum(qf * qf, axis=0, keepdims=True)) * k_max)
    worst = functools.reduce(jnp.maximum, [jnp.max(u, axis=-1, keepdims=True) for u in bounds])
    safe = worst[0, 0] <= SAFE_LOGIT_BOUND

    @pl.when(safe)
    def _():
        row0 = lax.broadcasted_iota(i32, (K_PAD, ts), 0) == 0
        qes = [jnp.concatenate([qt, jnp.where(row0, -c, 0.0).astype(qt.dtype)], axis=0) for qt, c in zip(qts, bounds)]
        acc_ref[...] = jnp.zeros(acc_ref.shape, f32)
        l_ref[...] = jnp.zeros(l_ref.shape, f32)
        s0_ref[...] = _dot(k_chunk(0), qes[0])
        per_trip = math.gcd(n_chunks, chunks_per_trip)
        order = [(dc, n) for dc in range(per_trip) for n in range(len(streams))]

        def step(i, carry):
            c = per_trip * i
            kcs = [k_chunk(jnp.minimum(c + dc, n_chunks - 1)) for dc in range(per_trip + 1)]
            vcs = [v_ref[0, 0, c + dc] for dc in range(per_trip)]
            s_cur = s0_ref[...]
            for idx, (dc, n) in enumerate(order):
                dc2, n2 = order[idx + 1] if idx + 1 < len(order) else (per_trip, 0)
                s_next = _dot(kcs[dc2], qes[n2])
                p = jnp.exp2(s_cur)
                l_ref[n] += row_sums(p)
                acc_ref[n] += _dot(vcs[dc], p.astype(bf16))[:dv]
                s_cur = s_next
            s0_ref[...] = s_cur
            return carry

        lax.fori_loop(0, n_chunks // per_trip, step, 0)
        finish()

    @pl.when(jnp.logical_not(safe))
    def _():
        m_ref[...] = jnp.full(m_ref.shape, -jnp.inf, f32)
        acc_ref[...] = jnp.zeros(acc_ref.shape, f32)
        l_ref[...] = jnp.zeros(l_ref.shape, f32)
        qzs =[jnp.concatenate([qt, jnp.zeros((K_PAD, ts), qt.dtype)], axis=0) for qt in qts]

        def step(c, carry):
            kc = k_chunk(c)
            vc = v_ref[0, 0, c]
            for n, qz in enumerate(qzs):
                s = _dot(kc, qz)
                m = m_ref[n]
                m_new = jnp.maximum(m, jnp.max(s, axis=0, keepdims=True))
                p = jnp.exp2(s - m_new)
                alpha = jnp.exp2(m - m_new)
                l_ref[n] = l_ref[n] * alpha + row_sums(p)
                acc_ref[n] = acc_ref[n] * alpha + _dot(vc, p.astype(bf16))[:dv]
                m_ref[n] = m_new
            return carry

        lax.fori_loop(0, n_chunks, step, 0)
        finish()


def _attention(qt, k, kn, vt, *, group, tq, n_sub, chunks_per_trip):
    b, hq, dq, s = qt.shape
    _, hk, n_chunks, v_rows, tk = vt.shape
    dv = MLA_V_DIM
    assert tq % n_sub == 0
    n_streams, ts = group * n_sub, tq // n_sub
    return pl.pallas_call(
        functools.partial(_attn_kernel, tk=tk, n_chunks=n_chunks, dv=dv, n_sub=n_sub, chunks_per_trip=chunks_per_trip),
        grid=(b, hk, s // tq),
        scratch_shapes=[
            pltpu.VMEM((tk, ts), f32),
            pltpu.VMEM((n_streams, 1, ts), f32),
            pltpu.VMEM((n_streams, V7X_SUBLANES, ts), f32),
            pltpu.VMEM((n_streams, dv, ts), f32),
        ],
        in_specs=[
            pl.BlockSpec((1, group, dq, tq), lambda bi, g, i: (bi, g, 0, i)),
            pl.BlockSpec((1, 1, s, dq + K_PAD), lambda bi, g, i: (bi, g, 0, 0)),
            pl.BlockSpec((1, 1, 1, s), lambda bi, g, i: (bi, g, 0, 0)),
            pl.BlockSpec((1, 1, n_chunks, v_rows, tk), lambda bi, g, i: (bi, g, 0, 0, 0)),
        ],
        out_specs=pl.BlockSpec((1, group * dv, tq), lambda bi, g, i: (bi, g, i)),
        out_shape=jax.ShapeDtypeStruct((b, hq * dv, s), bf16),
        compiler_params=_cparams(("parallel", "parallel", "parallel"), V7X_VMEM_LIMIT_BYTES),
        name="attention",
    )(qt, k, kn, vt)


POOL_HALO = 16


def _pool_kernel(up_ref, uc_ref, un_ref, w_ref, sc_ref, o_ref, *, seq):
    i = pl.program_id(1)
    n_t = pl.num_programs(1)
    cur = uc_ref[0]
    tp = cur.shape[0]
    prev = jnp.where(i > 0, up_ref[0], 0.0)
    nxt = jnp.where(i < n_t - 1, un_ref[0], 0.0)
    ext = jnp.concatenate([prev, cur, nxt], axis=0)
    t = i * tp + lax.broadcasted_iota(i32, (tp, 1), 0)
    lane = lax.broadcasted_iota(i32, (tp, POOL_DIM), 1)
    run, length = ext, 1
    wsum = cnt = None
    for gi, win in enumerate(POOL_WINDOWS):
        while length < win:
            run = run[:run.shape[0] - length] + run[length:]
            length *= 2
        lo = win // 2
        hi = win - 1 - lo
        mine = run[POOL_HALO - lo:POOL_HALO - lo + tp]
        n_valid = (jnp.minimum(t + hi, seq - 1) - jnp.maximum(t - lo, 0) + 1).astype(f32)
        if gi == 0:
            wsum, cnt = mine, jnp.broadcast_to(n_valid, (tp, POOL_DIM))
        else:
            in_later_group = lane >= gi * POOL_CH
            wsum = jnp.where(in_later_group, mine, wsum)
            cnt = jnp.where(in_later_group, n_valid, cnt)
    p = (wsum / cnt - cur).astype(bf16)
    o_ref[0] = (_dot(p, w_ref[...]) * sc_ref[...]).astype(o_ref.dtype)


def _pool(u, w_bd, scale, *, tp):
    b, s, c = u.shape
    n_t = s // tp
    r = tp // POOL_HALO
    return pl.pallas_call(
        functools.partial(_pool_kernel, seq=s),
        grid=(b, n_t),
        in_specs=[
            pl.BlockSpec((1, POOL_HALO, c), lambda bi, i: (bi, jnp.maximum(i * r - 1, 0), 0)),
            pl.BlockSpec((1, tp, c), lambda bi, i: (bi, i, 0)),
            pl.BlockSpec((1, POOL_HALO, c), lambda bi, i: (bi, jnp.minimum((i + 1) * r, s // POOL_HALO - 1), 0)),
            pl.BlockSpec(w_bd.shape, lambda bi, i: (0, 0)),
            pl.BlockSpec((1, c), lambda bi, i: (0, 0)),
        ],
        out_specs=pl.BlockSpec((1, tp, c), lambda bi, i: (bi, i, 0)),
        out_shape=jax.ShapeDtypeStruct((b, s, c), bf16),
        compiler_params=_cparams(("parallel", "parallel")),
        name="pool",
    )(u, u, u, w_bd, scale)


def _outproj_kernel(x_ref, yp_ref, og_ref, om_ref, wop_ref, wog_ref, wom_ref, gt_ref, g2_ref, sh_ref, sc_ref,
                    wrt_ref, x1_ref, h2_ref, aff_ref):
    tn = (((0,), (0,)), ((), ()))
    n_e = aff_ref.shape[2]
    w_hi, w_lo = _split_bf16(wrt_ref[...])
    w_hilo = jnp.concatenate([w_hi, w_lo], axis=1)
    rows = x_ref.shape[1] // OUTPROJ_ROW_SPLIT
    for r in range(OUTPROJ_ROW_SPLIT):
        sl = pl.ds(r * rows, rows)
        ls = pl.ds(r * rows, rows)
        y = _dot(yp_ref[0, sl, :], wop_ref[...])
        y = y + lax.dot_general(og_ref[0, :, ls], wog_ref[...], tn, preferred_element_type=f32)
        y = y + lax.dot_general(om_ref[0, :, ls], wom_ref[...], tn, preferred_element_type=f32)
        x1 = x_ref[0, sl, :] + gt_ref[0] * y
        x1_ref[0, sl, :] = x1
        h = x1 * lax.rsqrt(jnp.mean(x1 * x1, axis=-1, keepdims=True) + EPS) * g2_ref[...]
        h = h * (1.0 + sc_ref[0]) + sh_ref[0]
        h2_ref[0, sl, :] = h.astype(bf16)
        h_hi, h_lo = _split_bf16(h)
        two = _dot(h_hi, w_hilo)
        logits = two[:, :n_e] + two[:, n_e:] + _dot(h_lo, w_hi)
        ex = jnp.exp(logits - jnp.max(logits, axis=-1, keepdims=True))
        aff_ref[0, sl, :] = ex / jnp.sum(ex, axis=-1, keepdims=True)


def _outproj(x, ypool, og, om, wop, wog, wom, gt1, g2, sh2, sc2, w_router, *, tt):
    b, s, d = x.shape
    full = lambda shape: pl.BlockSpec(shape, lambda bi, i: (0,) * len(shape))
    vec = pl.BlockSpec((1, 1, d), lambda bi, i: (bi, 0, 0))
    n_e = w_router.shape[1]
    return pl.pallas_call(
        _outproj_kernel,
        grid=(b, s // tt),
        in_specs=[
            pl.BlockSpec((1, tt, d), lambda bi, i: (bi, i, 0)),
            pl.BlockSpec((1, tt, POOL_DIM), lambda bi, i: (bi, i, 0)),
            pl.BlockSpec((1, og.shape[1], tt), lambda bi, i: (bi, 0, i)),
            pl.BlockSpec((1, om.shape[1], tt), lambda bi, i: (bi, 0, i)),
            full(wop.shape), full(wog.shape), full(wom.shape),
            vec, full((1, d)), vec, vec, full(w_router.shape),
        ],
        out_specs=(
            pl.BlockSpec((1, tt, d), lambda bi, i: (bi, i, 0)),
            pl.BlockSpec((1, tt, d), lambda bi, i: (bi, i, 0)),
            pl.BlockSpec((1, tt, n_e), lambda bi, i: (bi, i, 0)),
        ),
        out_shape=(
            jax.ShapeDtypeStruct((b, s, d), f32),
            jax.ShapeDtypeStruct((b, s, d), bf16),
            jax.ShapeDtypeStruct((b, s, n_e), f32),
        ),
        compiler_params=_cparams(("parallel", "parallel"), V7X_VMEM_LIMIT_BYTES),
        name="outproj",
    )(x, ypool, og, om, wop, wog, wom, gt1, g2, sh2, sc2, w_router)


def _route_kernel(a_ref, posm_ref, pos_ref, *, cap):
    a = a_ref[0]
    n_e, nc, ln = a.shape
    bits = pltpu.bitcast(a, i32)

    def count(mask):
        c = jnp.sum(jnp.where(mask, 1.0, 0.0), axis=2, keepdims=True)
        return jnp.sum(c, axis=1, keepdims=True)

    thr = jnp.zeros((n_e, 1, 1), i32)
    for bit in range(30, -1, -1):
        cand = thr | (1 << bit)
        thr = jnp.where(count(bits >= cand) >= cap, cand, thr)
    gt = bits > thr
    eq = bits == thr
    need = cap - count(gt)

    r_i = lax.broadcasted_iota(i32, (ln, ln), 0)
    c_i = lax.broadcasted_iota(i32, (ln, ln), 1)
    tri_incl = jnp.where(r_i <= c_i, 1.0, 0.0).astype(bf16)
    r_c = lax.broadcasted_iota(i32, (nc, nc), 0)
    c_c = lax.broadcasted_iota(i32, (nc, nc), 1)
    tri_strict = jnp.where(c_c < r_c, 1.0, 0.0).astype(bf16)

    def excl_prefix(mask):
        x = jnp.where(mask, 1.0, 0.0)
        incl = _dot(x.astype(bf16).reshape(n_e * nc, ln), tri_incl).reshape(n_e, nc, ln)
        tot = jnp.broadcast_to(incl[:, :, ln - 1:ln], (n_e, nc, ln))
        tot_hi = tot.astype(bf16)
        offs = [_dot(tri_strict, tot_hi[e]) for e in range(n_e)]
        return jnp.stack(offs, axis=0) + incl - x

    sel = gt | (eq & (excl_prefix(eq) < need))
    pos = excl_prefix(sel).astype(i32)
    pos_ref[0] = pos
    posm_ref[0] = jnp.where(sel, pos, -1)


def _route(aff_r, *, cap):
    b, n_e, nc, ln = aff_r.shape
    spec = pl.BlockSpec((1, n_e, nc, ln), lambda bi: (bi, 0, 0, 0))
    return pl.pallas_call(
        functools.partial(_route_kernel, cap=cap),
        grid=(b,),
        in_specs=[spec],
        out_specs=(spec, spec),
        out_shape=(jax.ShapeDtypeStruct(aff_r.shape, i32), jax.ShapeDtypeStruct(aff_r.shape, i32)),
        compiler_params=_cparams(("parallel",), V7X_VMEM_LIMIT_BYTES),
        name="route",
    )(aff_r)


def _window_start(lo, width, cap):
    aligned = lax.shift_left(lax.shift_right_logical(lo, SLOT_ALIGN_LOG2), SLOT_ALIGN_LOG2)
    return pl.multiple_of(jnp.minimum(aligned, cap - width), 1 << SLOT_ALIGN_LOG2)


def _window_widths(cap, usual=FAST_SLOTS):
    return min(usual, cap), min(TOK_CHUNK + (1 << SLOT_ALIGN_LOG2), cap)


def _gather_kernel(offs_ref, h_ref, posm_ref, aff_ref, xe_ref, gate_ref, *, n_off, n_chunks):
    b, e = pl.program_id(0), pl.program_id(1)
    base = (b * pl.num_programs(1) + e) * n_off
    step = TOK_CHUNK // V7X_LANES
    cap = xe_ref.shape[2]
    fast_w, slow_w = _window_widths(cap, usual=MXU_ROWS_PER_PUSH)
    xe_ref[...] = jnp.zeros_like(xe_ref)
    gate_ref[...] = jnp.zeros_like(gate_ref)

    def misfit(c, bad):
        lo = offs_ref[base + c * step]
        hi = offs_ref[base + (c + 1) * step]
        return bad + (hi - _window_start(lo, fast_w, cap) > fast_w).astype(i32)

    bad = lax.fori_loop(0, n_chunks, misfit, jnp.int32(0))

    def run(width):
        slot_iota = lax.broadcasted_iota(i32, (width, TOK_CHUNK), 0)

        def chunk(c, carry):
            w = _window_start(offs_ref[base + c * step], width, cap)
            tok0 = pl.multiple_of(c * TOK_CHUNK, TOK_CHUNK)
            pr = posm_ref[0, 0, c]
            hit = pr == slot_iota + w
            rows = _dot(jnp.where(hit, 1.0, 0.0).astype(bf16), h_ref[0, pl.ds(tok0, TOK_CHUNK), :])
            xe_ref[0, 0, pl.ds(w, width), :] = xe_ref[0, 0, pl.ds(w, width), :] + rows.astype(xe_ref.dtype)
            gate_ref[0, 0, pl.ds(w, width), :] += jnp.sum(jnp.where(hit, aff_ref[0, 0, c], 0.0), axis=1, keepdims=True)
            return carry

        lax.fori_loop(0, n_chunks, chunk, 0, unroll=math.gcd(n_chunks, GATHER_UNROLL))

    @pl.when(bad == 0)
    def _():
        run(fast_w)

    @pl.when(bad != 0)
    def _():
        run(slow_w)


def _gather(offs, h2, posm_c, aff_c, *, cap):
    b, s, d = h2.shape
    n_e = posm_c.shape[1]
    n_chunks = s // TOK_CHUNK
    n_off = s // V7X_LANES + 1
    chunked = pl.BlockSpec((1, 1, n_chunks, 1, TOK_CHUNK), lambda bi, e, offs: (bi, e, 0, 0, 0))
    return pl.pallas_call(
        functools.partial(_gather_kernel, n_off=n_off, n_chunks=n_chunks),
        grid_spec=pltpu.PrefetchScalarGridSpec(
            num_scalar_prefetch=1,
            grid=(b, n_e),
            in_specs=[
                pl.BlockSpec((1, s, d), lambda bi, e, offs: (bi, 0, 0), pipeline_mode=pl.Buffered(1)),
                chunked, chunked,
            ],
            out_specs=(pl.BlockSpec((1, 1, cap, d), lambda bi, e, offs: (bi, e, 0, 0)),
                       pl.BlockSpec((1, 1, cap, 1), lambda bi, e, offs: (bi, e, 0, 0))),
        ),
        out_shape=(jax.ShapeDtypeStruct((b, n_e, cap, d), bf16), jax.ShapeDtypeStruct((b, n_e, cap, 1), f32)),
        compiler_params=_cparams(("arbitrary", "arbitrary"), V7X_VMEM_LIMIT_BYTES),
        name="gather",
    )(offs, h2, posm_c, aff_c)


def _ffn_kernel(x_ref, gate_ref, wg_ref, wu_ref, wd_ref, o_ref, acc_ref):
    @pl.when(pl.program_id(2) == 0)
    def _():
        acc_ref[...] = jnp.zeros_like(acc_ref)

    wg, wu, wd = wg_ref[0, 0].astype(bf16), wu_ref[0, 0].astype(bf16), wd_ref[0, 0].astype(bf16)
    rows = x_ref.shape[2] // FFN_ROW_SPLIT
    for r in range(FFN_ROW_SPLIT):
        sl = pl.ds(r * rows, rows)
        x = x_ref[0, 0, sl, :]
        a = _dot(x, wg)
        u = _dot(x, wu)
        hmid = (a * (1.0 / (1.0 + jnp.exp(-a))) * u).astype(bf16)
        total = acc_ref[sl, :] + _dot(hmid, wd)
        acc_ref[sl, :] = total
        o_ref[0, 0, sl, :] = (total * gate_ref[0, 0, sl, :]).astype(o_ref.dtype)


def _ffn(xe, gate, w_gate, w_up, w_down, layer):
    b, n_e, cap, d = xe.shape
    d_ff = w_gate.shape[-1]
    n_f = d_ff // FF_TILE
    return pl.pallas_call(
        _ffn_kernel,
        grid=(n_e, b, n_f),
        in_specs=[
            pl.BlockSpec((1, 1, cap, d), lambda e, bi, f: (bi, e, 0, 0)),
            pl.BlockSpec((1, 1, cap, 1), lambda e, bi, f: (bi, e, 0, 0)),
            pl.BlockSpec((1, 1, d, FF_TILE), lambda e, bi, f: (layer, e, 0, f)),
            pl.BlockSpec((1, 1, d, FF_TILE), lambda e, bi, f: (layer, e, 0, f)),
            pl.BlockSpec((1, 1, FF_TILE, d), lambda e, bi, f: (layer, e, f, 0)),
        ],
        out_specs=pl.BlockSpec((1, 1, cap, d), lambda e, bi, f: (bi, e, 0, 0)),
        out_shape=jax.ShapeDtypeStruct((b, n_e, cap, d), bf16),
        scratch_shapes=[pltpu.VMEM((cap, d), f32)],
        compiler_params=_cparams(("parallel", "parallel", "arbitrary"), V7X_VMEM_LIMIT_BYTES),
        name="expert_ffn",
    )(xe, gate, w_gate, w_up, w_down)


def _combine_kernel(offs_ref, x_ref, posm_ref, gt_ref, ye_ref, o_ref, *, n_off):
    for sub in range(x_ref.shape[1] // TOK_CHUNK):
        _combine_chunk(offs_ref, x_ref, posm_ref, gt_ref, ye_ref, o_ref, n_off=n_off, sub=sub)


def _combine_chunk(offs_ref, x_ref, posm_ref, gt_ref, ye_ref, o_ref, *, n_off, sub):
    b = pl.program_id(0)
    i = pl.program_id(2) * (x_ref.shape[1] // TOK_CHUNK) + sub
    tok = pl.ds(sub * TOK_CHUNK, TOK_CHUNK)
    n_e, cap = ye_ref.shape[1], ye_ref.shape[2]
    fast_w, slow_w = _window_widths(cap)
    step = TOK_CHUNK // V7X_LANES
    posm = posm_ref[0, tok, :]
    los = [offs_ref[(b * n_e + e) * n_off + i * step] for e in range(n_e)]
    his = [offs_ref[(b * n_e + e) * n_off + (i + 1) * step] for e in range(n_e)]
    bad = functools.reduce(
        lambda a, c: a + c, [(his[e] - _window_start(los[e], fast_w, cap) > fast_w).astype(i32) for e in range(n_e)])
    paired = 2 * fast_w == V7X_LANES and n_e % 2 == 0

    def run_stacked():
        lane = lax.broadcasted_iota(i32, (TOK_CHUNK, V7X_LANES), 1)
        ws = [_window_start(los[e], fast_w, cap) for e in range(n_e)]
        tiles, rows = [], []
        for e in range(0, n_e, 2):
            target = jnp.where(lane < fast_w, posm[:, e:e + 1] - ws[e], posm[:, e + 1:e + 2] - ws[e + 1] + fast_w)
            tiles.append(jnp.where(target == lane, 1.0, 0.0).astype(bf16))
            rows += [ye_ref[0, e, pl.ds(ws[e], fast_w), :], ye_ref[0, e + 1, pl.ds(ws[e + 1], fast_w), :]]
        total = _dot(jnp.concatenate(tiles, axis=1), jnp.concatenate(rows, axis=0))
        o_ref[0, tok, :] = x_ref[0, tok, :] + gt_ref[0] * total

    def run_per_expert(width):
        slot_iota = lax.broadcasted_iota(i32, (TOK_CHUNK, width), 1)
        total = jnp.zeros((TOK_CHUNK, o_ref.shape[2]), f32)
        for e in range(n_e):
            w = _window_start(los[e], width, cap)
            onehot = jnp.where(posm[:, e:e + 1] == slot_iota + w, 1.0, 0.0).astype(bf16)
            total = total + _dot(onehot, ye_ref[0, e, pl.ds(w, width), :])
        o_ref[0, tok, :] = x_ref[0, tok, :] + gt_ref[0] * total

    @pl.when(bad == 0)
    def _():
        run_stacked() if paired else run_per_expert(fast_w)

    @pl.when(bad != 0)
    def _():
        run_per_expert(slow_w)


def _combine(offs, x1, posm_t, gt2, ye):
    b, s, d = x1.shape
    n_e, cap = ye.shape[1], ye.shape[2]
    n_off = s // V7X_LANES + 1
    tok = math.gcd(s, COMBINE_TOK)
    return pl.pallas_call(
        functools.partial(_combine_kernel, n_off=n_off),
        grid_spec=pltpu.PrefetchScalarGridSpec(
            num_scalar_prefetch=1,
            grid=(b, d // COL_TILE, s // tok),
            in_specs=[
                pl.BlockSpec((1, tok, COL_TILE), lambda bi, j, i, offs: (bi, i, j)),
                pl.BlockSpec((1, tok, n_e), lambda bi, j, i, offs: (bi, i, 0)),
                pl.BlockSpec((1, 1, COL_TILE), lambda bi, j, i, offs: (bi, 0, j)),
                pl.BlockSpec((1, n_e, cap, COL_TILE), lambda bi, j, i, offs: (bi, 0, 0, j),
                             pipeline_mode=pl.Buffered(1)),
            ],
            out_specs=pl.BlockSpec((1, tok, COL_TILE), lambda bi, j, i, offs: (bi, i, j)),
        ),
        out_shape=jax.ShapeDtypeStruct((b, s, d), f32),
        compiler_params=_cparams(("arbitrary", "arbitrary", "arbitrary"), V7X_VMEM_LIMIT_BYTES),
        name="combine",
    )(offs, x1, posm_t, gt2, ye)


def _final_kernel(x_ref, g_ref, o_ref):
    x = x_ref[0]
    o_ref[0] = x * lax.rsqrt(jnp.mean(x * x, axis=-1, keepdims=True) + EPS) * g_ref[...]


def _final_norm(x, g, *, tt):
    b, s, d = x.shape
    return pl.pallas_call(
        _final_kernel,
        grid=(b, s // tt),
        in_specs=[pl.BlockSpec((1, tt, d), lambda bi, i: (bi, i, 0)), pl.BlockSpec((1, d), lambda bi, i: (0, 0))],
        out_specs=pl.BlockSpec((1, tt, d), lambda bi, i: (bi, i, 0)),
        out_shape=jax.ShapeDtypeStruct((b, s, d), f32),
        compiler_params=_cparams(("parallel", "parallel")),
        name="final_norm",
    )(x, g)


def _deinterleave(n):
    return np.concatenate([np.arange(0, n, 2), np.arange(1, n, 2)])


def _rope_tables_t(n, d_rot):
    n_rows = n // GRID_W
    row = jnp.repeat(jnp.arange(n_rows, dtype=f32), GRID_W)
    col = jnp.tile(jnp.arange(GRID_W, dtype=f32), n_rows)
    n_freq = d_rot // 4
    inv_freq = ROPE_THETA ** (-jnp.arange(n_freq, dtype=f32) / n_freq)
    ang = jnp.concatenate([row[:, None] * inv_freq, col[:, None] * inv_freq], axis=-1)
    return jnp.cos(ang).T, jnp.sin(ang).T


def _rest_columns():
    p64, p32 = _deinterleave(HEAD_DIM), _deinterleave(MLA_ROPE_DIM)
    cols = [OFF_GQA_Q + h * HEAD_DIM + p64 for h in range(GQA_HEADS)]
    cols += [OFF_GQA_K + h * HEAD_DIM + p64 for h in range(GQA_KV_HEADS)]
    cols += [np.arange(OFF_GQA_V, IN_DIM - MLA_ROPE_DIM), OFF_MLA_ROPE + p32]
    return np.concatenate(cols)


def _uq_columns():
    p32 = _deinterleave(MLA_ROPE_DIM)
    cols = []
    for h in range(MLA_HEADS):
        cols += [h * MLA_QK_DIM + np.arange(MLA_NOPE_DIM), h * MLA_QK_DIM + MLA_NOPE_DIM + p32]
    return np.concatenate(cols)


def _block_diag(w):
    depth, g, c, _ = w.shape
    rows = []
    for i in range(g):
        blocks = [w[:, i] if j == i else jnp.zeros((depth, c, c), w.dtype) for j in range(g)]
        rows.append(jnp.concatenate(blocks, axis=2))
    return jnp.concatenate(rows, axis=1)


def _trunk(x, c, w_mod, b_mod, g_norm1, w_in, pool_w, pool_scale, gqa_q_gain, gqa_k_gain, mla_q_gain, mla_kv_gain,
           mla_w_uq, mla_w_ukv, w_out, g_norm2, w_router, w_gate, w_up, w_down, g_final):
    b, s, d = x.shape
    depth = w_mod.shape[0]
    cap = (EC_CAPACITY * s) // N_EXPERTS
    tt = min(TOK_TILE, s)
    tq = min(ATT_TQ, s)
    tk = min(ATT_TK, s)
    nc = s // V7X_LANES

    mod_rows = 8
    c_pad = jnp.zeros((mod_rows, d), f32).at[:b].set(c)
    mod = _modulation(c_pad, w_mod, b_mod)[:, :b].reshape(depth, b, 6, 1, d)

    cg, sg = _rope_tables_t(s, HEAD_DIM)
    cm, sm = _rope_tables_t(s, MLA_ROPE_DIM)
    p64 = _deinterleave(HEAD_DIM)
    rest_cols, uq_cols = _rest_columns(), _uq_columns()

    wp_all = w_in[:, :, :POOL_DIM].astype(bf16)
    wr_all = jnp.swapaxes(w_in[:, :, rest_cols], 1, 2).astype(bf16)
    wuq_all = jnp.swapaxes(mla_w_uq[:, :, uq_cols], 1, 2).astype(bf16)
    wukv_all = jnp.swapaxes(mla_w_ukv, 1, 2).astype(bf16)
    gq_all, gk_all = gqa_q_gain[:, p64, None], gqa_k_gain[:, p64, None]
    gmq_all, gmkv_all = mla_q_gain[:, :, None], mla_kv_gain[:, :, None]
    pw_all = _block_diag(pool_w).astype(bf16)
    wo_all = w_out.astype(bf16)
    n_g = GQA_HEADS * HEAD_DIM

    for l in range(depth):
        sh1, sc1, gt1, sh2, sc2, gt2 = (mod[l, :, k] for k in range(6))
        u, qg, kg, kng, vg, qm, km, knm, vm = _inproj(
            x, g_norm1[l][None], sh1, sc1, wp_all[l], wr_all[l], gq_all[l], gk_all[l], gmq_all[l], gmkv_all[l],
            wuq_all[l], wukv_all[l], cg, sg, cm, sm, tt=max(min(INPROJ_TILE, s), tk), tk=tk)
        og = _attention(qg, kg, kng, vg, group=GQA_GROUP, tq=min(GQA_TQ, s), n_sub=min(GQA_TQ, s) // tq,
                        chunks_per_trip=GQA_CHUNKS_PER_TRIP)
        om = _attention(qm, km, knm, vm, group=1, tq=min(MLA_TQ, s), n_sub=min(MLA_TQ, s) // tq,
                        chunks_per_trip=MLA_CHUNKS_PER_TRIP)
        ypool = _pool(u, pw_all[l], pool_scale[l][None], tp=tt)
        wo = wo_all[l]
        x1, h2, aff = _outproj(x, ypool, og, om, wo[:POOL_DIM], wo[POOL_DIM:POOL_DIM + n_g], wo[POOL_DIM + n_g:],
                               gt1, g_norm2[l][None], sh2, sc2, w_router[l], tt=min(OUTPROJ_TILE, s))

        aff_r = aff.transpose(0, 2, 1).reshape(b, N_EXPERTS, nc, V7X_LANES)
        posm, pos = _route(aff_r, cap=cap)
        offs = jnp.concatenate([pos[..., 0], jnp.full((b, N_EXPERTS, 1), cap, i32)], axis=-1).reshape(-1)
        posm_c = posm.reshape(b, N_EXPERTS, s // TOK_CHUNK, 1, TOK_CHUNK)
        posm_t = posm.reshape(b, N_EXPERTS, s).transpose(0, 2, 1)
        aff_c = aff_r.reshape(b, N_EXPERTS, s // TOK_CHUNK, 1, TOK_CHUNK)
        xe, gate = _gather(offs, h2, posm_c, aff_c, cap=cap)
        ye = _ffn(xe, gate, w_gate, w_up, w_down, l)
        x = _combine(offs, x1, posm_t, gt2, ye)
    return _final_norm(x, g_final[None], tt=tt)


def kernel(x, c, w_mod, b_mod, g_norm1, w_in, pool_w, pool_scale, gqa_q_gain, gqa_k_gain, mla_q_gain, mla_kv_gain,
           mla_w_uq, mla_w_ukv, w_out, g_norm2, w_router, w_gate, w_up, w_down, g_final):
    return _trunk(x, c, w_mod, b_mod, g_norm1, w_in, pool_w, pool_scale, gqa_q_gain, gqa_k_gain, mla_q_gain,
                  mla_kv_gain, mla_w_uq, mla_w_ukv, w_out, g_norm2, w_router, w_gate, w_up, w_down, g_final)
```
